```python
import jax, jax.numpy as jnp
from jax import lax
import numpy as np

D_MODEL = 1024
BATCH = 8
SEQ = 2048
DEPTH = 2
DEC_BATCH = 128
DEC_SEQ = 1
PAST_LEN = 16384
PAGE_SIZE = 128

CHUNK = 128
D_CM = D_MODEL
CM_GROUPS = 8
CM_GROUP_DIM = D_CM // CM_GROUPS
EXPAND = 2
D_INNER = EXPAND * D_MODEL
SSD_HEAD_DIM = 64
SSD_HEADS = D_INNER // SSD_HEAD_DIM
SSD_GROUPS = 4
SSD_STATE = 128
CONV_K = 4
CONV_DIM = D_INNER + 2 * SSD_GROUPS * SSD_STATE
D_FF = ((8 * D_MODEL + 3 * 256 - 1) // (3 * 256)) * 256
IN_COLS = 2 * D_CM + D_INNER + CONV_DIM + SSD_HEADS + 2 * D_MODEL
ALPHA = (2 * DEPTH) ** 0.25
BETA = (8 * DEPTH) ** -0.25
LN_EPS = 1e-5

kernel_name = "gated_chunkmlp_ssd_deepnorm_adaln_step"


def layer_norm(x, g, b):
    xf = x.astype(jnp.float32)
    mu = jnp.mean(xf, axis=-1, keepdims=True)
    var = jnp.mean(jnp.square(xf - mu), axis=-1, keepdims=True)
    y = (xf - mu) * lax.rsqrt(var + LN_EPS)
    return (y * g.astype(jnp.float32) + b.astype(jnp.float32)).astype(x.dtype)


def rms_norm(x, g):
    xf = x.astype(jnp.float32)
    y = xf * lax.rsqrt(jnp.mean(jnp.square(xf), axis=-1, keepdims=True) + LN_EPS)
    return (y * g.astype(jnp.float32)).astype(x.dtype)


def causal_dwconv(xp, w, b):
    L = xp.shape[1] - CONV_K + 1
    out = b
    for k in range(CONV_K):
        out = out + xp[:, k:k + L] * w[k]
    return out


def chunk_spatial_mix(v, w_s, b_s):
    Bsz, L, _ = v.shape
    ch = min(CHUNK, L)
    nc = L // ch
    mask = jnp.tril(jnp.ones((ch, ch), dtype=bool))
    w = jnp.where(mask[None], w_s[:, :ch, :ch], 0.0).astype(v.dtype)
    bias = b_s[:, :ch].astype(v.dtype)
    v4 = v.reshape(Bsz, nc, ch, CM_GROUPS, CM_GROUP_DIM)
    out = jnp.einsum('gij,bcjgd->bcigd', w, v4) + bias.T[None, None, :, :, None]
    return out.reshape(Bsz, L, D_CM)


def ssd_scan(x, dt, a, bm, cm, h0):
    Bsz, L, H, P = x.shape
    G, N = SSD_GROUPS, SSD_STATE
    E = H // G
    ch = min(CHUNK, L)
    nc = L // ch
    f32 = jnp.float32
    xdt = (x.astype(f32) * dt[..., None]).reshape(Bsz, nc, ch, G, E, P)
    bmf = bm.astype(f32).reshape(Bsz, nc, ch, G, N)
    cmf = cm.astype(f32).reshape(Bsz, nc, ch, G, N)
    acum = jnp.cumsum((dt * a).reshape(Bsz, nc, ch, G, E), axis=2)
    seg = acum[:, :, :, None] - acum[:, :, None, :]
    causal = jnp.tril(jnp.ones((ch, ch), dtype=bool))[None, None, :, :, None, None]
    lmat = jnp.exp(jnp.where(causal, seg, -jnp.inf))
    cb = jnp.einsum('bclgn,bcsgn->bclsg', cmf, bmf)
    y_diag = jnp.einsum('bclsg,bclsge,bcsgep->bclgep', cb, lmat, xdt)
    decay_to_end = jnp.exp(acum[:, :, -1:] - acum)
    chunk_states = jnp.einsum('bclgn,bclge,bclgep->bcgepn', bmf, decay_to_end, xdt)
    chunk_decay = jnp.exp(acum[:, :, -1])

    def step(h, inp):
        dec, st = inp
        return h * dec[..., None, None] + st, h

    h_init = h0.astype(f32).reshape(Bsz, G, E, P, N)
    h_last, h_prev = lax.scan(step, h_init,
                              (jnp.moveaxis(chunk_decay, 1, 0), jnp.moveaxis(chunk_states, 1, 0)))
    y_off = jnp.einsum('bclgn,bclge,cbgepn->bclgep', cmf, jnp.exp(acum), h_prev)
    y = (y_diag + y_off).reshape(Bsz, L, H, P).astype(x.dtype)
    return y, h_last.reshape(Bsz, H, P, N)


def block(x, c, conv_buf, h0, lp):
    Bsz, L, _ = x.shape
    mod = jax.nn.silu(c) @ lp['w_ada'] + lp['b_ada']
    sh1, sc1, g1, sh2, sc2, g2 = jnp.split(mod[:, None, :], 6, axis=-1)
    h = x * (1 + sc1) + sh1
    proj = h @ lp['w_in']
    cuts = np.cumsum([D_CM, D_CM, D_INNER, CONV_DIM, SSD_HEADS, D_MODEL]).tolist()
    u, v, z, xbc, dt_raw, gate_cm, gate_ssd = jnp.split(proj, cuts, axis=-1)

    u = jax.nn.gelu(u, approximate=False)
    v = layer_norm(jax.nn.gelu(v, approximate=False), lp['ln_v_g'], lp['ln_v_b'])
    out_cm = u * chunk_spatial_mix(v, lp['w_spatial'], lp['b_spatial'])

    xbc_full = jnp.concatenate([conv_buf.astype(xbc.dtype), xbc], axis=1)
    new_conv = xbc_full[:, xbc_full.shape[1] - (CONV_K - 1):]
    xbc_c = jax.nn.silu(causal_dwconv(xbc_full, lp['conv_w'], lp['conv_b']))
    xs, bm, cmat = jnp.split(xbc_c, [D_INNER, D_INNER + SSD_GROUPS * SSD_STATE], axis=-1)
    xs = xs.reshape(Bsz, L, SSD_HEADS, SSD_HEAD_DIM)
    bm = bm.reshape(Bsz, L, SSD_GROUPS, SSD_STATE)
    cmat = cmat.reshape(Bsz, L, SSD_GROUPS, SSD_STATE)
    dt = jax.nn.softplus((dt_raw + lp['dt_bias']).astype(jnp.float32))
    a = -jnp.exp(lp['a_log'].astype(jnp.float32))
    y, h_last = ssd_scan(xs, dt, a, bm, cmat, h0)
    y = y + lp['d_skip'][:, None] * xs
    y = rms_norm(y.reshape(Bsz, L, D_INNER) * jax.nn.silu(z), lp['ssd_norm_w'])

    merged = (jax.nn.sigmoid(gate_cm + lp['b_gate'][:D_MODEL]) * (out_cm @ lp['w_cm_br'])
              + jax.nn.sigmoid(gate_ssd + lp['b_gate'][D_MODEL:]) * (y @ lp['w_ssd_br']))
    x = layer_norm(ALPHA * x + g1 * (merged @ lp['w_o']), lp['ln1_g'], lp['ln1_b'])

    h2 = x * (1 + sc2) + sh2
    ffn = (jax.nn.silu(h2 @ lp['w_ffn_gate']) * (h2 @ lp['w_ffn_up'])) @ lp['w_ffn_down']
    x = layer_norm(ALPHA * x + g2 * ffn, lp['ln2_g'], lp['ln2_b'])
    return x, new_conv, h_last.astype(h0.dtype), v


def setup_inputs(seed: int = 0) -> dict:
    key = jax.random.key(seed)
    ks = iter(jax.random.split(key, 40))
    f32 = jnp.float32

    def nrm(shape, s):
        return jax.random.normal(next(ks), shape, f32) * s

    dt0 = jnp.exp(jax.random.uniform(next(ks), (DEPTH, SSD_HEADS), f32,
                                     np.log(1e-3).astype(np.float32), np.log(1e-1).astype(np.float32)))
    return {
        'x_prompt': nrm((BATCH, SEQ, D_MODEL), 1.0),
        'x_sample': nrm((DEC_BATCH, DEC_SEQ, D_MODEL), 1.0),
        'state_ssd': nrm((DEPTH, DEC_BATCH, SSD_HEADS, SSD_HEAD_DIM, SSD_STATE), 0.1),
        'state_conv': nrm((DEPTH, DEC_BATCH, CONV_K - 1, CONV_DIM), 1.0),
        'c_prompt': nrm((BATCH, D_MODEL), 1.0),
        'c_sample': nrm((DEC_BATCH, D_MODEL), 1.0),
        'w_ada': nrm((DEPTH, D_MODEL, 6 * D_MODEL), 0.5 * D_MODEL ** -0.5),
        'b_ada': nrm((DEPTH, 6 * D_MODEL), 0.02),
        'w_in': nrm((DEPTH, D_MODEL, IN_COLS), D_MODEL ** -0.5),
        'b_gate': nrm((DEPTH, 2 * D_MODEL), 0.02),
        'ln_v_g': 1.0 + nrm((DEPTH, D_CM), 0.02),
        'ln_v_b': nrm((DEPTH, D_CM), 0.02),
        'w_spatial': nrm((DEPTH, CM_GROUPS, CHUNK, CHUNK), 0.5 * CHUNK ** -0.5),
        'b_spatial': 1.0 + nrm((DEPTH, CM_GROUPS, CHUNK), 0.02),
        'conv_w': nrm((DEPTH, CONV_K, CONV_DIM), CONV_K ** -0.5),
        'conv_b': nrm((DEPTH, CONV_DIM), 0.02),
        'dt_bias': dt0 + jnp.log(-jnp.expm1(-dt0)),
        'a_log': jnp.log(jax.random.uniform(next(ks), (DEPTH, SSD_HEADS), f32, 1.0, 16.0)),
        'd_skip': 1.0 + nrm((DEPTH, SSD_HEADS), 0.02),
        'ssd_norm_w': 1.0 + nrm((DEPTH, D_INNER), 0.02),
        'w_cm_br': nrm((DEPTH, D_CM, D_MODEL), D_CM ** -0.5),
        'w_ssd_br': nrm((DEPTH, D_INNER, D_MODEL), D_INNER ** -0.5),
        'w_o': nrm((DEPTH, D_MODEL, D_MODEL), BETA * D_MODEL ** -0.5),
        'ln1_g': 1.0 + nrm((DEPTH, D_MODEL), 0.02),
        'ln1_b': nrm((DEPTH, D_MODEL), 0.02),
        'w_ffn_gate': nrm((DEPTH, D_MODEL, D_FF), D_MODEL ** -0.5),
        'w_ffn_up': nrm((DEPTH, D_MODEL, D_FF), D_MODEL ** -0.5),
        'w_ffn_down': nrm((DEPTH, D_FF, D_MODEL), BETA * D_FF ** -0.5),
        'ln2_g': 1.0 + nrm((DEPTH, D_MODEL), 0.02),
        'ln2_b': nrm((DEPTH, D_MODEL), 0.02),
    }


def reference(x_prompt, x_sample, state_ssd, state_conv, c_prompt, c_sample,
              w_ada, b_ada, w_in, b_gate, ln_v_g, ln_v_b, w_spatial, b_spatial,
              conv_w, conv_b, dt_bias, a_log, d_skip, ssd_norm_w, w_cm_br, w_ssd_br,
              w_o, ln1_g, ln1_b, w_ffn_gate, w_ffn_up, w_ffn_down, ln2_g, ln2_b):
    def layer_params(i):
        return {'w_ada': w_ada[i], 'b_ada': b_ada[i], 'w_in': w_in[i], 'b_gate': b_gate[i],
                'ln_v_g': ln_v_g[i], 'ln_v_b': ln_v_b[i], 'w_spatial': w_spatial[i],
                'b_spatial': b_spatial[i], 'conv_w': conv_w[i], 'conv_b': conv_b[i],
                'dt_bias': dt_bias[i], 'a_log': a_log[i], 'd_skip': d_skip[i],
                'ssd_norm_w': ssd_norm_w[i], 'w_cm_br': w_cm_br[i], 'w_ssd_br': w_ssd_br[i],
                'w_o': w_o[i], 'ln1_g': ln1_g[i], 'ln1_b': ln1_b[i], 'w_ffn_gate': w_ffn_gate[i],
                'w_ffn_up': w_ffn_up[i], 'w_ffn_down': w_ffn_down[i], 'ln2_g': ln2_g[i],
                'ln2_b': ln2_b[i]}

    xp = x_prompt
    conv0_p = jnp.zeros((BATCH, CONV_K - 1, CONV_DIM), x_prompt.dtype)
    h0_p = jnp.zeros((BATCH, SSD_HEADS, SSD_HEAD_DIM, SSD_STATE), state_ssd.dtype)
    conv_p, ssd_p = [], []
    xs = x_sample
    conv_s, ssd_s, v_s = [], [], []
    for i in range(DEPTH):
        lp = layer_params(i)
        xp, cp, hp, _ = block(xp, c_prompt, conv0_p, h0_p, lp)
        conv_p.append(cp)
        ssd_p.append(hp)
        xs, cs, hs, vs = block(xs, c_sample, state_conv[i], state_ssd[i], lp)
        conv_s.append(cs)
        ssd_s.append(hs)
        v_s.append(vs)
    return (xp, xs, jnp.stack(ssd_p), jnp.stack(conv_p), jnp.stack(ssd_s), jnp.stack(conv_s), jnp.stack(v_s))
```

```python
import functools
import math

import numpy as np
import jax
import jax.numpy as jnp
from jax import lax
from jax.experimental import pallas as pl
from jax.experimental.pallas import tpu as pltpu

D_MODEL = 1024
BATCH = 8
SEQ = 2048
DEPTH = 2
DEC_BATCH = 128
CHUNK = 128
D_CM = D_MODEL
CM_GROUPS = 8
D_INNER = 2048
HEAD_DIM = 64
HEADS = 32
SSD_GROUPS = 4
HEADS_PER_GROUP = HEADS // SSD_GROUPS
STATE = 128
CONV_K = 4
CONV_DIM = D_INNER + 2 * SSD_GROUPS * STATE
D_FF = 2816
ALPHA = (2 * DEPTH) ** 0.25
LN_EPS = 1e-5

LANES = 128
DT_COPIES = 3
C_U = 0
C_V = C_U + D_CM
C_Z = C_V + D_CM
C_XBC = C_Z + D_INNER
C_GCM = C_XBC + CONV_DIM
C_GSSD = C_GCM + D_MODEL
C_DT = C_GSSD + D_MODEL
IN_COLS_P = C_DT + LANES

TILE_L = 256
TILE_F = 512
VMEM_LIMIT = 56 * 1024 * 1024

F32 = jnp.float32
BF16 = jnp.bfloat16


def _bdot(a, b):
    return jnp.dot(a, b, preferred_element_type=F32)


def _gelu(x):
    return 0.5 * x * (1.0 + lax.erf(x * np.float32(math.sqrt(0.5))))


def _sigmoid(x):
    return 1.0 / (1.0 + jnp.exp(-x))


def _silu(x):
    return x * _sigmoid(x)


def _softplus(x):
    return jnp.maximum(x, 0.0) + jnp.log1p(jnp.exp(-jnp.abs(x)))


def _layer_norm(x, g, b):
    mu = jnp.mean(x, axis=-1, keepdims=True)
    xc = x - mu
    var = jnp.mean(xc * xc, axis=-1, keepdims=True)
    return xc * lax.rsqrt(var + LN_EPS) * g + b


def _rms_norm(x, g):
    return x * lax.rsqrt(jnp.mean(x * x, axis=-1, keepdims=True) + LN_EPS) * g


def _pack3(x):
    hi = x.astype(BF16).astype(F32)
    r1 = x - hi
    mid = r1.astype(BF16).astype(F32)
    r2 = r1 - mid
    lane = lax.broadcasted_iota(jnp.int32, x.shape, 1)
    return jnp.where(lane < HEADS, x, jnp.where(lane < 2 * HEADS, r1, r2)).astype(BF16)


def _const_spec(shape, single=True):
    nd = len(shape)
    kw = {"pipeline_mode": pl.Buffered(1)} if single else {}
    return pl.BlockSpec(shape, lambda *_: (0,) * nd, **kw)


def _mod_kernel(c_ref, w_ref, b_ref, o_ref):
    s = _silu(c_ref[...])
    o_ref[...] = jnp.dot(s, w_ref[...], preferred_element_type=F32,
                         precision=lax.Precision.HIGHEST) + b_ref[...]


def _adaln_mod(c_all, w_ada, b_ada):
    rows = c_all.shape[0]
    nblk = 6
    return pl.pallas_call(
        _mod_kernel,
        grid=(DEPTH, nblk),
        in_specs=[
            pl.BlockSpec((rows, D_MODEL), lambda l, k: (0, 0)),
            pl.BlockSpec((None, D_MODEL, D_MODEL), lambda l, k: (l, 0, k)),
            pl.BlockSpec((None, 1, D_MODEL), lambda l, k: (l, 0, k)),
        ],
        out_specs=pl.BlockSpec((None, rows, D_MODEL), lambda l, k: (l, 0, k)),
        out_shape=jax.ShapeDtypeStruct((DEPTH, rows, nblk * D_MODEL), F32),
        compiler_params=pltpu.CompilerParams(
            dimension_semantics=("parallel", "parallel"), vmem_limit_bytes=VMEM_LIMIT),
        name="adaln_mod",
    )(c_all, w_ada, b_ada.reshape(DEPTH, 1, nblk * D_MODEL))


def _cm_branch(h, u_v, lnvg, lnvb):
    u = _gelu(u_v[:, :D_CM])
    v = _layer_norm(_gelu(u_v[:, D_CM:]), lnvg, lnvb)
    return u, v


def _post_mixer(x, g1, merged_bf16, wo_ref, ln1g, ln1b):
    o = _bdot(merged_bf16, wo_ref[...])
    return _layer_norm(ALPHA * x + g1 * o, ln1g, ln1b)


def _ffn(x1, sh2, sc2, g2, wg_ref, wu_ref, wd_ref, ln2g, ln2b):
    h2 = (x1 * (1.0 + sc2) + sh2).astype(BF16)
    gate = _bdot(h2, wg_ref[...])
    up = _bdot(h2, wu_ref[...])
    act = (_silu(gate) * up).astype(BF16)
    ffn = _bdot(act, wd_ref[...])
    return _layer_norm(ALPHA * x1 + g2 * ffn, ln2g, ln2b)


def _ssd_chunk(xs_c, b_c, c_c, dtseg, dtb, alog, tri, e64, e128, state_ref, y_ref, r0):
    row = lax.broadcasted_iota(jnp.int32, (CHUNK, LANES), 0)
    lane = lax.broadcasted_iota(jnp.int32, (CHUNK, LANES), 1)
    tril = row >= lane
    low_half = lane < HEAD_DIM

    dt = _softplus(dtseg + dtb)
    a = -jnp.exp(alog)
    da = dt * a
    hi = da.astype(BF16)
    r1 = da - hi.astype(F32)
    mid = r1.astype(BF16)
    lo = (r1 - mid.astype(F32)).astype(BF16)
    acum = _bdot(tri, hi) + _bdot(tri, mid) + _bdot(tri, lo)
    acum_t = acum.T
    dt_t = dt.T
    alast = acum[CHUNK - 1:CHUNK, :]
    w = dt * jnp.exp(alast - acum)
    wexp = _bdot(_pack3(w), e64)
    cd = jnp.broadcast_to(jnp.exp(alast), (8, LANES))
    cdexp = _bdot(_pack3(cd), e64)[0:1, :]
    abc = _bdot(_pack3(acum), e128)

    for g in range(SSD_GROUPS):
        bg = b_c[:, g * STATE:(g + 1) * STATE]
        cg = c_c[:, g * STATE:(g + 1) * STATE]
        bgb = bg.astype(BF16)
        cb = lax.dot_general(cg.astype(BF16), bgb, (((1,), (1,)), ((), ())),
                             preferred_element_type=F32)
        for pr in range(HEADS_PER_GROUP // 2):
            h0 = g * HEADS_PER_GROUP + 2 * pr
            pieces = []
            for h in (h0, h0 + 1):
                acol = abc[:, h * LANES:(h + 1) * LANES]
                seg = acol - acum_t[h:h + 1, :]
                lh = jnp.exp(jnp.where(tril, seg, -jnp.inf))
                mh = cb * lh * dt_t[h:h + 1, :]
                ch = cg * jnp.exp(acol)
                pieces += [mh.astype(BF16), ch.astype(BF16)]
            lhs = jnp.concatenate(pieces, axis=1)
            pc = h0 * HEAD_DIM
            xs_pair = xs_c[:, pc:pc + LANES]
            st_pair = state_ref[:, pc:pc + LANES]
            rhs = jnp.concatenate(
                [jnp.where(low_half, xs_pair, 0.0), jnp.where(low_half, st_pair, 0.0),
                 jnp.where(low_half, 0.0, xs_pair), jnp.where(low_half, 0.0, st_pair)],
                axis=0).astype(BF16)
            y_ref[r0:r0 + CHUNK, pc:pc + LANES] = _bdot(lhs, rhs)
        gc = g * HEADS_PER_GROUP * HEAD_DIM
        gw = HEADS_PER_GROUP * HEAD_DIM
        xw = (xs_c[:, gc:gc + gw] * wexp[:, gc:gc + gw]).astype(BF16)
        upd = _bdot(bg.T.astype(BF16), xw)
        state_ref[:, gc:gc + gw] = state_ref[:, gc:gc + gw] * cdexp[:, gc:gc + gw] + upd


def _p1_kernel(x_ref, mod_ref, win_ref, bg_ref, lnvg_ref, lnvb_ref, wsp_ref, bsp_ref,
               convw_ref, convb_ref, dtb_ref, alog_ref, dskip_ref, normw_ref,
               wcm_ref, wssd_ref, wo_ref, ln1g_ref, ln1b_ref, tri_ref, e64_ref, e128_ref,
               x1_ref, ssd_ref, conv_ref,
               state_ref, cbuf_ref, y_ref, *, tl):
    j = pl.program_id(1)
    last = pl.num_programs(1) - 1
    nc = tl // CHUNK

    @pl.when(j == 0)
    def _():
        state_ref[...] = jnp.zeros_like(state_ref)
        cbuf_ref[0:8, :] = jnp.zeros((8, CONV_DIM), F32)

    x = x_ref[...]
    sh1 = mod_ref[0:1, :]
    sc1 = mod_ref[1:2, :]
    g1 = mod_ref[2:3, :]
    h = (x * (1.0 + sc1) + sh1).astype(BF16)

    u, v = _cm_branch(h, _bdot(h, win_ref[:, C_U:C_Z]), lnvg_ref[...], lnvb_ref[...])
    vb = v.astype(BF16)
    row = lax.broadcasted_iota(jnp.int32, (CHUNK, CHUNK), 0)
    col = lax.broadcasted_iota(jnp.int32, (CHUNK, CHUNK), 1)
    wmix = [jnp.where(row >= col, wsp_ref[g], 0.0).astype(BF16) for g in range(CM_GROUPS)]
    rows = []
    for c in range(nc):
        cols = []
        for g in range(CM_GROUPS):
            cols.append(_bdot(wmix[g], vb[c * CHUNK:(c + 1) * CHUNK, g * LANES:(g + 1) * LANES]))
        rows.append(jnp.concatenate(cols, axis=1) + bsp_ref[...])
    mix = jnp.concatenate(rows, axis=0) if nc > 1 else rows[0]
    out_cm = (u * mix).astype(BF16)
    gate_cm = _sigmoid(_bdot(h, win_ref[:, C_GCM:C_GSSD]) + bg_ref[:, :D_MODEL])
    merged = gate_cm * _bdot(out_cm, wcm_ref[...])

    cbuf_ref[8:8 + tl, :] = _bdot(h, win_ref[:, C_XBC:C_GCM])
    acc = convb_ref[...]
    for k in range(CONV_K):
        acc = acc + cbuf_ref[5 + k:5 + k + tl, :] * convw_ref[k:k + 1, :]
    tail = cbuf_ref[tl + 5:tl + 8, :]
    cbuf_ref[5:8, :] = tail

    @pl.when(j == last)
    def _():
        conv_ref[...] = tail

    xbc_c = _silu(acc)
    xs = xbc_c[:, :D_INNER]
    dtseg = _bdot(h, win_ref[:, C_DT:IN_COLS_P])
    for c in range(nc):
        r0 = c * CHUNK
        _ssd_chunk(xs[r0:r0 + CHUNK], xbc_c[r0:r0 + CHUNK, D_INNER:D_INNER + SSD_GROUPS * STATE],
                   xbc_c[r0:r0 + CHUNK, D_INNER + SSD_GROUPS * STATE:], dtseg[r0:r0 + CHUNK],
                   dtb_ref[...], alog_ref[...], tri_ref[...], e64_ref[...], e128_ref[...],
                   state_ref, y_ref, r0)

    @pl.when(j == last)
    def _():
        ssd_ref[...] = state_ref[...].T

    z = _bdot(h, win_ref[:, C_Z:C_XBC])
    yz = (y_ref[...] + xs * dskip_ref[...]) * _silu(z)
    yn = _rms_norm(yz, normw_ref[...]).astype(BF16)
    gate_ssd = _sigmoid(_bdot(h, win_ref[:, C_GSSD:C_DT]) + bg_ref[:, D_MODEL:])
    merged = merged + gate_ssd * _bdot(yn, wssd_ref[...])
    x1_ref[...] = _post_mixer(x, g1, merged.astype(BF16), wo_ref, ln1g_ref[...], ln1b_ref[...])


def _prompt_mixer(x, mod_p, lw, consts):
    tl = TILE_L
    nj = SEQ // tl
    vec = lambda n: _const_spec((1, n))
    in_specs = [
        pl.BlockSpec((None, tl, D_MODEL), lambda b, j: (b, j, 0)),
        pl.BlockSpec((None, 6, D_MODEL), lambda b, j: (b, 0, 0)),
        _const_spec((D_MODEL, IN_COLS_P)),
        vec(2 * D_MODEL), vec(D_CM), vec(D_CM),
        _const_spec((CM_GROUPS, CHUNK, CHUNK)), _const_spec((CHUNK, D_CM)),
        _const_spec((CONV_K, CONV_DIM)), vec(CONV_DIM), vec(LANES), vec(LANES),
        vec(D_INNER), vec(D_INNER),
        _const_spec((D_CM, D_MODEL)), _const_spec((D_INNER, D_MODEL)), _const_spec((D_MODEL, D_MODEL)),
        vec(D_MODEL), vec(D_MODEL),
        _const_spec((CHUNK, CHUNK)), _const_spec((LANES, D_INNER)), _const_spec((LANES, HEADS * LANES)),
    ]
    out_specs = [
        pl.BlockSpec((None, tl, D_MODEL), lambda b, j: (b, j, 0)),
        pl.BlockSpec((None, D_INNER, STATE), lambda b, j: (b, 0, 0)),
        pl.BlockSpec((None, CONV_K - 1, CONV_DIM), lambda b, j: (b, 0, 0)),
    ]
    out_shape = [
        jax.ShapeDtypeStruct((BATCH, SEQ, D_MODEL), F32),
        jax.ShapeDtypeStruct((BATCH, D_INNER, STATE), F32),
        jax.ShapeDtypeStruct((BATCH, CONV_K - 1, CONV_DIM), F32),
    ]
    return pl.pallas_call(
        functools.partial(_p1_kernel, tl=tl),
        grid=(BATCH, nj),
        in_specs=in_specs, out_specs=out_specs, out_shape=out_shape,
        scratch_shapes=[pltpu.VMEM((STATE, D_INNER), F32),
                        pltpu.VMEM((tl + 8, CONV_DIM), F32),
                        pltpu.VMEM((tl, D_INNER), F32)],
        compiler_params=pltpu.CompilerParams(
            dimension_semantics=("parallel", "arbitrary"), vmem_limit_bytes=VMEM_LIMIT),
        name="prompt_mixer",
    )(x, mod_p, lw["w_in"], lw["b_gate"], lw["ln_v_g"], lw["ln_v_b"], lw["w_spatial"], lw["bsp_full"],
      lw["conv_w"], lw["conv_b"], lw["dt_bias3"], lw["a_log3"], lw["dskip_exp"], lw["ssd_norm_w"],
      lw["w_cm_br"], lw["w_ssd_br"], lw["w_o"], lw["ln1_g"], lw["ln1_b"],
      consts["tri"], consts["e64"], consts["e128"])


def _p2_kernel(x_ref, mod_ref, wg_ref, wu_ref, wd_ref, ln2g_ref, ln2b_ref, o_ref):
    o_ref[...] = _ffn(x_ref[...], mod_ref[3:4, :], mod_ref[4:5, :], mod_ref[5:6, :],
                      wg_ref, wu_ref, wd_ref, ln2g_ref[...], ln2b_ref[...])


def _prompt_ffn(x1, mod_p, lw):
    tf = TILE_F
    return pl.pallas_call(
        _p2_kernel,
        grid=(BATCH, SEQ // tf),
        in_specs=[
            pl.BlockSpec((None, tf, D_MODEL), lambda b, j: (b, j, 0)),
            pl.BlockSpec((None, 6, D_MODEL), lambda b, j: (b, 0, 0)),
            _const_spec((D_MODEL, D_FF)), _const_spec((D_MODEL, D_FF)), _const_spec((D_FF, D_MODEL)),
            _const_spec((1, D_MODEL)), _const_spec((1, D_MODEL)),
        ],
        out_specs=pl.BlockSpec((None, tf, D_MODEL), lambda b, j: (b, j, 0)),
        out_shape=jax.ShapeDtypeStruct((BATCH, SEQ, D_MODEL), F32),
        compiler_params=pltpu.CompilerParams(
            dimension_semantics=("parallel", "parallel"), vmem_limit_bytes=VMEM_LIMIT),
        name="prompt_ffn",
    )(x1, mod_p, lw["w_ffn_gate"], lw["w_ffn_up"], lw["w_ffn_down"], lw["ln2_g"], lw["ln2_b"])


def _s_pre_kernel(x_ref, mod_ref, win_ref, bg_ref, lnvg_ref, lnvb_ref, wsp0_ref, bsp0_ref,
                  convw_ref, convb_ref, cstate_ref, dtb_ref, alog_ref, dskip_ref, wcm_ref, e64_ref,
                  v_ref, conv_ref, da_ref, xdtt_ref, b_ref, c_ref, cmpart_ref, zs_ref, gs_ref, xsd_ref):
    x = x_ref[...]
    sh1 = mod_ref[:, 0:D_MODEL]
    sc1 = mod_ref[:, D_MODEL:2 * D_MODEL]
    h = (x * (1.0 + sc1) + sh1).astype(BF16)

    u, v = _cm_branch(h, _bdot(h, win_ref[:, C_U:C_Z]), lnvg_ref[...], lnvb_ref[...])
    v_ref[...] = v
    out_cm = (u * (wsp0_ref[...] * v + bsp0_ref[...])).astype(BF16)
    gate_cm = _sigmoid(_bdot(h, win_ref[:, C_GCM:C_GSSD]) + bg_ref[:, :D_MODEL])
    cmpart_ref[...] = gate_cm * _bdot(out_cm, wcm_ref[...])

    xbc = _bdot(h, win_ref[:, C_XBC:C_GCM])
    acc = convb_ref[...]
    for k in range(CONV_K - 1):
        acc = acc + cstate_ref[:, k * CONV_DIM:(k + 1) * CONV_DIM] * convw_ref[k:k + 1, :]
    acc = acc + xbc * convw_ref[CONV_K - 1:CONV_K, :]
    conv_ref[:, 0:(CONV_K - 2) * CONV_DIM] = cstate_ref[:, CONV_DIM:(CONV_K - 1) * CONV_DIM]
    conv_ref[:, (CONV_K - 2) * CONV_DIM:] = xbc
    xbc_c = _silu(acc)
    xs = xbc_c[:, :D_INNER]
    b_ref[...] = xbc_c[:, D_INNER:D_INNER + SSD_GROUPS * STATE]
    c_ref[...] = xbc_c[:, D_INNER + SSD_GROUPS * STATE:]
    dt = _softplus(_bdot(h, win_ref[:, C_DT:IN_COLS_P]) + dtb_ref[...])
    da_ref[...] = jnp.exp(dt * (-jnp.exp(alog_ref[...])))
    xdt = xs * _bdot(_pack3(dt), e64_ref[...])
    xdtt_ref[...] = xdt.T.astype(BF16)
    xsd_ref[...] = xs * dskip_ref[...]
    zs_ref[...] = _silu(_bdot(h, win_ref[:, C_Z:C_XBC]))
    gs_ref[...] = _sigmoid(_bdot(h, win_ref[:, C_GSSD:C_DT]) + bg_ref[:, D_MODEL:])


def _sample_pre(x, mod_s, cstate, lw, consts):
    n = DEC_BATCH
    shapes_in = [(n, D_MODEL), (n, 6 * D_MODEL), (D_MODEL, IN_COLS_P), (1, 2 * D_MODEL), (1, D_CM), (1, D_CM),
                 (1, D_CM), (1, D_CM), (CONV_K, CONV_DIM), (1, CONV_DIM), (n, (CONV_K - 1) * CONV_DIM),
                 (1, LANES), (1, LANES), (1, D_INNER), (D_CM, D_MODEL), (LANES, D_INNER)]
    outs = [((n, D_CM), F32), ((n, (CONV_K - 1) * CONV_DIM), F32), ((n, LANES), F32),
            ((D_INNER, n), BF16), ((n, SSD_GROUPS * STATE), F32), ((n, SSD_GROUPS * STATE), F32),
            ((n, D_MODEL), F32), ((n, D_INNER), F32), ((n, D_MODEL), F32), ((n, D_INNER), F32)]
    return pl.pallas_call(
        _s_pre_kernel,
        grid=(1,),
        in_specs=[_const_spec(s) for s in shapes_in],
        out_specs=[_const_spec(s, single=False) for s, _ in outs],
        out_shape=[jax.ShapeDtypeStruct(s, d) for s, d in outs],
        compiler_params=pltpu.CompilerParams(
            dimension_semantics=("arbitrary",), vmem_limit_bytes=VMEM_LIMIT),
        name="sample_pre",
    )(x, mod_s, lw["w_in"], lw["b_gate"], lw["ln_v_g"], lw["ln_v_b"], lw["wsp0"], lw["bsp0"],
      lw["conv_w"], lw["conv_b"], cstate, lw["dt_bias3"], lw["a_log3"], lw["dskip_exp"], lw["w_cm_br"],
      consts["e64"])


def _s_ssd_kernel(da_ref, xdtt_ref, b_ref, c_ref, st_ref, sto_ref, y_ref):
    b = pl.program_id(0)

    @pl.when(b == 0)
    def _():
        y_ref[...] = jnp.zeros_like(y_ref)

    b8 = pl.multiple_of((b // 8) * 8, 8)
    rows = lax.broadcasted_iota(jnp.int32, (DEC_BATCH, STATE), 0)
    rows8 = lax.broadcasted_iota(jnp.int32, (8, STATE), 0)
    gw = HEADS_PER_GROUP * HEAD_DIM
    for g in range(SSD_GROUPS):
        bg = jnp.where(rows == b, b_ref[:, g * STATE:(g + 1) * STATE], 0.0).astype(BF16)
        xb = _bdot(xdtt_ref[g * gw:(g + 1) * gw, :], bg)
        blocks = []
        for hh in range(HEADS_PER_GROUP):
            hd = g * HEADS_PER_GROUP + hh
            r = hd * HEAD_DIM
            blk = st_ref[r:r + HEAD_DIM, :] * da_ref[b, hd] + xb[hh * HEAD_DIM:(hh + 1) * HEAD_DIM, :]
            sto_ref[r:r + HEAD_DIM, :] = blk
            blocks.append(blk.astype(BF16))
        newg = jnp.concatenate(blocks, axis=0)
        cg = jnp.where(rows8 == b - b8, c_ref[pl.ds(b8, 8), g * STATE:(g + 1) * STATE], 0.0).astype(BF16)
        yg = lax.dot_general(cg, newg, (((1,), (1,)), ((), ())), preferred_element_type=F32)
        y_ref[pl.ds(b8, 8), g * gw:(g + 1) * gw] += yg


def _sample_ssd(da, xdtt, bm, cm, state):
    n = DEC_BATCH
    grid_spec = pltpu.PrefetchScalarGridSpec(
        num_scalar_prefetch=1,
        grid=(n,),
        in_specs=[
            pl.BlockSpec((D_INNER, n), lambda b, da: (0, 0)),
            pl.BlockSpec((n, SSD_GROUPS * STATE), lambda b, da: (0, 0)),
            pl.BlockSpec((n, SSD_GROUPS * STATE), lambda b, da: (0, 0)),
            pl.BlockSpec((None, D_INNER, STATE), lambda b, da: (b, 0, 0)),
        ],
        out_specs=[
            pl.BlockSpec((None, D_INNER, STATE), lambda b, da: (b, 0, 0)),
            pl.BlockSpec((n, D_INNER), lambda b, da: (0, 0)),
        ],
    )
    return pl.pallas_call(
        _s_ssd_kernel,
        grid_spec=grid_spec,
        out_shape=[jax.ShapeDtypeStruct((n, D_INNER, STATE), F32),
                   jax.ShapeDtypeStruct((n, D_INNER), F32)],
        compiler_params=pltpu.CompilerParams(
            dimension_semantics=("arbitrary",), vmem_limit_bytes=VMEM_LIMIT),
        name="sample_ssd",
    )(da, xdtt, bm, cm, state)


def _s_post_kernel(x_ref, mod_ref, y_ref, xsd_ref, zs_ref, gs_ref, cmpart_ref, normw_ref,
                   wssd_ref, wo_ref, ln1g_ref, ln1b_ref, wg_ref, wu_ref, wd_ref, ln2g_ref, ln2b_ref, o_ref):
    md = lambda k: mod_ref[:, k * D_MODEL:(k + 1) * D_MODEL]
    yn = _rms_norm((y_ref[...] + xsd_ref[...]) * zs_ref[...], normw_ref[...]).astype(BF16)
    merged = cmpart_ref[...] + gs_ref[...] * _bdot(yn, wssd_ref[...])
    x1 = _post_mixer(x_ref[...], md(2), merged.astype(BF16), wo_ref, ln1g_ref[...], ln1b_ref[...])
    o_ref[...] = _ffn(x1, md(3), md(4), md(5), wg_ref, wu_ref, wd_ref, ln2g_ref[...], ln2b_ref[...])


def _sample_post(x, mod_s, y, xsd, zs, gs, cmpart, lw):
    n = DEC_BATCH
    shapes_in = [(n, D_MODEL), (n, 6 * D_MODEL), (n, D_INNER), (n, D_INNER), (n, D_INNER), (n, D_MODEL),
                 (n, D_MODEL), (1, D_INNER), (D_INNER, D_MODEL), (D_MODEL, D_MODEL), (1, D_MODEL), (1, D_MODEL),
                 (D_MODEL, D_FF), (D_MODEL, D_FF), (D_FF, D_MODEL), (1, D_MODEL), (1, D_MODEL)]
    return pl.pallas_call(
        _s_post_kernel,
        grid=(1,),
        in_specs=[_const_spec(s) for s in shapes_in],
        out_specs=_const_spec((n, D_MODEL), single=False),
        out_shape=jax.ShapeDtypeStruct((n, D_MODEL), F32),
        compiler_params=pltpu.CompilerParams(
            dimension_semantics=("arbitrary",), vmem_limit_bytes=VMEM_LIMIT),
        name="sample_post",
    )(x, mod_s, y, xsd, zs, gs, cmpart, lw["ssd_norm_w"], lw["w_ssd_br"], lw["w_o"], lw["ln1_g"], lw["ln1_b"],
      lw["w_ffn_gate"], lw["w_ffn_up"], lw["w_ffn_down"], lw["ln2_g"], lw["ln2_b"])


def _constants():
    tri = np.tril(np.ones((CHUNK, CHUNK), np.float32))
    e64 = np.zeros((LANES, D_INNER), np.float32)
    e128 = np.zeros((LANES, HEADS * LANES), np.float32)
    for r in range(DT_COPIES * HEADS):
        hd = r % HEADS
        e64[r, hd * HEAD_DIM:(hd + 1) * HEAD_DIM] = 1.0
        e128[r, hd * LANES:(hd + 1) * LANES] = 1.0
    return {"tri": jnp.asarray(tri, BF16), "e64": jnp.asarray(e64, BF16), "e128": jnp.asarray(e128, BF16)}


def _layer_weights(i, p):
    w_in = p["w_in"][i]
    cuts = np.cumsum([D_CM, D_CM, D_INNER, CONV_DIM, HEADS, D_MODEL]).tolist()
    u_v_z_xbc = w_in[:, :cuts[3]]
    w_dt = w_in[:, cuts[3]:cuts[4]]
    gates = w_in[:, cuts[4]:]
    pad = jnp.zeros((D_MODEL, LANES - DT_COPIES * HEADS), w_in.dtype)
    w_in_r = jnp.concatenate([u_v_z_xbc, gates] + [w_dt] * DT_COPIES + [pad], axis=1).astype(BF16)
    zpad = jnp.zeros((LANES - DT_COPIES * HEADS,), F32)
    row = lambda a: a.reshape(1, -1)
    return {
        "w_in": w_in_r,
        "b_gate": row(p["b_gate"][i]),
        "ln_v_g": row(p["ln_v_g"][i]), "ln_v_b": row(p["ln_v_b"][i]),
        "w_spatial": p["w_spatial"][i],
        "bsp_full": jnp.repeat(p["b_spatial"][i].T, LANES, axis=1),
        "wsp0": row(jnp.repeat(p["w_spatial"][i, :, 0, 0], LANES)),
        "bsp0": row(jnp.repeat(p["b_spatial"][i, :, 0], LANES)),
        "conv_w": p["conv_w"][i], "conv_b": row(p["conv_b"][i]),
        "dt_bias3": row(jnp.concatenate([p["dt_bias"][i]] * DT_COPIES + [zpad])),
        "a_log3": row(jnp.concatenate([p["a_log"][i]] * DT_COPIES + [zpad])),
        "dskip_exp": row(jnp.repeat(p["d_skip"][i], HEAD_DIM)),
        "ssd_norm_w": row(p["ssd_norm_w"][i]),
        "w_cm_br": p["w_cm_br"][i].astype(BF16), "w_ssd_br": p["w_ssd_br"][i].astype(BF16),
        "w_o": p["w_o"][i].astype(BF16),
        "ln1_g": row(p["ln1_g"][i]), "ln1_b": row(p["ln1_b"][i]),
        "w_ffn_gate": p["w_ffn_gate"][i].astype(BF16), "w_ffn_up": p["w_ffn_up"][i].astype(BF16),
        "w_ffn_down": p["w_ffn_down"][i].astype(BF16),
        "ln2_g": row(p["ln2_g"][i]), "ln2_b": row(p["ln2_b"][i]),
    }


def kernel(x_prompt, x_sample, state_ssd, state_conv, c_prompt, c_sample, w_ada, b_ada, w_in, b_gate,
           ln_v_g, ln_v_b, w_spatial, b_spatial, conv_w, conv_b, dt_bias, a_log, d_skip, ssd_norm_w,
           w_cm_br, w_ssd_br, w_o, ln1_g, ln1_b, w_ffn_gate, w_ffn_up, w_ffn_down, ln2_g, ln2_b):
    p = dict(w_in=w_in, b_gate=b_gate, ln_v_g=ln_v_g, ln_v_b=ln_v_b, w_spatial=w_spatial,
             b_spatial=b_spatial, conv_w=conv_w, conv_b=conv_b, dt_bias=dt_bias, a_log=a_log,
             d_skip=d_skip, ssd_norm_w=ssd_norm_w, w_cm_br=w_cm_br, w_ssd_br=w_ssd_br, w_o=w_o,
             ln1_g=ln1_g, ln1_b=ln1_b, w_ffn_gate=w_ffn_gate, w_ffn_up=w_ffn_up,
             w_ffn_down=w_ffn_down, ln2_g=ln2_g, ln2_b=ln2_b)
    consts = _constants()
    mod = _adaln_mod(jnp.concatenate([c_prompt, c_sample], axis=0), w_ada, b_ada)
    xp = x_prompt
    xs = x_sample.reshape(DEC_BATCH, D_MODEL)
    ssd_p, conv_p, ssd_s, conv_s, v_s = [], [], [], [], []
    for i in range(DEPTH):
        lw = _layer_weights(i, p)
        mod_p = mod[i, :BATCH].reshape(BATCH, 6, D_MODEL)
        mod_s = mod[i, BATCH:]
        x1, sp, cp = _prompt_mixer(xp, mod_p, lw, consts)
        xp = _prompt_ffn(x1, mod_p, lw)
        ssd_p.append(sp.reshape(BATCH, HEADS, HEAD_DIM, STATE))
        conv_p.append(cp)
        cstate = state_conv[i].reshape(DEC_BATCH, (CONV_K - 1) * CONV_DIM)
        v, cs, da, xdtt, bm, cm, cmpart, zs, gs, xsd = _sample_pre(xs, mod_s, cstate, lw, consts)
        st, y = _sample_ssd(da[:, :HEADS], xdtt, bm, cm,
                            state_ssd[i].reshape(DEC_BATCH, D_INNER, STATE))
        xs = _sample_post(xs, mod_s, y, xsd, zs, gs, cmpart, lw)
        ssd_s.append(st.reshape(DEC_BATCH, HEADS, HEAD_DIM, STATE))
        conv_s.append(cs.reshape(DEC_BATCH, CONV_K - 1, CONV_DIM))
        v_s.append(v.reshape(DEC_BATCH, 1, D_CM))
    return (xp, xs.reshape(DEC_BATCH, 1, D_MODEL), jnp.stack(ssd_p), jnp.stack(conv_p),
            jnp.stack(ssd_s), jnp.stack(conv_s), jnp.stack(v_s))
```

```python
import functools
import math

import numpy as np
import jax
import jax.numpy as jnp
from jax import lax
from jax.experimental import pallas as pl
from jax.experimental.pallas import tpu as pltpu

D_MODEL = 1024
BATCH = 8
SEQ = 2048
DEPTH = 2
DEC_BATCH = 128
CHUNK = 128
D_CM = D_MODEL
CM_GROUPS = 8
D_INNER = 2048
HEAD_DIM = 64
HEADS = 32
SSD_GROUPS = 4
HEADS_PER_GROUP = HEADS // SSD_GROUPS
STATE = 128
CONV_K = 4
CONV_DIM = D_INNER + 2 * SSD_GROUPS * STATE
D_FF = 2816
ALPHA = (2 * DEPTH) ** 0.25
LN_EPS = 1e-5

LANES = 128
DT_COPIES = 3
C_U = 0
C_V = C_U + D_CM
C_Z = C_V + D_CM
C_XBC = C_Z + D_INNER
C_GCM = C_XBC + CONV_DIM
C_GSSD = C_GCM + D_MODEL
C_DT = C_GSSD + D_MODEL
IN_COLS_P = C_DT + LANES

TILE_L = 256
TILE_F = 512
SSD_BB = 4
VMEM_LIMIT = 56 * 1024 * 1024

F32 = jnp.float32
BF16 = jnp.bfloat16


def _bdot(a, b):
    return jnp.dot(a, b, preferred_element_type=F32)


def _gelu(x):
    return 0.5 * x * (1.0 + lax.erf(x * np.float32(math.sqrt(0.5))))


def _sigmoid(x):
    return 1.0 / (1.0 + jnp.exp(-x))


def _silu(x):
    return x * _sigmoid(x)


def _softplus(x):
    return jnp.maximum(x, 0.0) + jnp.log1p(jnp.exp(-jnp.abs(x)))


def _layer_norm(x, g, b):
    mu = jnp.mean(x, axis=-1, keepdims=True)
    xc = x - mu
    var = jnp.mean(xc * xc, axis=-1, keepdims=True)
    return xc * lax.rsqrt(var + LN_EPS) * g + b


def _rms_norm(x, g):
    return x * lax.rsqrt(jnp.mean(x * x, axis=-1, keepdims=True) + LN_EPS) * g


def _pack3(x):
    hi = x.astype(BF16).astype(F32)
    r1 = x - hi
    mid = r1.astype(BF16).astype(F32)
    r2 = r1 - mid
    lane = lax.broadcasted_iota(jnp.int32, x.shape, 1)
    return jnp.where(lane < HEADS, x, jnp.where(lane < 2 * HEADS, r1, r2)).astype(BF16)


def _const_spec(shape, single=True):
    nd = len(shape)
    kw = {"pipeline_mode": pl.Buffered(1)} if single else {}
    return pl.BlockSpec(shape, lambda *_: (0,) * nd, **kw)


def _mod_kernel(c_ref, w_ref, b_ref, o_ref):
    s = _silu(c_ref[...])
    o_ref[...] = jnp.dot(s, w_ref[...], preferred_element_type=F32,
                         precision=lax.Precision.HIGHEST) + b_ref[...]


def _adaln_mod(c_all, w_ada, b_ada):
    rows = c_all.shape[0]
    nblk = 6
    return pl.pallas_call(
        _mod_kernel,
        grid=(DEPTH, nblk),
        in_specs=[
            pl.BlockSpec((rows, D_MODEL), lambda l, k: (0, 0)),
            pl.BlockSpec((None, D_MODEL, D_MODEL), lambda l, k: (l, 0, k)),
            pl.BlockSpec((None, 1, D_MODEL), lambda l, k: (l, 0, k)),
        ],
        out_specs=pl.BlockSpec((None, rows, D_MODEL), lambda l, k: (l, 0, k)),
        out_shape=jax.ShapeDtypeStruct((DEPTH, rows, nblk * D_MODEL), F32),
        compiler_params=pltpu.CompilerParams(
            dimension_semantics=("parallel", "parallel"), vmem_limit_bytes=VMEM_LIMIT),
        name="adaln_mod",
    )(c_all, w_ada, b_ada.reshape(DEPTH, 1, nblk * D_MODEL))


def _cm_branch(h, u_v, lnvg, lnvb):
    u = _gelu(u_v[:, :D_CM])
    v = _layer_norm(_gelu(u_v[:, D_CM:]), lnvg, lnvb)
    return u, v


def _post_mixer(x, g1, merged_bf16, wo_ref, ln1g, ln1b):
    o = _bdot(merged_bf16, wo_ref[...])
    return _layer_norm(ALPHA * x + g1 * o, ln1g, ln1b)


def _ffn(x1, sh2, sc2, g2, wg_ref, wu_ref, wd_ref, ln2g, ln2b):
    h2 = (x1 * (1.0 + sc2) + sh2).astype(BF16)
    gate = _bdot(h2, wg_ref[...])
    up = _bdot(h2, wu_ref[...])
    act = (_silu(gate) * up).astype(BF16)
    ffn = _bdot(act, wd_ref[...])
    return _layer_norm(ALPHA * x1 + g2 * ffn, ln2g, ln2b)


def _ssd_chunk(xs_c, b_c, c_c, dtseg, dtb, alog, tri, e64, e128, state_ref, y_ref, r0):
    row = lax.broadcasted_iota(jnp.int32, (CHUNK, LANES), 0)
    lane = lax.broadcasted_iota(jnp.int32, (CHUNK, LANES), 1)
    tril = row >= lane
    low_half = lane < HEAD_DIM

    dt = _softplus(dtseg + dtb)
    a = -jnp.exp(alog)
    da = dt * a
    hi = da.astype(BF16)
    r1 = da - hi.astype(F32)
    mid = r1.astype(BF16)
    lo = (r1 - mid.astype(F32)).astype(BF16)
    acum = _bdot(tri, hi) + _bdot(tri, mid) + _bdot(tri, lo)
    acum_t = acum.T
    dt_t = dt.T
    alast = acum[CHUNK - 1:CHUNK, :]
    w = dt * jnp.exp(alast - acum)
    wexp = _bdot(_pack3(w), e64)
    cd = jnp.broadcast_to(jnp.exp(alast), (8, LANES))
    cdexp = _bdot(_pack3(cd), e64)[0:1, :]
    abc = _bdot(_pack3(acum), e128)

    for g in range(SSD_GROUPS):
        bg = b_c[:, g * STATE:(g + 1) * STATE]
        cg = c_c[:, g * STATE:(g + 1) * STATE]
        bgb = bg.astype(BF16)
        cb = lax.dot_general(cg.astype(BF16), bgb, (((1,), (1,)), ((), ())),
                             preferred_element_type=F32)
        for pr in range(HEADS_PER_GROUP // 2):
            h0 = g * HEADS_PER_GROUP + 2 * pr
            pieces = []
            for h in (h0, h0 + 1):
                acol = abc[:, h * LANES:(h + 1) * LANES]
                seg = acol - acum_t[h:h + 1, :]
                lh = jnp.exp(jnp.where(tril, seg, -jnp.inf))
                mh = cb * lh * dt_t[h:h + 1, :]
                ch = cg * jnp.exp(acol)
                pieces += [mh.astype(BF16), ch.astype(BF16)]
            lhs = jnp.concatenate(pieces, axis=1)
            pc = h0 * HEAD_DIM
            xs_pair = xs_c[:, pc:pc + LANES]
            st_pair = state_ref[:, pc:pc + LANES]
            rhs = jnp.concatenate(
                [jnp.where(low_half, xs_pair, 0.0), jnp.where(low_half, st_pair, 0.0),
                 jnp.where(low_half, 0.0, xs_pair), jnp.where(low_half, 0.0, st_pair)],
                axis=0).astype(BF16)
            y_ref[r0:r0 + CHUNK, pc:pc + LANES] = _bdot(lhs, rhs)
        gc = g * HEADS_PER_GROUP * HEAD_DIM
        gw = HEADS_PER_GROUP * HEAD_DIM
        xw = (xs_c[:, gc:gc + gw] * wexp[:, gc:gc + gw]).astype(BF16)
        upd = _bdot(bg.T.astype(BF16), xw)
        state_ref[:, gc:gc + gw] = state_ref[:, gc:gc + gw] * cdexp[:, gc:gc + gw] + upd


def _p1_kernel(x_ref, mod_ref, win_ref, bg_ref, lnvg_ref, lnvb_ref, wsp_ref, bsp_ref,
               convw_ref, convb_ref, dtb_ref, alog_ref, dskip_ref, normw_ref,
               wcm_ref, wssd_ref, wo_ref, ln1g_ref, ln1b_ref, tri_ref, e64_ref, e128_ref,
               x1_ref, ssd_ref, conv_ref,
               state_ref, cbuf_ref, y_ref, *, tl):
    j = pl.program_id(1)
    last = pl.num_programs(1) - 1
    nc = tl // CHUNK

    @pl.when(j == 0)
    def _():
        state_ref[...] = jnp.zeros_like(state_ref)
        cbuf_ref[0:8, :] = jnp.zeros((8, CONV_DIM), F32)

    x = x_ref[...]
    sh1 = mod_ref[0:1, :]
    sc1 = mod_ref[1:2, :]
    g1 = mod_ref[2:3, :]
    h = (x * (1.0 + sc1) + sh1).astype(BF16)

    u, v = _cm_branch(h, _bdot(h, win_ref[:, C_U:C_Z]), lnvg_ref[...], lnvb_ref[...])
    vb = v.astype(BF16)
    row = lax.broadcasted_iota(jnp.int32, (CHUNK, CHUNK), 0)
    col = lax.broadcasted_iota(jnp.int32, (CHUNK, CHUNK), 1)
    wmix = [jnp.where(row >= col, wsp_ref[g], 0.0).astype(BF16) for g in range(CM_GROUPS)]
    rows = []
    for c in range(nc):
        cols = []
        for g in range(CM_GROUPS):
            cols.append(_bdot(wmix[g], vb[c * CHUNK:(c + 1) * CHUNK, g * LANES:(g + 1) * LANES]))
        rows.append(jnp.concatenate(cols, axis=1) + bsp_ref[...])
    mix = jnp.concatenate(rows, axis=0) if nc > 1 else rows[0]
    out_cm = (u * mix).astype(BF16)
    gate_cm = _sigmoid(_bdot(h, win_ref[:, C_GCM:C_GSSD]) + bg_ref[:, :D_MODEL])
    merged = gate_cm * _bdot(out_cm, wcm_ref[...])

    cbuf_ref[8:8 + tl, :] = _bdot(h, win_ref[:, C_XBC:C_GCM])
    acc = convb_ref[...]
    for k in range(CONV_K):
        acc = acc + cbuf_ref[5 + k:5 + k + tl, :] * convw_ref[k:k + 1, :]
    tail = cbuf_ref[tl + 5:tl + 8, :]
    cbuf_ref[5:8, :] = tail

    @pl.when(j == last)
    def _():
        conv_ref[...] = tail

    xbc_c = _silu(acc)
    xs = xbc_c[:, :D_INNER]
    dtseg = _bdot(h, win_ref[:, C_DT:IN_COLS_P])
    for c in range(nc):
        r0 = c * CHUNK
        _ssd_chunk(xs[r0:r0 + CHUNK], xbc_c[r0:r0 + CHUNK, D_INNER:D_INNER + SSD_GROUPS * STATE],
                   xbc_c[r0:r0 + CHUNK, D_INNER + SSD_GROUPS * STATE:], dtseg[r0:r0 + CHUNK],
                   dtb_ref[...], alog_ref[...], tri_ref[...], e64_ref[...], e128_ref[...],
                   state_ref, y_ref, r0)

    @pl.when(j == last)
    def _():
        ssd_ref[...] = state_ref[...].T

    z = _bdot(h, win_ref[:, C_Z:C_XBC])
    yz = (y_ref[...] + xs * dskip_ref[...]) * _silu(z)
    yn = _rms_norm(yz, normw_ref[...]).astype(BF16)
    gate_ssd = _sigmoid(_bdot(h, win_ref[:, C_GSSD:C_DT]) + bg_ref[:, D_MODEL:])
    merged = merged + gate_ssd * _bdot(yn, wssd_ref[...])
    x1_ref[...] = _post_mixer(x, g1, merged.astype(BF16), wo_ref, ln1g_ref[...], ln1b_ref[...])


def _prompt_mixer(x, mod_p, lw, consts):
    tl = TILE_L
    nj = SEQ // tl
    vec = lambda n: _const_spec((1, n))
    in_specs = [
        pl.BlockSpec((None, tl, D_MODEL), lambda b, j: (b, j, 0)),
        pl.BlockSpec((None, 6, D_MODEL), lambda b, j: (b, 0, 0)),
        _const_spec((D_MODEL, IN_COLS_P)),
        vec(2 * D_MODEL), vec(D_CM), vec(D_CM),
        _const_spec((CM_GROUPS, CHUNK, CHUNK)), _const_spec((CHUNK, D_CM)),
        _const_spec((CONV_K, CONV_DIM)), vec(CONV_DIM), vec(LANES), vec(LANES),
        vec(D_INNER), vec(D_INNER),
        _const_spec((D_CM, D_MODEL)), _const_spec((D_INNER, D_MODEL)), _const_spec((D_MODEL, D_MODEL)),
        vec(D_MODEL), vec(D_MODEL),
        _const_spec((CHUNK, CHUNK)), _const_spec((LANES, D_INNER)), _const_spec((LANES, HEADS * LANES)),
    ]
    out_specs = [
        pl.BlockSpec((None, tl, D_MODEL), lambda b, j: (b, j, 0)),
        pl.BlockSpec((None, D_INNER, STATE), lambda b, j: (b, 0, 0)),
        pl.BlockSpec((None, CONV_K - 1, CONV_DIM), lambda b, j: (b, 0, 0)),
    ]
    out_shape = [
        jax.ShapeDtypeStruct((BATCH, SEQ, D_MODEL), F32),
        jax.ShapeDtypeStruct((BATCH, D_INNER, STATE), F32),
        jax.ShapeDtypeStruct((BATCH, CONV_K - 1, CONV_DIM), F32),
    ]
    return pl.pallas_call(
        functools.partial(_p1_kernel, tl=tl),
        grid=(BATCH, nj),
        in_specs=in_specs, out_specs=out_specs, out_shape=out_shape,
        scratch_shapes=[pltpu.VMEM((STATE, D_INNER), F32),
                        pltpu.VMEM((tl + 8, CONV_DIM), F32),
                        pltpu.VMEM((tl, D_INNER), F32)],
        compiler_params=pltpu.CompilerParams(
            dimension_semantics=("parallel", "arbitrary"), vmem_limit_bytes=VMEM_LIMIT),
        name="prompt_mixer",
    )(x, mod_p, lw["w_in"], lw["b_gate"], lw["ln_v_g"], lw["ln_v_b"], lw["w_spatial"], lw["bsp_full"],
      lw["conv_w"], lw["conv_b"], lw["dt_bias3"], lw["a_log3"], lw["dskip_exp"], lw["ssd_norm_w"],
      lw["w_cm_br"], lw["w_ssd_br"], lw["w_o"], lw["ln1_g"], lw["ln1_b"],
      consts["tri"], consts["e64"], consts["e128"])


def _p2_kernel(x_ref, mod_ref, wg_ref, wu_ref, wd_ref, ln2g_ref, ln2b_ref, o_ref):
    o_ref[...] = _ffn(x_ref[...], mod_ref[3:4, :], mod_ref[4:5, :], mod_ref[5:6, :],
                      wg_ref, wu_ref, wd_ref, ln2g_ref[...], ln2b_ref[...])


def _prompt_ffn(x1, mod_p, lw):
    tf = TILE_F
    return pl.pallas_call(
        _p2_kernel,
        grid=(BATCH, SEQ // tf),
        in_specs=[
            pl.BlockSpec((None, tf, D_MODEL), lambda b, j: (b, j, 0)),
            pl.BlockSpec((None, 6, D_MODEL), lambda b, j: (b, 0, 0)),
            _const_spec((D_MODEL, D_FF)), _const_spec((D_MODEL, D_FF)), _const_spec((D_FF, D_MODEL)),
            _const_spec((1, D_MODEL)), _const_spec((1, D_MODEL)),
        ],
        out_specs=pl.BlockSpec((None, tf, D_MODEL), lambda b, j: (b, j, 0)),
        out_shape=jax.ShapeDtypeStruct((BATCH, SEQ, D_MODEL), F32),
        compiler_params=pltpu.CompilerParams(
            dimension_semantics=("parallel", "parallel"), vmem_limit_bytes=VMEM_LIMIT),
        name="prompt_ffn",
    )(x1, mod_p, lw["w_ffn_gate"], lw["w_ffn_up"], lw["w_ffn_down"], lw["ln2_g"], lw["ln2_b"])


def _s_pre_kernel(x_ref, mod_ref, win_ref, bg_ref, lnvg_ref, lnvb_ref, wsp0_ref, bsp0_ref,
                  convw_ref, convb_ref, cstate_ref, dtb_ref, alog_ref, dskip_ref, wcm_ref, e64_ref,
                  v_ref, conv_ref, da_ref, xdtt_ref, b_ref, c_ref, cmpart_ref, zs_ref, gs_ref, xsd_ref):
    x = x_ref[...]
    sh1 = mod_ref[:, 0:D_MODEL]
    sc1 = mod_ref[:, D_MODEL:2 * D_MODEL]
    h = (x * (1.0 + sc1) + sh1).astype(BF16)

    u, v = _cm_branch(h, _bdot(h, win_ref[:, C_U:C_Z]), lnvg_ref[...], lnvb_ref[...])
    v_ref[...] = v
    out_cm = (u * (wsp0_ref[...] * v + bsp0_ref[...])).astype(BF16)
    gate_cm = _sigmoid(_bdot(h, win_ref[:, C_GCM:C_GSSD]) + bg_ref[:, :D_MODEL])
    cmpart_ref[...] = gate_cm * _bdot(out_cm, wcm_ref[...])

    xbc = _bdot(h, win_ref[:, C_XBC:C_GCM])
    acc = convb_ref[...]
    for k in range(CONV_K - 1):
        acc = acc + cstate_ref[:, k * CONV_DIM:(k + 1) * CONV_DIM] * convw_ref[k:k + 1, :]
    acc = acc + xbc * convw_ref[CONV_K - 1:CONV_K, :]
    conv_ref[:, 0:(CONV_K - 2) * CONV_DIM] = cstate_ref[:, CONV_DIM:(CONV_K - 1) * CONV_DIM]
    conv_ref[:, (CONV_K - 2) * CONV_DIM:] = xbc
    xbc_c = _silu(acc)
    xs = xbc_c[:, :D_INNER]
    b_ref[...] = xbc_c[:, D_INNER:D_INNER + SSD_GROUPS * STATE]
    c_ref[...] = xbc_c[:, D_INNER + SSD_GROUPS * STATE:]
    dt = _softplus(_bdot(h, win_ref[:, C_DT:IN_COLS_P]) + dtb_ref[...])
    da_ref[...] = jnp.exp(dt * (-jnp.exp(alog_ref[...])))
    xdt = xs * _bdot(_pack3(dt), e64_ref[...])
    xdtt_ref[...] = xdt.T.astype(BF16)
    xsd_ref[...] = xs * dskip_ref[...]
    zs_ref[...] = _silu(_bdot(h, win_ref[:, C_Z:C_XBC]))
    gs_ref[...] = _sigmoid(_bdot(h, win_ref[:, C_GSSD:C_DT]) + bg_ref[:, D_MODEL:])


def _sample_pre(x, mod_s, cstate, lw, consts):
    n = DEC_BATCH
    shapes_in = [(n, D_MODEL), (n, 6 * D_MODEL), (D_MODEL, IN_COLS_P), (1, 2 * D_MODEL), (1, D_CM), (1, D_CM),
                 (1, D_CM), (1, D_CM), (CONV_K, CONV_DIM), (1, CONV_DIM), (n, (CONV_K - 1) * CONV_DIM),
                 (1, LANES), (1, LANES), (1, D_INNER), (D_CM, D_MODEL), (LANES, D_INNER)]
    outs = [((n, D_CM), F32), ((n, (CONV_K - 1) * CONV_DIM), F32), ((n, LANES), F32),
            ((D_INNER, n), BF16), ((n, SSD_GROUPS * STATE), F32), ((n, SSD_GROUPS * STATE), F32),
            ((n, D_MODEL), F32), ((n, D_INNER), F32), ((n, D_MODEL), F32), ((n, D_INNER), F32)]
    return pl.pallas_call(
        _s_pre_kernel,
        grid=(1,),
        in_specs=[_const_spec(s) for s in shapes_in],
        out_specs=[_const_spec(s, single=False) for s, _ in outs],
        out_shape=[jax.ShapeDtypeStruct(s, d) for s, d in outs],
        compiler_params=pltpu.CompilerParams(
            dimension_semantics=("arbitrary",), vmem_limit_bytes=VMEM_LIMIT),
        name="sample_pre",
    )(x, mod_s, lw["w_in"], lw["b_gate"], lw["ln_v_g"], lw["ln_v_b"], lw["wsp0"], lw["bsp0"],
      lw["conv_w"], lw["conv_b"], cstate, lw["dt_bias3"], lw["a_log3"], lw["dskip_exp"], lw["w_cm_br"],
      consts["e64"])


def _s_ssd_kernel(da_ref, xdtt_ref, b_ref, c_ref, st_ref, *rest):
    sto_ref, y_ref = rest[-2:]
    i = pl.program_id(0)

    @pl.when(i == 0)
    def _():
        y_ref[...] = jnp.zeros_like(y_ref)

    rows = lax.broadcasted_iota(jnp.int32, (DEC_BATCH, STATE), 0)
    rows8 = lax.broadcasted_iota(jnp.int32, (8, STATE), 0)
    gw = HEADS_PER_GROUP * HEAD_DIM
    for q in range(SSD_BB):
        b = i * SSD_BB + q
        b8 = pl.multiple_of((b // 8) * 8, 8)
        for g in range(SSD_GROUPS):
            bg = jnp.where(rows == b, b_ref[:, g * STATE:(g + 1) * STATE], 0.0).astype(BF16)
            xb = _bdot(xdtt_ref[g * gw:(g + 1) * gw, :], bg)
            blocks = []
            for hh in range(HEADS_PER_GROUP):
                hd = g * HEADS_PER_GROUP + hh
                r = hd * HEAD_DIM
                blk = st_ref[q, r:r + HEAD_DIM, :] * da_ref[b, hd] + xb[hh * HEAD_DIM:(hh + 1) * HEAD_DIM, :]
                sto_ref[q, r:r + HEAD_DIM, :] = blk
                blocks.append(blk.astype(BF16))
            newg = jnp.concatenate(blocks, axis=0)
            cg = jnp.where(rows8 == b - b8, c_ref[pl.ds(b8, 8), g * STATE:(g + 1) * STATE], 0.0).astype(BF16)
            yg = lax.dot_general(cg, newg, (((1,), (1,)), ((), ())), preferred_element_type=F32)
            y_ref[pl.ds(b8, 8), g * gw:(g + 1) * gw] += yg


def _sample_ssd(layer, da, xdtt, bm, cm, state, stacked):
    n = DEC_BATCH
    in_specs = [
        pl.BlockSpec((D_INNER, n), lambda i, da: (0, 0)),
        pl.BlockSpec((n, SSD_GROUPS * STATE), lambda i, da: (0, 0)),
        pl.BlockSpec((n, SSD_GROUPS * STATE), lambda i, da: (0, 0)),
        pl.BlockSpec((None, SSD_BB, D_INNER, STATE), lambda i, da: (layer, i, 0, 0)),
    ]
    args = [da, xdtt, bm, cm, state]
    aliases = {}
    if stacked is not None:
        in_specs.append(pl.BlockSpec(memory_space=pl.ANY))
        args.append(stacked)
        aliases = {len(args) - 1: 0}
    grid_spec = pltpu.PrefetchScalarGridSpec(
        num_scalar_prefetch=1,
        grid=(n // SSD_BB,),
        in_specs=in_specs,
        out_specs=[
            pl.BlockSpec((None, SSD_BB, D_INNER, STATE), lambda i, da: (layer, i, 0, 0)),
            pl.BlockSpec((n, D_INNER), lambda i, da: (0, 0)),
        ],
    )
    return pl.pallas_call(
        _s_ssd_kernel,
        grid_spec=grid_spec,
        out_shape=[jax.ShapeDtypeStruct((DEPTH, n, D_INNER, STATE), F32),
                   jax.ShapeDtypeStruct((n, D_INNER), F32)],
        input_output_aliases=aliases,
        compiler_params=pltpu.CompilerParams(
            dimension_semantics=("arbitrary",), vmem_limit_bytes=VMEM_LIMIT),
        name="sample_ssd",
    )(*args)


def _s_post_kernel(x_ref, mod_ref, y_ref, xsd_ref, zs_ref, gs_ref, cmpart_ref, normw_ref,
                   wssd_ref, wo_ref, ln1g_ref, ln1b_ref, wg_ref, wu_ref, wd_ref, ln2g_ref, ln2b_ref, o_ref):
    md = lambda k: mod_ref[:, k * D_MODEL:(k + 1) * D_MODEL]
    yn = _rms_norm((y_ref[...] + xsd_ref[...]) * zs_ref[...], normw_ref[...]).astype(BF16)
    merged = cmpart_ref[...] + gs_ref[...] * _bdot(yn, wssd_ref[...])
    x1 = _post_mixer(x_ref[...], md(2), merged.astype(BF16), wo_ref, ln1g_ref[...], ln1b_ref[...])
    o_ref[...] = _ffn(x1, md(3), md(4), md(5), wg_ref, wu_ref, wd_ref, ln2g_ref[...], ln2b_ref[...])


def _sample_post(x, mod_s, y, xsd, zs, gs, cmpart, lw):
    n = DEC_BATCH
    shapes_in = [(n, D_MODEL), (n, 6 * D_MODEL), (n, D_INNER), (n, D_INNER), (n, D_INNER), (n, D_MODEL),
                 (n, D_MODEL), (1, D_INNER), (D_INNER, D_MODEL), (D_MODEL, D_MODEL), (1, D_MODEL), (1, D_MODEL),
                 (D_MODEL, D_FF), (D_MODEL, D_FF), (D_FF, D_MODEL), (1, D_MODEL), (1, D_MODEL)]
    return pl.pallas_call(
        _s_post_kernel,
        grid=(1,),
        in_specs=[_const_spec(s) for s in shapes_in],
        out_specs=_const_spec((n, D_MODEL), single=False),
        out_shape=jax.ShapeDtypeStruct((n, D_MODEL), F32),
        compiler_params=pltpu.CompilerParams(
            dimension_semantics=("arbitrary",), vmem_limit_bytes=VMEM_LIMIT),
        name="sample_post",
    )(x, mod_s, y, xsd, zs, gs, cmpart, lw["ssd_norm_w"], lw["w_ssd_br"], lw["w_o"], lw["ln1_g"], lw["ln1_b"],
      lw["w_ffn_gate"], lw["w_ffn_up"], lw["w_ffn_down"], lw["ln2_g"], lw["ln2_b"])


def _constants():
    tri = np.tril(np.ones((CHUNK, CHUNK), np.float32))
    e64 = np.zeros((LANES, D_INNER), np.float32)
    e128 = np.zeros((LANES, HEADS * LANES), np.float32)
    for r in range(DT_COPIES * HEADS):
        hd = r % HEADS
        e64[r, hd * HEAD_DIM:(hd + 1) * HEAD_DIM] = 1.0
        e128[r, hd * LANES:(hd + 1) * LANES] = 1.0
    return {"tri": jnp.asarray(tri, BF16), "e64": jnp.asarray(e64, BF16), "e128": jnp.asarray(e128, BF16)}


def _layer_weights(i, p):
    w_in = p["w_in"][i]
    cuts = np.cumsum([D_CM, D_CM, D_INNER, CONV_DIM, HEADS, D_MODEL]).tolist()
    u_v_z_xbc = w_in[:, :cuts[3]]
    w_dt = w_in[:, cuts[3]:cuts[4]]
    gates = w_in[:, cuts[4]:]
    pad = jnp.zeros((D_MODEL, LANES - DT_COPIES * HEADS), w_in.dtype)
    w_in_r = jnp.concatenate([u_v_z_xbc, gates] + [w_dt] * DT_COPIES + [pad], axis=1).astype(BF16)
    zpad = jnp.zeros((LANES - DT_COPIES * HEADS,), F32)
    row = lambda a: a.reshape(1, -1)
    return {
        "w_in": w_in_r,
        "b_gate": row(p["b_gate"][i]),
        "ln_v_g": row(p["ln_v_g"][i]), "ln_v_b": row(p["ln_v_b"][i]),
        "w_spatial": p["w_spatial"][i],
        "bsp_full": jnp.repeat(p["b_spatial"][i].T, LANES, axis=1),
        "wsp0": row(jnp.repeat(p["w_spatial"][i, :, 0, 0], LANES)),
        "bsp0": row(jnp.repeat(p["b_spatial"][i, :, 0], LANES)),
        "conv_w": p["conv_w"][i], "conv_b": row(p["conv_b"][i]),
        "dt_bias3": row(jnp.concatenate([p["dt_bias"][i]] * DT_COPIES + [zpad])),
        "a_log3": row(jnp.concatenate([p["a_log"][i]] * DT_COPIES + [zpad])),
        "dskip_exp": row(jnp.repeat(p["d_skip"][i], HEAD_DIM)),
        "ssd_norm_w": row(p["ssd_norm_w"][i]),
        "w_cm_br": p["w_cm_br"][i].astype(BF16), "w_ssd_br": p["w_ssd_br"][i].astype(BF16),
        "w_o": p["w_o"][i].astype(BF16),
        "ln1_g": row(p["ln1_g"][i]), "ln1_b": row(p["ln1_b"][i]),
        "w_ffn_gate": p["w_ffn_gate"][i].astype(BF16), "w_ffn_up": p["w_ffn_up"][i].astype(BF16),
        "w_ffn_down": p["w_ffn_down"][i].astype(BF16),
        "ln2_g": row(p["ln2_g"][i]), "ln2_b": row(p["ln2_b"][i]),
    }


def kernel(x_prompt, x_sample, state_ssd, state_conv, c_prompt, c_sample, w_ada, b_ada, w_in, b_gate,
           ln_v_g, ln_v_b, w_spatial, b_spatial, conv_w, conv_b, dt_bias, a_log, d_skip, ssd_norm_w,
           w_cm_br, w_ssd_br, w_o, ln1_g, ln1_b, w_ffn_gate, w_ffn_up, w_ffn_down, ln2_g, ln2_b):
    p = dict(w_in=w_in, b_gate=b_gate, ln_v_g=ln_v_g, ln_v_b=ln_v_b, w_spatial=w_spatial,
             b_spatial=b_spatial, conv_w=conv_w, conv_b=conv_b, dt_bias=dt_bias, a_log=a_log,
             d_skip=d_skip, ssd_norm_w=ssd_norm_w, w_cm_br=w_cm_br, w_ssd_br=w_ssd_br, w_o=w_o,
             ln1_g=ln1_g, ln1_b=ln1_b, w_ffn_gate=w_ffn_gate, w_ffn_up=w_ffn_up,
             w_ffn_down=w_ffn_down, ln2_g=ln2_g, ln2_b=ln2_b)
    consts = _constants()
    mod = _adaln_mod(jnp.concatenate([c_prompt, c_sample], axis=0), w_ada, b_ada)
    xp = x_prompt
    xs = x_sample.reshape(DEC_BATCH, D_MODEL)
    ssd_p, conv_p, conv_s, v_s = [], [], [], []
    state_in = state_ssd.reshape(DEPTH, DEC_BATCH, D_INNER, STATE)
    ssd_s = None
    for i in range(DEPTH):
        lw = _layer_weights(i, p)
        mod_p = mod[i, :BATCH].reshape(BATCH, 6, D_MODEL)
        mod_s = mod[i, BATCH:]
        x1, sp, cp = _prompt_mixer(xp, mod_p, lw, consts)
        xp = _prompt_ffn(x1, mod_p, lw)
        ssd_p.append(sp.reshape(BATCH, HEADS, HEAD_DIM, STATE))
        conv_p.append(cp)
        cstate = state_conv[i].reshape(DEC_BATCH, (CONV_K - 1) * CONV_DIM)
        v, cs, da, xdtt, bm, cm, cmpart, zs, gs, xsd = _sample_pre(xs, mod_s, cstate, lw, consts)
        ssd_s, y = _sample_ssd(i, da[:, :HEADS], xdtt, bm, cm, state_in, ssd_s)
        xs = _sample_post(xs, mod_s, y, xsd, zs, gs, cmpart, lw)
        conv_s.append(cs.reshape(DEC_BATCH, CONV_K - 1, CONV_DIM))
        v_s.append(v.reshape(DEC_BATCH, 1, D_CM))
    return (xp, xs.reshape(DEC_BATCH, 1, D_MODEL), jnp.stack(ssd_p), jnp.stack(conv_p),
            ssd_s.reshape(DEPTH, DEC_BATCH, HEADS, HEAD_DIM, STATE), jnp.stack(conv_s), jnp.stack(v_s))
```

```python
import functools
import math

import numpy as np
import jax
import jax.numpy as jnp
from jax import lax
from jax.experimental import pallas as pl
from jax.experimental.pallas import tpu as pltpu

D_MODEL = 1024
BATCH = 8
SEQ = 2048
DEPTH = 2
DEC_BATCH = 128
CHUNK = 128
D_CM = D_MODEL
CM_GROUPS = 8
D_INNER = 2048
HEAD_DIM = 64
HEADS = 32
SSD_GROUPS = 4
HEADS_PER_GROUP = HEADS // SSD_GROUPS
STATE = 128
CONV_K = 4
CONV_DIM = D_INNER + 2 * SSD_GROUPS * STATE
D_FF = 2816
ALPHA = (2 * DEPTH) ** 0.25
LN_EPS = 1e-5

LANES = 128
DT_COPIES = 3
C_U = 0
C_V = C_U + D_CM
C_Z = C_V + D_CM
C_XBC = C_Z + D_INNER
C_GCM = C_XBC + CONV_DIM
C_GSSD = C_GCM + D_MODEL
C_DT = C_GSSD + D_MODEL
IN_COLS_P = C_DT + LANES

TILE_L = 256
TILE_F = 512
SSD_BB = 4
VMEM_LIMIT = 56 * 1024 * 1024

F32 = jnp.float32
BF16 = jnp.bfloat16


def _bdot(a, b):
    return jnp.dot(a, b, preferred_element_type=F32)


def _gelu(x):
    return 0.5 * x * (1.0 + lax.erf(x * np.float32(math.sqrt(0.5))))


def _sigmoid(x):
    return 1.0 / (1.0 + jnp.exp(-x))


def _silu(x):
    return x * _sigmoid(x)


def _softplus(x):
    return jnp.maximum(x, 0.0) + jnp.log1p(jnp.exp(-jnp.abs(x)))


def _layer_norm(x, g, b):
    mu = jnp.mean(x, axis=-1, keepdims=True)
    xc = x - mu
    var = jnp.mean(xc * xc, axis=-1, keepdims=True)
    return xc * lax.rsqrt(var + LN_EPS) * g + b


def _rms_norm(x, g):
    return x * lax.rsqrt(jnp.mean(x * x, axis=-1, keepdims=True) + LN_EPS) * g


def _pack3(x):
    hi = x.astype(BF16).astype(F32)
    r1 = x - hi
    mid = r1.astype(BF16).astype(F32)
    r2 = r1 - mid
    lane = lax.broadcasted_iota(jnp.int32, x.shape, 1)
    return jnp.where(lane < HEADS, x, jnp.where(lane < 2 * HEADS, r1, r2)).astype(BF16)


def _const_spec(shape, single=True):
    nd = len(shape)
    kw = {"pipeline_mode": pl.Buffered(1)} if single else {}
    return pl.BlockSpec(shape, lambda *_: (0,) * nd, **kw)


def _mod_kernel(c_ref, w_ref, b_ref, o_ref):
    s = _silu(c_ref[...])
    o_ref[...] = jnp.dot(s, w_ref[...], preferred_element_type=F32,
                         precision=lax.Precision.HIGHEST) + b_ref[...]


def _adaln_mod(c_all, w_ada, b_ada):
    rows = c_all.shape[0]
    nblk = 6
    return pl.pallas_call(
        _mod_kernel,
        grid=(DEPTH, nblk),
        in_specs=[
            pl.BlockSpec((rows, D_MODEL), lambda l, k: (0, 0)),
            pl.BlockSpec((None, D_MODEL, D_MODEL), lambda l, k: (l, 0, k)),
            pl.BlockSpec((None, 1, D_MODEL), lambda l, k: (l, 0, k)),
        ],
        out_specs=pl.BlockSpec((None, rows, D_MODEL), lambda l, k: (l, 0, k)),
        out_shape=jax.ShapeDtypeStruct((DEPTH, rows, nblk * D_MODEL), F32),
        compiler_params=pltpu.CompilerParams(
            dimension_semantics=("parallel", "parallel"), vmem_limit_bytes=VMEM_LIMIT),
        name="adaln_mod",
    )(c_all, w_ada, b_ada.reshape(DEPTH, 1, nblk * D_MODEL))


def _cm_branch(h, u_v, lnvg, lnvb):
    u = _gelu(u_v[:, :D_CM])
    v = _layer_norm(_gelu(u_v[:, D_CM:]), lnvg, lnvb)
    return u, v


def _post_mixer(x, g1, merged_bf16, wo_ref, ln1g, ln1b):
    o = _bdot(merged_bf16, wo_ref[...])
    return _layer_norm(ALPHA * x + g1 * o, ln1g, ln1b)


def _ffn(x1, sh2, sc2, g2, wg_ref, wu_ref, wd_ref, ln2g, ln2b):
    h2 = (x1 * (1.0 + sc2) + sh2).astype(BF16)
    gate = _bdot(h2, wg_ref[...])
    up = _bdot(h2, wu_ref[...])
    act = (_silu(gate) * up).astype(BF16)
    ffn = _bdot(act, wd_ref[...])
    return _layer_norm(ALPHA * x1 + g2 * ffn, ln2g, ln2b)


def _ssd_chunk(xs_c, b_c, c_c, dtseg, dtb, alog, tri, e64, state_ref, y_ref, r0):
    row = lax.broadcasted_iota(jnp.int32, (CHUNK, LANES), 0)
    lane = lax.broadcasted_iota(jnp.int32, (CHUNK, LANES), 1)
    tril = row >= lane
    low_half = lane < HEAD_DIM

    dt = _softplus(dtseg + dtb)
    a = -jnp.exp(alog)
    da = dt * a
    hi = da.astype(BF16)
    r1 = da - hi.astype(F32)
    mid = r1.astype(BF16)
    lo = (r1 - mid.astype(F32)).astype(BF16)
    acum = _bdot(tri, hi) + _bdot(tri, mid) + _bdot(tri, lo)
    shift_t = acum.T - jnp.log(dt.T)
    alast = acum[CHUNK - 1:CHUNK, :]
    w = dt * jnp.exp(alast - acum)
    wexp = _bdot(_pack3(w), e64)
    eexp = _bdot(_pack3(jnp.exp(acum)), e64)
    cd = jnp.broadcast_to(jnp.exp(alast), (8, LANES))
    cdexp = _bdot(_pack3(cd), e64)[0:1, :]
    xs_b = xs_c.astype(BF16)
    zero_b = jnp.zeros((CHUNK, LANES), BF16)

    gw = HEADS_PER_GROUP * HEAD_DIM
    for g in range(SSD_GROUPS):
        gc = g * gw
        bg = b_c[:, g * STATE:(g + 1) * STATE]
        cgb = c_c[:, g * STATE:(g + 1) * STATE].astype(BF16)
        cb = lax.dot_general(cgb, bg.astype(BF16), (((1,), (1,)), ((), ())),
                             preferred_element_type=F32)
        y_off = eexp[:, gc:gc + gw] * _bdot(cgb, state_ref[:, gc:gc + gw].astype(BF16))
        for pr in range(HEADS_PER_GROUP // 2):
            h0 = g * HEADS_PER_GROUP + 2 * pr
            pieces = []
            for h in (h0, h0 + 1):
                seg = jnp.broadcast_to(acum[:, h:h + 1], (CHUNK, LANES)) - shift_t[h:h + 1, :]
                pieces.append((cb * jnp.exp(jnp.where(tril, seg, -jnp.inf))).astype(BF16))
            lhs = jnp.concatenate(pieces, axis=1)
            pc = h0 * HEAD_DIM
            xs_pair = xs_b[:, pc:pc + LANES]
            rhs = jnp.concatenate([jnp.where(low_half, xs_pair, zero_b),
                                   jnp.where(low_half, zero_b, xs_pair)], axis=0)
            y_ref[r0:r0 + CHUNK, pc:pc + LANES] = (
                _bdot(lhs, rhs) + y_off[:, pc - gc:pc - gc + LANES])
        xw = (xs_c[:, gc:gc + gw] * wexp[:, gc:gc + gw]).astype(BF16)
        upd = _bdot(bg.T.astype(BF16), xw)
        state_ref[:, gc:gc + gw] = state_ref[:, gc:gc + gw] * cdexp[:, gc:gc + gw] + upd


def _p1_kernel(x_ref, mod_ref, win_ref, bg_ref, lnvg_ref, lnvb_ref, wsp_ref, bsp_ref,
               convw_ref, convb_ref, dtb_ref, alog_ref, dskip_ref, normw_ref,
               wcm_ref, wssd_ref, wo_ref, ln1g_ref, ln1b_ref, tri_ref, e64_ref,
               x1_ref, ssd_ref, conv_ref,
               state_ref, cbuf_ref, y_ref, *, tl):
    j = pl.program_id(1)
    last = pl.num_programs(1) - 1
    nc = tl // CHUNK

    @pl.when(j == 0)
    def _():
        state_ref[...] = jnp.zeros_like(state_ref)
        cbuf_ref[0:8, :] = jnp.zeros((8, CONV_DIM), F32)

    x = x_ref[...]
    sh1 = mod_ref[0:1, :]
    sc1 = mod_ref[1:2, :]
    g1 = mod_ref[2:3, :]
    h = (x * (1.0 + sc1) + sh1).astype(BF16)

    cbuf_ref[8:8 + tl, :] = _bdot(h, win_ref[:, C_XBC:C_GCM])
    dtseg = _bdot(h, win_ref[:, C_DT:IN_COLS_P])
    u_v = _bdot(h, win_ref[:, C_U:C_Z])

    acc = convb_ref[...]
    for k in range(CONV_K):
        acc = acc + cbuf_ref[5 + k:5 + k + tl, :] * convw_ref[k:k + 1, :]
    tail = cbuf_ref[tl + 5:tl + 8, :]
    cbuf_ref[5:8, :] = tail
    xbc_c = _silu(acc)
    xs = xbc_c[:, :D_INNER]

    z = _bdot(h, win_ref[:, C_Z:C_XBC])
    gcm_logit = _bdot(h, win_ref[:, C_GCM:C_GSSD])
    gssd_logit = _bdot(h, win_ref[:, C_GSSD:C_DT])

    u, v = _cm_branch(h, u_v, lnvg_ref[...], lnvb_ref[...])
    vb = v.astype(BF16)
    row = lax.broadcasted_iota(jnp.int32, (CHUNK, CHUNK), 0)
    col = lax.broadcasted_iota(jnp.int32, (CHUNK, CHUNK), 1)
    wmix = [jnp.where(row >= col, wsp_ref[g], 0.0).astype(BF16) for g in range(CM_GROUPS)]

    for c in range(nc):
        r0 = c * CHUNK
        _ssd_chunk(xs[r0:r0 + CHUNK], xbc_c[r0:r0 + CHUNK, D_INNER:D_INNER + SSD_GROUPS * STATE],
                   xbc_c[r0:r0 + CHUNK, D_INNER + SSD_GROUPS * STATE:], dtseg[r0:r0 + CHUNK],
                   dtb_ref[...], alog_ref[...], tri_ref[...], e64_ref[...],
                   state_ref, y_ref, r0)

    rows = []
    for c in range(nc):
        cols = []
        for g in range(CM_GROUPS):
            cols.append(_bdot(wmix[g], vb[c * CHUNK:(c + 1) * CHUNK, g * LANES:(g + 1) * LANES]))
        rows.append(jnp.concatenate(cols, axis=1) + bsp_ref[...])
    mix = jnp.concatenate(rows, axis=0) if nc > 1 else rows[0]
    out_cm = (u * mix).astype(BF16)
    merged = _sigmoid(gcm_logit + bg_ref[:, :D_MODEL]) * _bdot(out_cm, wcm_ref[...])

    yz = (y_ref[...] + xs * dskip_ref[...]) * _silu(z)
    yn = _rms_norm(yz, normw_ref[...]).astype(BF16)
    merged = merged + _sigmoid(gssd_logit + bg_ref[:, D_MODEL:]) * _bdot(yn, wssd_ref[...])
    x1_ref[...] = _post_mixer(x, g1, merged.astype(BF16), wo_ref, ln1g_ref[...], ln1b_ref[...])

    @pl.when(j == last)
    def _():
        conv_ref[...] = cbuf_ref[5:8, :]
        ssd_ref[...] = state_ref[...].T


def _prompt_mixer(x, mod_p, lw, consts):
    tl = TILE_L
    nj = SEQ // tl
    vec = lambda n: _const_spec((1, n))
    in_specs = [
        pl.BlockSpec((None, tl, D_MODEL), lambda b, j: (b, j, 0)),
        pl.BlockSpec((None, 6, D_MODEL), lambda b, j: (b, 0, 0)),
        _const_spec((D_MODEL, IN_COLS_P)),
        vec(2 * D_MODEL), vec(D_CM), vec(D_CM),
        _const_spec((CM_GROUPS, CHUNK, CHUNK)), _const_spec((CHUNK, D_CM)),
        _const_spec((CONV_K, CONV_DIM)), vec(CONV_DIM), vec(LANES), vec(LANES),
        vec(D_INNER), vec(D_INNER),
        _const_spec((D_CM, D_MODEL)), _const_spec((D_INNER, D_MODEL)), _const_spec((D_MODEL, D_MODEL)),
        vec(D_MODEL), vec(D_MODEL),
        _const_spec((CHUNK, CHUNK)), _const_spec((LANES, D_INNER)),
    ]
    out_specs = [
        pl.BlockSpec((None, tl, D_MODEL), lambda b, j: (b, j, 0)),
        pl.BlockSpec((None, D_INNER, STATE), lambda b, j: (b, 0, 0)),
        pl.BlockSpec((None, CONV_K - 1, CONV_DIM), lambda b, j: (b, 0, 0)),
    ]
    out_shape = [
        jax.ShapeDtypeStruct((BATCH, SEQ, D_MODEL), F32),
        jax.ShapeDtypeStruct((BATCH, D_INNER, STATE), F32),
        jax.ShapeDtypeStruct((BATCH, CONV_K - 1, CONV_DIM), F32),
    ]
    return pl.pallas_call(
        functools.partial(_p1_kernel, tl=tl),
        grid=(BATCH, nj),
        in_specs=in_specs, out_specs=out_specs, out_shape=out_shape,
        scratch_shapes=[pltpu.VMEM((STATE, D_INNER), F32),
                        pltpu.VMEM((tl + 8, CONV_DIM), F32),
                        pltpu.VMEM((tl, D_INNER), F32)],
        compiler_params=pltpu.CompilerParams(
            dimension_semantics=("parallel", "arbitrary"), vmem_limit_bytes=VMEM_LIMIT),
        name="prompt_mixer",
    )(x, mod_p, lw["w_in"], lw["b_gate"], lw["ln_v_g"], lw["ln_v_b"], lw["w_spatial"], lw["bsp_full"],
      lw["conv_w"], lw["conv_b"], lw["dt_bias3"], lw["a_log3"], lw["dskip_exp"], lw["ssd_norm_w"],
      lw["w_cm_br"], lw["w_ssd_br"], lw["w_o"], lw["ln1_g"], lw["ln1_b"],
      consts["tri"], consts["e64"])


def _p2_kernel(x_ref, mod_ref, wg_ref, wu_ref, wd_ref, ln2g_ref, ln2b_ref, o_ref):
    o_ref[...] = _ffn(x_ref[...], mod_ref[3:4, :], mod_ref[4:5, :], mod_ref[5:6, :],
                      wg_ref, wu_ref, wd_ref, ln2g_ref[...], ln2b_ref[...])


def _prompt_ffn(x1, mod_p, lw):
    tf = TILE_F
    return pl.pallas_call(
        _p2_kernel,
        grid=(BATCH, SEQ // tf),
        in_specs=[
            pl.BlockSpec((None, tf, D_MODEL), lambda b, j: (b, j, 0)),
            pl.BlockSpec((None, 6, D_MODEL), lambda b, j: (b, 0, 0)),
            _const_spec((D_MODEL, D_FF)), _const_spec((D_MODEL, D_FF)), _const_spec((D_FF, D_MODEL)),
            _const_spec((1, D_MODEL)), _const_spec((1, D_MODEL)),
        ],
        out_specs=pl.BlockSpec((None, tf, D_MODEL), lambda b, j: (b, j, 0)),
        out_shape=jax.ShapeDtypeStruct((BATCH, SEQ, D_MODEL), F32),
        compiler_params=pltpu.CompilerParams(
            dimension_semantics=("parallel", "parallel"), vmem_limit_bytes=VMEM_LIMIT),
        name="prompt_ffn",
    )(x1, mod_p, lw["w_ffn_gate"], lw["w_ffn_up"], lw["w_ffn_down"], lw["ln2_g"], lw["ln2_b"])


def _s_pre_kernel(x_ref, mod_ref, win_ref, bg_ref, lnvg_ref, lnvb_ref, wsp0_ref, bsp0_ref,
                  convw_ref, convb_ref, cstate_ref, dtb_ref, alog_ref, dskip_ref, wcm_ref, e64_ref,
                  v_ref, conv_ref, da_ref, xdtt_ref, b_ref, c_ref, cmpart_ref, zs_ref, gs_ref, xsd_ref):
    x = x_ref[...]
    sh1 = mod_ref[:, 0:D_MODEL]
    sc1 = mod_ref[:, D_MODEL:2 * D_MODEL]
    h = (x * (1.0 + sc1) + sh1).astype(BF16)

    u, v = _cm_branch(h, _bdot(h, win_ref[:, C_U:C_Z]), lnvg_ref[...], lnvb_ref[...])
    v_ref[...] = v
    out_cm = (u * (wsp0_ref[...] * v + bsp0_ref[...])).astype(BF16)
    gate_cm = _sigmoid(_bdot(h, win_ref[:, C_GCM:C_GSSD]) + bg_ref[:, :D_MODEL])
    cmpart_ref[...] = gate_cm * _bdot(out_cm, wcm_ref[...])

    xbc = _bdot(h, win_ref[:, C_XBC:C_GCM])
    acc = convb_ref[...]
    for k in range(CONV_K - 1):
        acc = acc + cstate_ref[:, k * CONV_DIM:(k + 1) * CONV_DIM] * convw_ref[k:k + 1, :]
    acc = acc + xbc * convw_ref[CONV_K - 1:CONV_K, :]
    conv_ref[:, 0:(CONV_K - 2) * CONV_DIM] = cstate_ref[:, CONV_DIM:(CONV_K - 1) * CONV_DIM]
    conv_ref[:, (CONV_K - 2) * CONV_DIM:] = xbc
    xbc_c = _silu(acc)
    xs = xbc_c[:, :D_INNER]
    b_ref[...] = xbc_c[:, D_INNER:D_INNER + SSD_GROUPS * STATE]
    c_ref[...] = xbc_c[:, D_INNER + SSD_GROUPS * STATE:]
    dt = _softplus(_bdot(h, win_ref[:, C_DT:IN_COLS_P]) + dtb_ref[...])
    da_ref[...] = jnp.exp(dt * (-jnp.exp(alog_ref[...])))
    xdt = xs * _bdot(_pack3(dt), e64_ref[...])
    xdtt_ref[...] = xdt.T.astype(BF16)
    xsd_ref[...] = xs * dskip_ref[...]
    zs_ref[...] = _silu(_bdot(h, win_ref[:, C_Z:C_XBC]))
    gs_ref[...] = _sigmoid(_bdot(h, win_ref[:, C_GSSD:C_DT]) + bg_ref[:, D_MODEL:])


def _sample_pre(x, mod_s, cstate, lw, consts):
    n = DEC_BATCH
    shapes_in = [(n, D_MODEL), (n, 6 * D_MODEL), (D_MODEL, IN_COLS_P), (1, 2 * D_MODEL), (1, D_CM), (1, D_CM),
                 (1, D_CM), (1, D_CM), (CONV_K, CONV_DIM), (1, CONV_DIM), (n, (CONV_K - 1) * CONV_DIM),
                 (1, LANES), (1, LANES), (1, D_INNER), (D_CM, D_MODEL), (LANES, D_INNER)]
    outs = [((n, D_CM), F32), ((n, (CONV_K - 1) * CONV_DIM), F32), ((n, LANES), F32),
            ((D_INNER, n), BF16), ((n, SSD_GROUPS * STATE), F32), ((n, SSD_GROUPS * STATE), F32),
            ((n, D_MODEL), F32), ((n, D_INNER), F32), ((n, D_MODEL), F32), ((n, D_INNER), F32)]
    return pl.pallas_call(
        _s_pre_kernel,
        grid=(1,),
        in_specs=[_const_spec(s) for s in shapes_in],
        out_specs=[_const_spec(s, single=False) for s, _ in outs],
        out_shape=[jax.ShapeDtypeStruct(s, d) for s, d in outs],
        compiler_params=pltpu.CompilerParams(
            dimension_semantics=("arbitrary",), vmem_limit_bytes=VMEM_LIMIT),
        name="sample_pre",
    )(x, mod_s, lw["w_in"], lw["b_gate"], lw["ln_v_g"], lw["ln_v_b"], lw["wsp0"], lw["bsp0"],
      lw["conv_w"], lw["conv_b"], cstate, lw["dt_bias3"], lw["a_log3"], lw["dskip_exp"], lw["w_cm_br"],
      consts["e64"])


def _s_ssd_kernel(da_ref, xdtt_ref, b_ref, c_ref, st_ref, *rest):
    sto_ref, y_ref = rest[-2:]
    i = pl.program_id(0)

    @pl.when(i == 0)
    def _():
        y_ref[...] = jnp.zeros_like(y_ref)

    rows = lax.broadcasted_iota(jnp.int32, (DEC_BATCH, STATE), 0)
    rows8 = lax.broadcasted_iota(jnp.int32, (8, STATE), 0)
    gw = HEADS_PER_GROUP * HEAD_DIM
    for q in range(SSD_BB):
        b = i * SSD_BB + q
        b8 = pl.multiple_of((b // 8) * 8, 8)
        for g in range(SSD_GROUPS):
            bg = jnp.where(rows == b, b_ref[:, g * STATE:(g + 1) * STATE], 0.0).astype(BF16)
            xb = _bdot(xdtt_ref[g * gw:(g + 1) * gw, :], bg)
            blocks = []
            for hh in range(HEADS_PER_GROUP):
                hd = g * HEADS_PER_GROUP + hh
                r = hd * HEAD_DIM
                blk = st_ref[q, r:r + HEAD_DIM, :] * da_ref[b, hd] + xb[hh * HEAD_DIM:(hh + 1) * HEAD_DIM, :]
                sto_ref[q, r:r + HEAD_DIM, :] = blk
                blocks.append(blk.astype(BF16))
            newg = jnp.concatenate(blocks, axis=0)
            cg = jnp.where(rows8 == b - b8, c_ref[pl.ds(b8, 8), g * STATE:(g + 1) * STATE], 0.0).astype(BF16)
            yg = lax.dot_general(cg, newg, (((1,), (1,)), ((), ())), preferred_element_type=F32)
            y_ref[pl.ds(b8, 8), g * gw:(g + 1) * gw] += yg


def _sample_ssd(layer, da, xdtt, bm, cm, state, stacked):
    n = DEC_BATCH
    in_specs = [
        pl.BlockSpec((D_INNER, n), lambda i, da: (0, 0)),
        pl.BlockSpec((n, SSD_GROUPS * STATE), lambda i, da: (0, 0)),
        pl.BlockSpec((n, SSD_GROUPS * STATE), lambda i, da: (0, 0)),
        pl.BlockSpec((None, SSD_BB, D_INNER, STATE), lambda i, da: (layer, i, 0, 0)),
    ]
    args = [da, xdtt, bm, cm, state]
    aliases = {}
    if stacked is not None:
        in_specs.append(pl.BlockSpec(memory_space=pl.ANY))
        args.append(stacked)
        aliases = {len(args) - 1: 0}
    grid_spec = pltpu.PrefetchScalarGridSpec(
        num_scalar_prefetch=1,
        grid=(n // SSD_BB,),
        in_specs=in_specs,
        out_specs=[
            pl.BlockSpec((None, SSD_BB, D_INNER, STATE), lambda i, da: (layer, i, 0, 0)),
            pl.BlockSpec((n, D_INNER), lambda i, da: (0, 0)),
        ],
    )
    return pl.pallas_call(
        _s_ssd_kernel,
        grid_spec=grid_spec,
        out_shape=[jax.ShapeDtypeStruct((DEPTH, n, D_INNER, STATE), F32),
                   jax.ShapeDtypeStruct((n, D_INNER), F32)],
        input_output_aliases=aliases,
        compiler_params=pltpu.CompilerParams(
            dimension_semantics=("arbitrary",), vmem_limit_bytes=VMEM_LIMIT),
        name="sample_ssd",
    )(*args)


def _s_post_kernel(x_ref, mod_ref, y_ref, xsd_ref, zs_ref, gs_ref, cmpart_ref, normw_ref,
                   wssd_ref, wo_ref, ln1g_ref, ln1b_ref, wg_ref, wu_ref, wd_ref, ln2g_ref, ln2b_ref, o_ref):
    md = lambda k: mod_ref[:, k * D_MODEL:(k + 1) * D_MODEL]
    yn = _rms_norm((y_ref[...] + xsd_ref[...]) * zs_ref[...], normw_ref[...]).astype(BF16)
    merged = cmpart_ref[...] + gs_ref[...] * _bdot(yn, wssd_ref[...])
    x1 = _post_mixer(x_ref[...], md(2), merged.astype(BF16), wo_ref, ln1g_ref[...], ln1b_ref[...])
    o_ref[...] = _ffn(x1, md(3), md(4), md(5), wg_ref, wu_ref, wd_ref, ln2g_ref[...], ln2b_ref[...])


def _sample_post(x, mod_s, y, xsd, zs, gs, cmpart, lw):
    n = DEC_BATCH
    shapes_in = [(n, D_MODEL), (n, 6 * D_MODEL), (n, D_INNER), (n, D_INNER), (n, D_INNER), (n, D_MODEL),
                 (n, D_MODEL), (1, D_INNER), (D_INNER, D_MODEL), (D_MODEL, D_MODEL), (1, D_MODEL), (1, D_MODEL),
                 (D_MODEL, D_FF), (D_MODEL, D_FF), (D_FF, D_MODEL), (1, D_MODEL), (1, D_MODEL)]
    return pl.pallas_call(
        _s_post_kernel,
        grid=(1,),
        in_specs=[_const_spec(s) for s in shapes_in],
        out_specs=_const_spec((n, D_MODEL), single=False),
        out_shape=jax.ShapeDtypeStruct((n, D_MODEL), F32),
        compiler_params=pltpu.CompilerParams(
            dimension_semantics=("arbitrary",), vmem_limit_bytes=VMEM_LIMIT),
        name="sample_post",
    )(x, mod_s, y, xsd, zs, gs, cmpart, lw["ssd_norm_w"], lw["w_ssd_br"], lw["w_o"], lw["ln1_g"], lw["ln1_b"],
      lw["w_ffn_gate"], lw["w_ffn_up"], lw["w_ffn_down"], lw["ln2_g"], lw["ln2_b"])


def _constants():
    tri = np.tril(np.ones((CHUNK, CHUNK), np.float32))
    e64 = np.zeros((LANES, D_INNER), np.float32)
    for r in range(DT_COPIES * HEADS):
        hd = r % HEADS
        e64[r, hd * HEAD_DIM:(hd + 1) * HEAD_DIM] = 1.0
    return {"tri": jnp.asarray(tri, BF16), "e64": jnp.asarray(e64, BF16)}


def _layer_weights(i, p):
    w_in = p["w_in"][i]
    cuts = np.cumsum([D_CM, D_CM, D_INNER, CONV_DIM, HEADS, D_MODEL]).tolist()
    u_v_z_xbc = w_in[:, :cuts[3]]
    w_dt = w_in[:, cuts[3]:cuts[4]]
    gates = w_in[:, cuts[4]:]
    pad = jnp.zeros((D_MODEL, LANES - DT_COPIES * HEADS), w_in.dtype)
    w_in_r = jnp.concatenate([u_v_z_xbc, gates] + [w_dt] * DT_COPIES + [pad], axis=1).astype(BF16)
    zpad = jnp.zeros((LANES - DT_COPIES * HEADS,), F32)
    row = lambda a: a.reshape(1, -1)
    return {
        "w_in": w_in_r,
        "b_gate": row(p["b_gate"][i]),
        "ln_v_g": row(p["ln_v_g"][i]), "ln_v_b": row(p["ln_v_b"][i]),
        "w_spatial": p["w_spatial"][i],
        "bsp_full": jnp.repeat(p["b_spatial"][i].T, LANES, axis=1),
        "wsp0": row(jnp.repeat(p["w_spatial"][i, :, 0, 0], LANES)),
        "bsp0": row(jnp.repeat(p["b_spatial"][i, :, 0], LANES)),
        "conv_w": p["conv_w"][i], "conv_b": row(p["conv_b"][i]),
        "dt_bias3": row(jnp.concatenate([p["dt_bias"][i]] * DT_COPIES + [zpad])),
        "a_log3": row(jnp.concatenate([p["a_log"][i]] * DT_COPIES + [zpad])),
        "dskip_exp": row(jnp.repeat(p["d_skip"][i], HEAD_DIM)),
        "ssd_norm_w": row(p["ssd_norm_w"][i]),
        "w_cm_br": p["w_cm_br"][i].astype(BF16), "w_ssd_br": p["w_ssd_br"][i].astype(BF16),
        "w_o": p["w_o"][i].astype(BF16),
        "ln1_g": row(p["ln1_g"][i]), "ln1_b": row(p["ln1_b"][i]),
        "w_ffn_gate": p["w_ffn_gate"][i].astype(BF16), "w_ffn_up": p["w_ffn_up"][i].astype(BF16),
        "w_ffn_down": p["w_ffn_down"][i].astype(BF16),
        "ln2_g": row(p["ln2_g"][i]), "ln2_b": row(p["ln2_b"][i]),
    }


def kernel(x_prompt, x_sample, state_ssd, state_conv, c_prompt, c_sample, w_ada, b_ada, w_in, b_gate,
           ln_v_g, ln_v_b, w_spatial, b_spatial, conv_w, conv_b, dt_bias, a_log, d_skip, ssd_norm_w,
           w_cm_br, w_ssd_br, w_o, ln1_g, ln1_b, w_ffn_gate, w_ffn_up, w_ffn_down, ln2_g, ln2_b):
    p = dict(w_in=w_in, b_gate=b_gate, ln_v_g=ln_v_g, ln_v_b=ln_v_b, w_spatial=w_spatial,
             b_spatial=b_spatial, conv_w=conv_w, conv_b=conv_b, dt_bias=dt_bias, a_log=a_log,
             d_skip=d_skip, ssd_norm_w=ssd_norm_w, w_cm_br=w_cm_br, w_ssd_br=w_ssd_br, w_o=w_o,
             ln1_g=ln1_g, ln1_b=ln1_b, w_ffn_gate=w_ffn_gate, w_ffn_up=w_ffn_up,
             w_ffn_down=w_ffn_down, ln2_g=ln2_g, ln2_b=ln2_b)
    consts = _constants()
    mod = _adaln_mod(jnp.concatenate([c_prompt, c_sample], axis=0), w_ada, b_ada)
    xp = x_prompt
    xs = x_sample.reshape(DEC_BATCH, D_MODEL)
    ssd_p, conv_p, conv_s, v_s = [], [], [], []
    state_in = state_ssd.reshape(DEPTH, DEC_BATCH, D_INNER, STATE)
    ssd_s = None
    for i in range(DEPTH):
        lw = _layer_weights(i, p)
        mod_p = mod[i, :BATCH].reshape(BATCH, 6, D_MODEL)
        mod_s = mod[i, BATCH:]
        x1, sp, cp = _prompt_mixer(xp, mod_p, lw, consts)
        xp = _prompt_ffn(x1, mod_p, lw)
        ssd_p.append(sp.reshape(BATCH, HEADS, HEAD_DIM, STATE))
        conv_p.append(cp)
        cstate = state_conv[i].reshape(DEC_BATCH, (CONV_K - 1) * CONV_DIM)
        v, cs, da, xdtt, bm, cm, cmpart, zs, gs, xsd = _sample_pre(xs, mod_s, cstate, lw, consts)
        ssd_s, y = _sample_ssd(i, da[:, :HEADS], xdtt, bm, cm, state_in, ssd_s)
        xs = _sample_post(xs, mod_s, y, xsd, zs, gs, cmpart, lw)
        conv_s.append(cs.reshape(DEC_BATCH, CONV_K - 1, CONV_DIM))
        v_s.append(v.reshape(DEC_BATCH, 1, D_CM))
    return (xp, xs.reshape(DEC_BATCH, 1, D_MODEL), jnp.stack(ssd_p), jnp.stack(conv_p),
            ssd_s.reshape(DEPTH, DEC_BATCH, HEADS, HEAD_DIM, STATE), jnp.stack(conv_s), jnp.stack(v_s))
```

```python
import functools
import math

import numpy as np
import jax
import jax.numpy as jnp
from jax import lax
from jax.experimental import pallas as pl
from jax.experimental.pallas import tpu as pltpu

D_MODEL = 1024
BATCH = 8
SEQ = 2048
DEPTH = 2
DEC_BATCH = 128
CHUNK = 128
D_CM = D_MODEL
CM_GROUPS = 8
D_INNER = 2048
HEAD_DIM = 64
HEADS = 32
SSD_GROUPS = 4
HEADS_PER_GROUP = HEADS // SSD_GROUPS
STATE = 128
CONV_K = 4
CONV_DIM = D_INNER + 2 * SSD_GROUPS * STATE
D_FF = 2816
ALPHA = (2 * DEPTH) ** 0.25
LN_EPS = 1e-5
REF_DT = 2 * D_CM + D_INNER + CONV_DIM
REF_GATES = REF_DT + HEADS
IN_COLS = REF_GATES + 2 * D_MODEL

LANES = 128
DT_COPIES = 3
C_U = 0
C_V = C_U + D_CM
C_Z = C_V + D_CM
C_XBC = C_Z + D_INNER
C_GCM = C_XBC + CONV_DIM
C_GSSD = C_GCM + D_MODEL
C_DT = C_GSSD + D_MODEL
IN_COLS_P = C_DT + LANES

TILE_L = 256
TILE_F = 512
SSD_BB = 4
VMEM_LIMIT = 56 * 1024 * 1024

F32 = jnp.float32
BF16 = jnp.bfloat16


def _bdot(a, b):
    return jnp.dot(a, b, preferred_element_type=F32)


def _gelu(x):
    return 0.5 * x * (1.0 + lax.erf(x * np.float32(math.sqrt(0.5))))


def _sigmoid(x):
    return 1.0 / (1.0 + jnp.exp(-x))


def _silu(x):
    return x * _sigmoid(x)


def _softplus(x):
    return jnp.maximum(x, 0.0) + jnp.log1p(jnp.exp(-jnp.abs(x)))


def _layer_norm(x, g, b):
    mu = jnp.mean(x, axis=-1, keepdims=True)
    xc = x - mu
    var = jnp.mean(xc * xc, axis=-1, keepdims=True)
    return xc * lax.rsqrt(var + LN_EPS) * g + b


def _rms_norm(x, g):
    return x * lax.rsqrt(jnp.mean(x * x, axis=-1, keepdims=True) + LN_EPS) * g


def _pack3(x):
    hi = x.astype(BF16).astype(F32)
    r1 = x - hi
    mid = r1.astype(BF16).astype(F32)
    r2 = r1 - mid
    lane = lax.broadcasted_iota(jnp.int32, x.shape, 1)
    return jnp.where(lane < HEADS, x, jnp.where(lane < 2 * HEADS, r1, r2)).astype(BF16)


def _const_spec(shape, single=True):
    nd = len(shape)
    kw = {"pipeline_mode": pl.Buffered(1)} if single else {}
    return pl.BlockSpec(shape, lambda *_: (0,) * nd, **kw)


def _layer_spec(shape, layer):
    nd = len(shape)
    return pl.BlockSpec((None,) + tuple(shape), lambda *_: (layer,) + (0,) * nd,
                        pipeline_mode=pl.Buffered(1))


def _cast_kernel(w_ref, o_ref):
    o_ref[...] = w_ref[...].astype(BF16)


def _to_bf16(w, rows):
    depth, r, c = w.shape
    return pl.pallas_call(
        _cast_kernel,
        grid=(depth, r // rows),
        in_specs=[pl.BlockSpec((None, rows, c), lambda l, i: (l, i, 0))],
        out_specs=pl.BlockSpec((None, rows, c), lambda l, i: (l, i, 0)),
        out_shape=jax.ShapeDtypeStruct((depth, r, c), BF16),
        compiler_params=pltpu.CompilerParams(
            dimension_semantics=("parallel", "parallel"), vmem_limit_bytes=VMEM_LIMIT),
        name="cast_bf16",
    )(w)


def _win_prep_kernel(w_ref, o_ref):
    o_ref[:, :C_GCM] = w_ref[:, :REF_DT].astype(BF16)
    o_ref[:, C_GCM:C_DT] = w_ref[:, REF_GATES:IN_COLS].astype(BF16)
    blk = w_ref[:, REF_DT:REF_DT + LANES]
    lane = lax.broadcasted_iota(jnp.int32, blk.shape, 1)
    dt0 = jnp.where(lane < HEADS, blk, 0.0)
    dt3 = dt0
    for k in range(1, DT_COPIES):
        dt3 = dt3 + pltpu.roll(dt0, k * HEADS, 1)
    o_ref[:, C_DT:] = dt3.astype(BF16)


def _win_prep(w_in, rows=256):
    return pl.pallas_call(
        _win_prep_kernel,
        grid=(DEPTH, D_MODEL // rows),
        in_specs=[pl.BlockSpec((None, rows, IN_COLS), lambda l, i: (l, i, 0))],
        out_specs=pl.BlockSpec((None, rows, IN_COLS_P), lambda l, i: (l, i, 0)),
        out_shape=jax.ShapeDtypeStruct((DEPTH, D_MODEL, IN_COLS_P), BF16),
        compiler_params=pltpu.CompilerParams(
            dimension_semantics=("parallel", "parallel"), vmem_limit_bytes=VMEM_LIMIT),
        name="w_in_prep",
    )(w_in)


def _mod_kernel(c_ref, w_ref, b_ref, o_ref):
    s = _silu(c_ref[...])
    o_ref[...] = jnp.dot(s, w_ref[...], preferred_element_type=F32,
                         precision=lax.Precision.HIGHEST) + b_ref[...]


def _adaln_mod(c_all, w_ada, b_ada):
    rows = c_all.shape[0]
    nblk = 6
    return pl.pallas_call(
        _mod_kernel,
        grid=(DEPTH, nblk),
        in_specs=[
            pl.BlockSpec((rows, D_MODEL), lambda l, k: (0, 0)),
            pl.BlockSpec((None, D_MODEL, D_MODEL), lambda l, k: (l, 0, k)),
            pl.BlockSpec((None, 1, D_MODEL), lambda l, k: (l, 0, k)),
        ],
        out_specs=pl.BlockSpec((None, rows, D_MODEL), lambda l, k: (l, 0, k)),
        out_shape=jax.ShapeDtypeStruct((DEPTH, rows, nblk * D_MODEL), F32),
        compiler_params=pltpu.CompilerParams(
            dimension_semantics=("parallel", "parallel"), vmem_limit_bytes=VMEM_LIMIT),
        name="adaln_mod",
    )(c_all, w_ada, b_ada.reshape(DEPTH, 1, nblk * D_MODEL))


def _cm_branch(u_v, lnvg, lnvb):
    u = _gelu(u_v[:, :D_CM])
    v = _layer_norm(_gelu(u_v[:, D_CM:]), lnvg, lnvb)
    return u, v


def _post_mixer(x, g1, merged_bf16, wo_ref, ln1g, ln1b):
    o = _bdot(merged_bf16, wo_ref[...])
    return _layer_norm(ALPHA * x + g1 * o, ln1g, ln1b)


def _ffn(x1, sh2, sc2, g2, wg_ref, wu_ref, wd_ref, ln2g, ln2b):
    h2 = (x1 * (1.0 + sc2) + sh2).astype(BF16)
    gate = _bdot(h2, wg_ref[...])
    up = _bdot(h2, wu_ref[...])
    act = (_silu(gate) * up).astype(BF16)
    ffn = _bdot(act, wd_ref[...])
    return _layer_norm(ALPHA * x1 + g2 * ffn, ln2g, ln2b)


def _ssd_chunk(xs_c, b_c, c_c, dtseg, dtb, alog, tri, e64, state_ref, y_ref, r0):
    row = lax.broadcasted_iota(jnp.int32, (CHUNK, LANES), 0)
    lane = lax.broadcasted_iota(jnp.int32, (CHUNK, LANES), 1)
    tril = row >= lane
    low_half = lane < HEAD_DIM

    dt = _softplus(dtseg + dtb)
    a = -jnp.exp(alog)
    da = dt * a
    hi = da.astype(BF16)
    r1 = da - hi.astype(F32)
    mid = r1.astype(BF16)
    lo = (r1 - mid.astype(F32)).astype(BF16)
    acum = _bdot(tri, hi) + _bdot(tri, mid) + _bdot(tri, lo)
    shift_t = acum.T - jnp.log(dt.T)
    alast = acum[CHUNK - 1:CHUNK, :]
    w = dt * jnp.exp(alast - acum)
    wexp = _bdot(_pack3(w), e64)
    eexp = _bdot(_pack3(jnp.exp(acum)), e64)
    cd = jnp.broadcast_to(jnp.exp(alast), (8, LANES))
    cdexp = _bdot(_pack3(cd), e64)[0:1, :]
    xs_b = xs_c.astype(BF16)
    zero_b = jnp.zeros((CHUNK, LANES), BF16)

    gw = HEADS_PER_GROUP * HEAD_DIM
    for g in range(SSD_GROUPS):
        gc = g * gw
        bg = b_c[:, g * STATE:(g + 1) * STATE]
        cgb = c_c[:, g * STATE:(g + 1) * STATE].astype(BF16)
        cb = lax.dot_general(cgb, bg.astype(BF16), (((1,), (1,)), ((), ())),
                             preferred_element_type=F32)
        y_off = eexp[:, gc:gc + gw] * _bdot(cgb, state_ref[:, gc:gc + gw].astype(BF16))
        for pr in range(HEADS_PER_GROUP // 2):
            h0 = g * HEADS_PER_GROUP + 2 * pr
            pieces = []
            for h in (h0, h0 + 1):
                seg = jnp.broadcast_to(acum[:, h:h + 1], (CHUNK, LANES)) - shift_t[h:h + 1, :]
                pieces.append((cb * jnp.exp(jnp.where(tril, seg, -jnp.inf))).astype(BF16))
            lhs = jnp.concatenate(pieces, axis=1)
            pc = h0 * HEAD_DIM
            xs_pair = xs_b[:, pc:pc + LANES]
            rhs = jnp.concatenate([jnp.where(low_half, xs_pair, zero_b),
                                   jnp.where(low_half, zero_b, xs_pair)], axis=0)
            y_ref[r0:r0 + CHUNK, pc:pc + LANES] = (
                _bdot(lhs, rhs) + y_off[:, pc - gc:pc - gc + LANES])
        xw = (xs_c[:, gc:gc + gw] * wexp[:, gc:gc + gw]).astype(BF16)
        upd = _bdot(bg.T.astype(BF16), xw)
        state_ref[:, gc:gc + gw] = state_ref[:, gc:gc + gw] * cdexp[:, gc:gc + gw] + upd


def _p1_kernel(x_ref, mod_ref, win_ref, bg_ref, lnvg_ref, lnvb_ref, wsp_ref, bsp_ref,
               convw_ref, convb_ref, dtb_ref, alog_ref, dskip_ref, normw_ref,
               wcm_ref, wssd_ref, wo_ref, ln1g_ref, ln1b_ref, tri_ref, e64_ref,
               x1_ref, ssd_ref, conv_ref,
               state_ref, cbuf_ref, y_ref, *, tl):
    j = pl.program_id(1)
    last = pl.num_programs(1) - 1
    nc = tl // CHUNK

    @pl.when(j == 0)
    def _():
        state_ref[...] = jnp.zeros_like(state_ref)
        cbuf_ref[0:8, :] = jnp.zeros((8, CONV_DIM), F32)

    x = x_ref[...]
    sh1 = mod_ref[0:1, :]
    sc1 = mod_ref[1:2, :]
    g1 = mod_ref[2:3, :]
    h = (x * (1.0 + sc1) + sh1).astype(BF16)

    cbuf_ref[8:8 + tl, :] = _bdot(h, win_ref[:, C_XBC:C_GCM])
    dtseg = _bdot(h, win_ref[:, C_DT:IN_COLS_P])
    u_v = _bdot(h, win_ref[:, C_U:C_Z])

    acc = convb_ref[...]
    for k in range(CONV_K):
        acc = acc + cbuf_ref[5 + k:5 + k + tl, :] * convw_ref[k:k + 1, :]
    tail = cbuf_ref[tl + 5:tl + 8, :]
    cbuf_ref[5:8, :] = tail
    xbc_c = _silu(acc)
    xs = xbc_c[:, :D_INNER]

    z = _bdot(h, win_ref[:, C_Z:C_XBC])
    gcm_logit = _bdot(h, win_ref[:, C_GCM:C_GSSD])
    gssd_logit = _bdot(h, win_ref[:, C_GSSD:C_DT])

    u, v = _cm_branch(u_v, lnvg_ref[...], lnvb_ref[...])
    vb = v.astype(BF16)
    row = lax.broadcasted_iota(jnp.int32, (CHUNK, CHUNK), 0)
    col = lax.broadcasted_iota(jnp.int32, (CHUNK, CHUNK), 1)
    wmix = [jnp.where(row >= col, wsp_ref[g], 0.0).astype(BF16) for g in range(CM_GROUPS)]

    for c in range(nc):
        r0 = c * CHUNK
        _ssd_chunk(xs[r0:r0 + CHUNK], xbc_c[r0:r0 + CHUNK, D_INNER:D_INNER + SSD_GROUPS * STATE],
                   xbc_c[r0:r0 + CHUNK, D_INNER + SSD_GROUPS * STATE:], dtseg[r0:r0 + CHUNK],
                   dtb_ref[...], alog_ref[...], tri_ref[...], e64_ref[...],
                   state_ref, y_ref, r0)

    rows = []
    for c in range(nc):
        cols = []
        for g in range(CM_GROUPS):
            cols.append(_bdot(wmix[g], vb[c * CHUNK:(c + 1) * CHUNK, g * LANES:(g + 1) * LANES]))
        rows.append(jnp.concatenate(cols, axis=1) + bsp_ref[...])
    mix = jnp.concatenate(rows, axis=0) if nc > 1 else rows[0]
    out_cm = (u * mix).astype(BF16)
    merged = _sigmoid(gcm_logit + bg_ref[:, :D_MODEL]) * _bdot(out_cm, wcm_ref[...])

    yz = (y_ref[...] + xs * dskip_ref[...]) * _silu(z)
    yn = _rms_norm(yz, normw_ref[...]).astype(BF16)
    merged = merged + _sigmoid(gssd_logit + bg_ref[:, D_MODEL:]) * _bdot(yn, wssd_ref[...])
    x1_ref[...] = _post_mixer(x, g1, merged.astype(BF16), wo_ref, ln1g_ref[...], ln1b_ref[...])

    @pl.when(j == last)
    def _():
        conv_ref[...] = cbuf_ref[5:8, :]
        ssd_ref[...] = state_ref[...].T


def _mod_prompt_spec(layer):
    return pl.BlockSpec((None, None, 6, D_MODEL), lambda b, j: (layer, DEC_BATCH + b, 0, 0))


def _prompt_mixer(layer, x, mod4, P, consts):
    tl = TILE_L
    nj = SEQ // tl
    ls = lambda *shape: _layer_spec(shape, layer)
    in_specs = [
        pl.BlockSpec((None, tl, D_MODEL), lambda b, j: (b, j, 0)),
        _mod_prompt_spec(layer),
        ls(D_MODEL, IN_COLS_P),
        ls(1, 2 * D_MODEL), ls(1, D_CM), ls(1, D_CM),
        ls(CM_GROUPS, CHUNK, CHUNK), ls(CHUNK, D_CM),
        ls(CONV_K, CONV_DIM), ls(1, CONV_DIM), ls(1, LANES), ls(1, LANES),
        ls(1, D_INNER), ls(1, D_INNER),
        ls(D_CM, D_MODEL), ls(D_INNER, D_MODEL), ls(D_MODEL, D_MODEL),
        ls(1, D_MODEL), ls(1, D_MODEL),
        _const_spec((CHUNK, CHUNK)), _const_spec((LANES, D_INNER)),
    ]
    out_specs = [
        pl.BlockSpec((None, tl, D_MODEL), lambda b, j: (b, j, 0)),
        pl.BlockSpec((None, D_INNER, STATE), lambda b, j: (b, 0, 0)),
        pl.BlockSpec((None, CONV_K - 1, CONV_DIM), lambda b, j: (b, 0, 0)),
    ]
    out_shape = [
        jax.ShapeDtypeStruct((BATCH, SEQ, D_MODEL), F32),
        jax.ShapeDtypeStruct((BATCH, D_INNER, STATE), F32),
        jax.ShapeDtypeStruct((BATCH, CONV_K - 1, CONV_DIM), F32),
    ]
    return pl.pallas_call(
        functools.partial(_p1_kernel, tl=tl),
        grid=(BATCH, nj),
        in_specs=in_specs, out_specs=out_specs, out_shape=out_shape,
        scratch_shapes=[pltpu.VMEM((STATE, D_INNER), F32),
                        pltpu.VMEM((tl + 8, CONV_DIM), F32),
                        pltpu.VMEM((tl, D_INNER), F32)],
        compiler_params=pltpu.CompilerParams(
            dimension_semantics=("parallel", "arbitrary"), vmem_limit_bytes=VMEM_LIMIT),
        name="prompt_mixer",
    )(x, mod4, P["w_in"], P["b_gate"], P["ln_v_g"], P["ln_v_b"], P["w_spatial"], P["bsp_full"],
      P["conv_w"], P["conv_b"], P["dt_bias3"], P["a_log3"], P["dskip_exp"], P["ssd_norm_w"],
      P["w_cm_br"], P["w_ssd_br"], P["w_o"], P["ln1_g"], P["ln1_b"],
      consts["tri"], consts["e64"])


def _p2_kernel(x_ref, mod_ref, wg_ref, wu_ref, wd_ref, ln2g_ref, ln2b_ref, o_ref):
    o_ref[...] = _ffn(x_ref[...], mod_ref[3:4, :], mod_ref[4:5, :], mod_ref[5:6, :],
                      wg_ref, wu_ref, wd_ref, ln2g_ref[...], ln2b_ref[...])


def _prompt_ffn(layer, x1, mod4, P):
    tf = TILE_F
    ls = lambda *shape: _layer_spec(shape, layer)
    return pl.pallas_call(
        _p2_kernel,
        grid=(BATCH, SEQ // tf),
        in_specs=[
            pl.BlockSpec((None, tf, D_MODEL), lambda b, j: (b, j, 0)),
            _mod_prompt_spec(layer),
            ls(D_MODEL, D_FF), ls(D_MODEL, D_FF), ls(D_FF, D_MODEL),
            ls(1, D_MODEL), ls(1, D_MODEL),
        ],
        out_specs=pl.BlockSpec((None, tf, D_MODEL), lambda b, j: (b, j, 0)),
        out_shape=jax.ShapeDtypeStruct((BATCH, SEQ, D_MODEL), F32),
        compiler_params=pltpu.CompilerParams(
            dimension_semantics=("parallel", "parallel"), vmem_limit_bytes=VMEM_LIMIT),
        name="prompt_ffn",
    )(x1, mod4, P["w_ffn_gate"], P["w_ffn_up"], P["w_ffn_down"], P["ln2_g"], P["ln2_b"])


def _s_pre_kernel(x_ref, mod_ref, win_ref, bg_ref, lnvg_ref, lnvb_ref, wsp0_ref, bsp0_ref,
                  convw_ref, convb_ref, cstate_ref, dtb_ref, alog_ref, dskip_ref, wcm_ref, e64_ref,
                  v_ref, conv_ref, da_ref, xdtt_ref, b_ref, c_ref, cmpart_ref, zs_ref, gs_ref, xsd_ref):
    x = x_ref[...]
    sh1 = mod_ref[:, 0:D_MODEL]
    sc1 = mod_ref[:, D_MODEL:2 * D_MODEL]
    h = (x * (1.0 + sc1) + sh1).astype(BF16)

    u, v = _cm_branch(_bdot(h, win_ref[:, C_U:C_Z]), lnvg_ref[...], lnvb_ref[...])
    v_ref[...] = v
    out_cm = (u * (wsp0_ref[...] * v + bsp0_ref[...])).astype(BF16)
    gate_cm = _sigmoid(_bdot(h, win_ref[:, C_GCM:C_GSSD]) + bg_ref[:, :D_MODEL])
    cmpart_ref[...] = gate_cm * _bdot(out_cm, wcm_ref[...])

    xbc = _bdot(h, win_ref[:, C_XBC:C_GCM])
    acc = convb_ref[...]
    for k in range(CONV_K - 1):
        acc = acc + cstate_ref[:, k * CONV_DIM:(k + 1) * CONV_DIM] * convw_ref[k:k + 1, :]
    acc = acc + xbc * convw_ref[CONV_K - 1:CONV_K, :]
    conv_ref[:, 0:(CONV_K - 2) * CONV_DIM] = cstate_ref[:, CONV_DIM:(CONV_K - 1) * CONV_DIM]
    conv_ref[:, (CONV_K - 2) * CONV_DIM:] = xbc
    xbc_c = _silu(acc)
    xs = xbc_c[:, :D_INNER]
    b_ref[...] = xbc_c[:, D_INNER:D_INNER + SSD_GROUPS * STATE]
    c_ref[...] = xbc_c[:, D_INNER + SSD_GROUPS * STATE:]
    dt = _softplus(_bdot(h, win_ref[:, C_DT:IN_COLS_P]) + dtb_ref[...])
    da_ref[...] = jnp.exp(dt * (-jnp.exp(alog_ref[...])))
    xdt = xs * _bdot(_pack3(dt), e64_ref[...])
    xdtt_ref[...] = xdt.T.astype(BF16)
    xsd_ref[...] = xs * dskip_ref[...]
    zs_ref[...] = _silu(_bdot(h, win_ref[:, C_Z:C_XBC]))
    gs_ref[...] = _sigmoid(_bdot(h, win_ref[:, C_GSSD:C_DT]) + bg_ref[:, D_MODEL:])


def _sample_pre(layer, x, mod, cstate, P, consts):
    n = DEC_BATCH
    ls = lambda *shape: _layer_spec(shape, layer)
    in_specs = [
        _const_spec((n, D_MODEL)), ls(n, 6 * D_MODEL), ls(D_MODEL, IN_COLS_P),
        ls(1, 2 * D_MODEL), ls(1, D_CM), ls(1, D_CM), ls(1, D_CM), ls(1, D_CM),
        ls(CONV_K, CONV_DIM), ls(1, CONV_DIM), ls(n, (CONV_K - 1) * CONV_DIM),
        ls(1, LANES), ls(1, LANES), ls(1, D_INNER), ls(D_CM, D_MODEL), _const_spec((LANES, D_INNER)),
    ]
    outs = [((n, D_CM), F32), ((n, (CONV_K - 1) * CONV_DIM), F32), ((n, LANES), F32),
            ((D_INNER, n), BF16), ((n, SSD_GROUPS * STATE), F32), ((n, SSD_GROUPS * STATE), F32),
            ((n, D_MODEL), F32), ((n, D_INNER), F32), ((n, D_MODEL), F32), ((n, D_INNER), F32)]
    return pl.pallas_call(
        _s_pre_kernel,
        grid=(1,),
        in_specs=in_specs,
        out_specs=[_const_spec(s, single=False) for s, _ in outs],
        out_shape=[jax.ShapeDtypeStruct(s, d) for s, d in outs],
        compiler_params=pltpu.CompilerParams(
            dimension_semantics=("arbitrary",), vmem_limit_bytes=VMEM_LIMIT),
        name="sample_pre",
    )(x, mod, P["w_in"], P["b_gate"], P["ln_v_g"], P["ln_v_b"], P["wsp0"], P["bsp0"],
      P["conv_w"], P["conv_b"], cstate, P["dt_bias3"], P["a_log3"], P["dskip_exp"], P["w_cm_br"],
      consts["e64"])


def _s_ssd_kernel(da_ref, xdtt_ref, b_ref, c_ref, st_ref, *rest):
    sto_ref, y_ref = rest[-2:]
    i = pl.program_id(0)

    @pl.when(i == 0)
    def _():
        y_ref[...] = jnp.zeros_like(y_ref)

    rows = lax.broadcasted_iota(jnp.int32, (DEC_BATCH, STATE), 0)
    rows8 = lax.broadcasted_iota(jnp.int32, (8, STATE), 0)
    gw = HEADS_PER_GROUP * HEAD_DIM
    for q in range(SSD_BB):
        b = i * SSD_BB + q
        b8 = pl.multiple_of((b // 8) * 8, 8)
        for g in range(SSD_GROUPS):
            bg = jnp.where(rows == b, b_ref[:, g * STATE:(g + 1) * STATE], 0.0).astype(BF16)
            xb = _bdot(xdtt_ref[g * gw:(g + 1) * gw, :], bg)
            blocks = []
            for hh in range(HEADS_PER_GROUP):
                hd = g * HEADS_PER_GROUP + hh
                r = hd * HEAD_DIM
                blk = st_ref[q, r:r + HEAD_DIM, :] * da_ref[b, hd] + xb[hh * HEAD_DIM:(hh + 1) * HEAD_DIM, :]
                sto_ref[q, r:r + HEAD_DIM, :] = blk
                blocks.append(blk.astype(BF16))
            newg = jnp.concatenate(blocks, axis=0)
            cg = jnp.where(rows8 == b - b8, c_ref[pl.ds(b8, 8), g * STATE:(g + 1) * STATE], 0.0).astype(BF16)
            yg = lax.dot_general(cg, newg, (((1,), (1,)), ((), ())), preferred_element_type=F32)
            y_ref[pl.ds(b8, 8), g * gw:(g + 1) * gw] += yg


def _sample_ssd(layer, da, xdtt, bm, cm, state, stacked):
    n = DEC_BATCH
    in_specs = [
        pl.BlockSpec((D_INNER, n), lambda i, da: (0, 0)),
        pl.BlockSpec((n, SSD_GROUPS * STATE), lambda i, da: (0, 0)),
        pl.BlockSpec((n, SSD_GROUPS * STATE), lambda i, da: (0, 0)),
        pl.BlockSpec((None, SSD_BB, D_INNER, STATE), lambda i, da: (layer, i, 0, 0)),
    ]
    args = [da, xdtt, bm, cm, state]
    aliases = {}
    if stacked is not None:
        in_specs.append(pl.BlockSpec(memory_space=pl.ANY))
        args.append(stacked)
        aliases = {len(args) - 1: 0}
    grid_spec = pltpu.PrefetchScalarGridSpec(
        num_scalar_prefetch=1,
        grid=(n // SSD_BB,),
        in_specs=in_specs,
        out_specs=[
            pl.BlockSpec((None, SSD_BB, D_INNER, STATE), lambda i, da: (layer, i, 0, 0)),
            pl.BlockSpec((n, D_INNER), lambda i, da: (0, 0)),
        ],
    )
    return pl.pallas_call(
        _s_ssd_kernel,
        grid_spec=grid_spec,
        out_shape=[jax.ShapeDtypeStruct((DEPTH, n, D_INNER, STATE), F32),
                   jax.ShapeDtypeStruct((n, D_INNER), F32)],
        input_output_aliases=aliases,
        compiler_params=pltpu.CompilerParams(
            dimension_semantics=("arbitrary",), vmem_limit_bytes=VMEM_LIMIT),
        name="sample_ssd",
    )(*args)


def _s_post_kernel(x_ref, mod_ref, y_ref, xsd_ref, zs_ref, gs_ref, cmpart_ref, normw_ref,
                   wssd_ref, wo_ref, ln1g_ref, ln1b_ref, wg_ref, wu_ref, wd_ref, ln2g_ref, ln2b_ref, o_ref):
    md = lambda k: mod_ref[:, k * D_MODEL:(k + 1) * D_MODEL]
    yn = _rms_norm((y_ref[...] + xsd_ref[...]) * zs_ref[...], normw_ref[...]).astype(BF16)
    merged = cmpart_ref[...] + gs_ref[...] * _bdot(yn, wssd_ref[...])
    x1 = _post_mixer(x_ref[...], md(2), merged.astype(BF16), wo_ref, ln1g_ref[...], ln1b_ref[...])
    o_ref[...] = _ffn(x1, md(3), md(4), md(5), wg_ref, wu_ref, wd_ref, ln2g_ref[...], ln2b_ref[...])


def _sample_post(layer, x, mod, y, xsd, zs, gs, cmpart, P):
    n = DEC_BATCH
    ls = lambda *shape: _layer_spec(shape, layer)
    cs = _const_spec
    in_specs = [
        cs((n, D_MODEL)), ls(n, 6 * D_MODEL), cs((n, D_INNER)), cs((n, D_INNER)), cs((n, D_INNER)),
        cs((n, D_MODEL)), cs((n, D_MODEL)), ls(1, D_INNER), ls(D_INNER, D_MODEL), ls(D_MODEL, D_MODEL),
        ls(1, D_MODEL), ls(1, D_MODEL), ls(D_MODEL, D_FF), ls(D_MODEL, D_FF), ls(D_FF, D_MODEL),
        ls(1, D_MODEL), ls(1, D_MODEL),
    ]
    return pl.pallas_call(
        _s_post_kernel,
        grid=(1,),
        in_specs=in_specs,
        out_specs=_const_spec((n, D_MODEL), single=False),
        out_shape=jax.ShapeDtypeStruct((n, D_MODEL), F32),
        compiler_params=pltpu.CompilerParams(
            dimension_semantics=("arbitrary",), vmem_limit_bytes=VMEM_LIMIT),
        name="sample_post",
    )(x, mod, y, xsd, zs, gs, cmpart, P["ssd_norm_w"], P["w_ssd_br"], P["w_o"], P["ln1_g"], P["ln1_b"],
      P["w_ffn_gate"], P["w_ffn_up"], P["w_ffn_down"], P["ln2_g"], P["ln2_b"])


def _constants():
    tri = np.tril(np.ones((CHUNK, CHUNK), np.float32))
    e64 = np.zeros((LANES, D_INNER), np.float32)
    for r in range(DT_COPIES * HEADS):
        hd = r % HEADS
        e64[r, hd * HEAD_DIM:(hd + 1) * HEAD_DIM] = 1.0
    return {"tri": jnp.asarray(tri, BF16), "e64": jnp.asarray(e64, BF16)}


def _params(p):
    rows = lambda a: a[:, None, :]
    pad = jnp.zeros((DEPTH, LANES - DT_COPIES * HEADS), F32)
    return {
        "w_in": _win_prep(p["w_in"]),
        "b_gate": rows(p["b_gate"]),
        "ln_v_g": rows(p["ln_v_g"]), "ln_v_b": rows(p["ln_v_b"]),
        "w_spatial": p["w_spatial"],
        "bsp_full": jnp.repeat(jnp.swapaxes(p["b_spatial"], 1, 2), LANES, axis=2),
        "wsp0": rows(jnp.repeat(p["w_spatial"][:, :, 0, 0], LANES, axis=1)),
        "bsp0": rows(jnp.repeat(p["b_spatial"][:, :, 0], LANES, axis=1)),
        "conv_w": p["conv_w"], "conv_b": rows(p["conv_b"]),
        "dt_bias3": rows(jnp.concatenate([p["dt_bias"]] * DT_COPIES + [pad], axis=1)),
        "a_log3": rows(jnp.concatenate([p["a_log"]] * DT_COPIES + [pad], axis=1)),
        "dskip_exp": rows(jnp.repeat(p["d_skip"], HEAD_DIM, axis=1)),
        "ssd_norm_w": rows(p["ssd_norm_w"]),
        "w_cm_br": _to_bf16(p["w_cm_br"], 512), "w_ssd_br": _to_bf16(p["w_ssd_br"], 512),
        "w_o": _to_bf16(p["w_o"], 512),
        "ln1_g": rows(p["ln1_g"]), "ln1_b": rows(p["ln1_b"]),
        "w_ffn_gate": _to_bf16(p["w_ffn_gate"], 512), "w_ffn_up": _to_bf16(p["w_ffn_up"], 512),
        "w_ffn_down": _to_bf16(p["w_ffn_down"], D_FF // 4),
        "ln2_g": rows(p["ln2_g"]), "ln2_b": rows(p["ln2_b"]),
    }


def kernel(x_prompt, x_sample, state_ssd, state_conv, c_prompt, c_sample, w_ada, b_ada, w_in, b_gate,
           ln_v_g, ln_v_b, w_spatial, b_spatial, conv_w, conv_b, dt_bias, a_log, d_skip, ssd_norm_w,
           w_cm_br, w_ssd_br, w_o, ln1_g, ln1_b, w_ffn_gate, w_ffn_up, w_ffn_down, ln2_g, ln2_b):
    P = _params(dict(w_in=w_in, b_gate=b_gate, ln_v_g=ln_v_g, ln_v_b=ln_v_b, w_spatial=w_spatial,
                     b_spatial=b_spatial, conv_w=conv_w, conv_b=conv_b, dt_bias=dt_bias, a_log=a_log,
                     d_skip=d_skip, ssd_norm_w=ssd_norm_w, w_cm_br=w_cm_br, w_ssd_br=w_ssd_br, w_o=w_o,
                     ln1_g=ln1_g, ln1_b=ln1_b, w_ffn_gate=w_ffn_gate, w_ffn_up=w_ffn_up,
                     w_ffn_down=w_ffn_down, ln2_g=ln2_g, ln2_b=ln2_b))
    consts = _constants()
    mod = _adaln_mod(jnp.concatenate([c_sample, c_prompt], axis=0), w_ada, b_ada)
    mod4 = mod.reshape(DEPTH, DEC_BATCH + BATCH, 6, D_MODEL)
    xp = x_prompt
    xs = x_sample.reshape(DEC_BATCH, D_MODEL)
    ssd_p, conv_p, conv_s, v_s = [], [], [], []
    state_in = state_ssd.reshape(DEPTH, DEC_BATCH, D_INNER, STATE)
    cstate = state_conv.reshape(DEPTH, DEC_BATCH, (CONV_K - 1) * CONV_DIM)
    ssd_s = None
    for i in range(DEPTH):
        x1, sp, cp = _prompt_mixer(i, xp, mod4, P, consts)
        xp = _prompt_ffn(i, x1, mod4, P)
        ssd_p.append(sp.reshape(BATCH, HEADS, HEAD_DIM, STATE))
        conv_p.append(cp)
        v, cs, da, xdtt, bm, cm, cmpart, zs, gs, xsd = _sample_pre(i, xs, mod, cstate, P, consts)
        ssd_s, y = _sample_ssd(i, da[:, :HEADS], xdtt, bm, cm, state_in, ssd_s)
        xs = _sample_post(i, xs, mod, y, xsd, zs, gs, cmpart, P)
        conv_s.append(cs.reshape(DEC_BATCH, CONV_K - 1, CONV_DIM))
        v_s.append(v.reshape(DEC_BATCH, 1, D_CM))
    return (xp, xs.reshape(DEC_BATCH, 1, D_MODEL), jnp.stack(ssd_p), jnp.stack(conv_p),
            ssd_s.reshape(DEPTH, DEC_BATCH, HEADS, HEAD_DIM, STATE), jnp.stack(conv_s), jnp.stack(v_s))
```

```python
import functools
import math

import numpy as np
import jax
import jax.numpy as jnp
from jax import lax
from jax.experimental import pallas as pl
from jax.experimental.pallas import tpu as pltpu

D_MODEL = 1024
BATCH = 8
SEQ = 2048
DEPTH = 2
DEC_BATCH = 128
CHUNK = 128
D_CM = D_MODEL
CM_GROUPS = 8
D_INNER = 2048
HEAD_DIM = 64
HEADS = 32
SSD_GROUPS = 4
HEADS_PER_GROUP = HEADS // SSD_GROUPS
STATE = 128
CONV_K = 4
CONV_DIM = D_INNER + 2 * SSD_GROUPS * STATE
D_FF = 2816
ALPHA = (2 * DEPTH) ** 0.25
LN_EPS = 1e-5
REF_DT = 2 * D_CM + D_INNER + CONV_DIM
REF_GATES = REF_DT + HEADS
IN_COLS = REF_GATES + 2 * D_MODEL

LANES = 128
DT_COPIES = 3
C_U = 0
C_V = C_U + D_CM
C_Z = C_V + D_CM
C_XBC = C_Z + D_INNER
C_GCM = C_XBC + CONV_DIM
C_GSSD = C_GCM + D_MODEL
C_DT = C_GSSD + D_MODEL
IN_COLS_P = C_DT + LANES

TILE_L = 256
TILE_F = 512
SSD_BB = 8
VMEM_LIMIT = 56 * 1024 * 1024

F32 = jnp.float32
BF16 = jnp.bfloat16


def _bdot(a, b):
    return jnp.dot(a, b, preferred_element_type=F32)


def _gelu(x):
    return 0.5 * x * (1.0 + lax.erf(x * np.float32(math.sqrt(0.5))))


def _sigmoid(x):
    return 1.0 / (1.0 + jnp.exp(-x))


def _silu(x):
    return x * _sigmoid(x)


def _softplus(x):
    return jnp.maximum(x, 0.0) + jnp.log1p(jnp.exp(-jnp.abs(x)))


def _layer_norm(x, g, b):
    mu = jnp.mean(x, axis=-1, keepdims=True)
    xc = x - mu
    var = jnp.mean(xc * xc, axis=-1, keepdims=True)
    return xc * lax.rsqrt(var + LN_EPS) * g + b


def _rms_norm(x, g):
    return x * lax.rsqrt(jnp.mean(x * x, axis=-1, keepdims=True) + LN_EPS) * g


def _pack3(x):
    hi = x.astype(BF16).astype(F32)
    r1 = x - hi
    mid = r1.astype(BF16).astype(F32)
    r2 = r1 - mid
    lane = lax.broadcasted_iota(jnp.int32, x.shape, 1)
    return jnp.where(lane < HEADS, x, jnp.where(lane < 2 * HEADS, r1, r2)).astype(BF16)


def _const_spec(shape, single=True):
    nd = len(shape)
    kw = {"pipeline_mode": pl.Buffered(1)} if single else {}
    return pl.BlockSpec(shape, lambda *_: (0,) * nd, **kw)


def _layer_spec(shape, layer):
    nd = len(shape)
    return pl.BlockSpec((None,) + tuple(shape), lambda *_: (layer,) + (0,) * nd,
                        pipeline_mode=pl.Buffered(1))


def _cast_kernel(w_ref, o_ref):
    o_ref[...] = w_ref[...].astype(BF16)


def _to_bf16(w, rows):
    depth, r, c = w.shape
    return pl.pallas_call(
        _cast_kernel,
        grid=(depth, r // rows),
        in_specs=[pl.BlockSpec((None, rows, c), lambda l, i: (l, i, 0))],
        out_specs=pl.BlockSpec((None, rows, c), lambda l, i: (l, i, 0)),
        out_shape=jax.ShapeDtypeStruct((depth, r, c), BF16),
        compiler_params=pltpu.CompilerParams(
            dimension_semantics=("parallel", "parallel"), vmem_limit_bytes=VMEM_LIMIT),
        name="cast_bf16",
    )(w)


def _win_prep(w_in):
    b16 = lambda a: a.astype(BF16)
    w_dt = b16(w_in[:, :, REF_DT:REF_GATES])
    pad = jnp.zeros((DEPTH, D_MODEL, LANES - DT_COPIES * HEADS), BF16)
    return jnp.concatenate([b16(w_in[:, :, :REF_DT]), b16(w_in[:, :, REF_GATES:])]
                           + [w_dt] * DT_COPIES + [pad], axis=-1)


def _mod_kernel(c_ref, w_ref, b_ref, os_ref, op_ref):
    s = _silu(c_ref[...])
    res = jnp.dot(s, w_ref[...], preferred_element_type=F32,
                  precision=lax.Precision.HIGHEST) + b_ref[...]
    os_ref[...] = res[:DEC_BATCH]
    op_ref[...] = res[DEC_BATCH:]


def _adaln_mod(c_all, w_ada, b_ada):
    rows = c_all.shape[0]
    nblk = 6
    return pl.pallas_call(
        _mod_kernel,
        grid=(DEPTH, nblk),
        in_specs=[
            pl.BlockSpec((rows, D_MODEL), lambda l, k: (0, 0)),
            pl.BlockSpec((None, D_MODEL, D_MODEL), lambda l, k: (l, 0, k)),
            pl.BlockSpec((None, 1, D_MODEL), lambda l, k: (l, 0, k)),
        ],
        out_specs=[pl.BlockSpec((None, DEC_BATCH, D_MODEL), lambda l, k: (l, 0, k)),
                   pl.BlockSpec((None, BATCH, D_MODEL), lambda l, k: (l, 0, k))],
        out_shape=[jax.ShapeDtypeStruct((DEPTH, DEC_BATCH, nblk * D_MODEL), F32),
                   jax.ShapeDtypeStruct((DEPTH, BATCH, nblk * D_MODEL), F32)],
        compiler_params=pltpu.CompilerParams(
            dimension_semantics=("parallel", "parallel"), vmem_limit_bytes=VMEM_LIMIT),
        name="adaln_mod",
    )(c_all, w_ada, b_ada.reshape(DEPTH, 1, nblk * D_MODEL))


def _cm_branch(u_v, lnvg, lnvb):
    u = _gelu(u_v[:, :D_CM])
    v = _layer_norm(_gelu(u_v[:, D_CM:]), lnvg, lnvb)
    return u, v


def _post_mixer(x, g1, merged_bf16, wo_ref, ln1g, ln1b):
    o = _bdot(merged_bf16, wo_ref[...])
    return _layer_norm(ALPHA * x + g1 * o, ln1g, ln1b)


def _ffn(x1, sh2, sc2, g2, wg_ref, wu_ref, wd_ref, ln2g, ln2b):
    h2 = (x1 * (1.0 + sc2) + sh2).astype(BF16)
    gate = _bdot(h2, wg_ref[...])
    up = _bdot(h2, wu_ref[...])
    act = (_silu(gate) * up).astype(BF16)
    ffn = _bdot(act, wd_ref[...])
    return _layer_norm(ALPHA * x1 + g2 * ffn, ln2g, ln2b)


def _ssd_chunk(xs_c, b_c, c_c, dtseg, dtb, alog, tri, e64, state_ref, y_ref, r0):
    row = lax.broadcasted_iota(jnp.int32, (CHUNK, LANES), 0)
    lane = lax.broadcasted_iota(jnp.int32, (CHUNK, LANES), 1)
    tril = row >= lane
    low_half = lane < HEAD_DIM

    dt = _softplus(dtseg + dtb)
    a = -jnp.exp(alog)
    da = dt * a
    hi = da.astype(BF16)
    r1 = da - hi.astype(F32)
    mid = r1.astype(BF16)
    lo = (r1 - mid.astype(F32)).astype(BF16)
    acum = _bdot(tri, hi) + _bdot(tri, mid) + _bdot(tri, lo)
    shift_t = acum.T - jnp.log(dt.T)
    alast = acum[CHUNK - 1:CHUNK, :]
    w = dt * jnp.exp(alast - acum)
    wexp = _bdot(_pack3(w), e64)
    eexp = _bdot(_pack3(jnp.exp(acum)), e64)
    cd = jnp.broadcast_to(jnp.exp(alast), (8, LANES))
    cdexp = _bdot(_pack3(cd), e64)[0:1, :]
    xs_b = xs_c.astype(BF16)
    zero_b = jnp.zeros((CHUNK, LANES), BF16)

    gw = HEADS_PER_GROUP * HEAD_DIM
    for g in range(SSD_GROUPS):
        gc = g * gw
        bg = b_c[:, g * STATE:(g + 1) * STATE]
        cgb = c_c[:, g * STATE:(g + 1) * STATE].astype(BF16)
        cb = lax.dot_general(cgb, bg.astype(BF16), (((1,), (1,)), ((), ())),
                             preferred_element_type=F32)
        y_off = eexp[:, gc:gc + gw] * _bdot(cgb, state_ref[:, gc:gc + gw].astype(BF16))
        for pr in range(HEADS_PER_GROUP // 2):
            h0 = g * HEADS_PER_GROUP + 2 * pr
            pieces = []
            for h in (h0, h0 + 1):
                seg = jnp.broadcast_to(acum[:, h:h + 1], (CHUNK, LANES)) - shift_t[h:h + 1, :]
                pieces.append((cb * jnp.exp(jnp.where(tril, seg, -jnp.inf))).astype(BF16))
            lhs = jnp.concatenate(pieces, axis=1)
            pc = h0 * HEAD_DIM
            xs_pair = xs_b[:, pc:pc + LANES]
            rhs = jnp.concatenate([jnp.where(low_half, xs_pair, zero_b),
                                   jnp.where(low_half, zero_b, xs_pair)], axis=0)
            y_ref[r0:r0 + CHUNK, pc:pc + LANES] = (
                _bdot(lhs, rhs) + y_off[:, pc - gc:pc - gc + LANES])
        xw = (xs_c[:, gc:gc + gw] * wexp[:, gc:gc + gw]).astype(BF16)
        upd = _bdot(bg.T.astype(BF16), xw)
        state_ref[:, gc:gc + gw] = state_ref[:, gc:gc + gw] * cdexp[:, gc:gc + gw] + upd


def _p1_kernel(x_ref, mod_ref, win_ref, bg_ref, lnvg_ref, lnvb_ref, wsp_ref, bsp_ref,
               convw_ref, convb_ref, dtb_ref, alog_ref, dskip_ref, normw_ref,
               wcm_ref, wssd_ref, wo_ref, ln1g_ref, ln1b_ref, tri_ref, e64_ref,
               x1_ref, ssd_ref, conv_ref,
               state_ref, cbuf_ref, y_ref, *, tl):
    j = pl.program_id(1)
    last = pl.num_programs(1) - 1
    nc = tl // CHUNK

    @pl.when(j == 0)
    def _():
        state_ref[...] = jnp.zeros_like(state_ref)
        cbuf_ref[0:8, :] = jnp.zeros((8, CONV_DIM), F32)

    x = x_ref[...]
    sh1 = mod_ref[0:1, :]
    sc1 = mod_ref[1:2, :]
    g1 = mod_ref[2:3, :]
    h = (x * (1.0 + sc1) + sh1).astype(BF16)

    cbuf_ref[8:8 + tl, :] = _bdot(h, win_ref[:, C_XBC:C_GCM])
    dtseg = _bdot(h, win_ref[:, C_DT:IN_COLS_P])
    u_v = _bdot(h, win_ref[:, C_U:C_Z])

    acc = convb_ref[...]
    for k in range(CONV_K):
        acc = acc + cbuf_ref[5 + k:5 + k + tl, :] * convw_ref[k:k + 1, :]
    tail = cbuf_ref[tl + 5:tl + 8, :]
    cbuf_ref[5:8, :] = tail
    xbc_c = _silu(acc)
    xs = xbc_c[:, :D_INNER]

    z = _bdot(h, win_ref[:, C_Z:C_XBC])
    gcm_logit = _bdot(h, win_ref[:, C_GCM:C_GSSD])
    gssd_logit = _bdot(h, win_ref[:, C_GSSD:C_DT])

    u, v = _cm_branch(u_v, lnvg_ref[...], lnvb_ref[...])
    vb = v.astype(BF16)
    row = lax.broadcasted_iota(jnp.int32, (CHUNK, CHUNK), 0)
    col = lax.broadcasted_iota(jnp.int32, (CHUNK, CHUNK), 1)
    wmix = [jnp.where(row >= col, wsp_ref[g], 0.0).astype(BF16) for g in range(CM_GROUPS)]

    for c in range(nc):
        r0 = c * CHUNK
        _ssd_chunk(xs[r0:r0 + CHUNK], xbc_c[r0:r0 + CHUNK, D_INNER:D_INNER + SSD_GROUPS * STATE],
                   xbc_c[r0:r0 + CHUNK, D_INNER + SSD_GROUPS * STATE:], dtseg[r0:r0 + CHUNK],
                   dtb_ref[...], alog_ref[...], tri_ref[...], e64_ref[...],
                   state_ref, y_ref, r0)

    rows = []
    for c in range(nc):
        cols = []
        for g in range(CM_GROUPS):
            cols.append(_bdot(wmix[g], vb[c * CHUNK:(c + 1) * CHUNK, g * LANES:(g + 1) * LANES]))
        rows.append(jnp.concatenate(cols, axis=1) + bsp_ref[...])
    mix = jnp.concatenate(rows, axis=0) if nc > 1 else rows[0]
    out_cm = (u * mix).astype(BF16)
    merged = _sigmoid(gcm_logit + bg_ref[:, :D_MODEL]) * _bdot(out_cm, wcm_ref[...])

    yz = (y_ref[...] + xs * dskip_ref[...]) * _silu(z)
    yn = _rms_norm(yz, normw_ref[...]).astype(BF16)
    merged = merged + _sigmoid(gssd_logit + bg_ref[:, D_MODEL:]) * _bdot(yn, wssd_ref[...])
    x1_ref[...] = _post_mixer(x, g1, merged.astype(BF16), wo_ref, ln1g_ref[...], ln1b_ref[...])

    @pl.when(j == last)
    def _():
        conv_ref[...] = cbuf_ref[5:8, :]
        ssd_ref[...] = state_ref[...].T


def _mod_prompt_spec(layer):
    return pl.BlockSpec((None, None, 6, D_MODEL), lambda b, j: (layer, b, 0, 0))


def _prompt_mixer(layer, x, mod4, P, consts):
    tl = TILE_L
    nj = SEQ // tl
    ls = lambda *shape: _layer_spec(shape, layer)
    in_specs = [
        pl.BlockSpec((None, tl, D_MODEL), lambda b, j: (b, j, 0)),
        _mod_prompt_spec(layer),
        ls(D_MODEL, IN_COLS_P),
        ls(1, 2 * D_MODEL), ls(1, D_CM), ls(1, D_CM),
        ls(CM_GROUPS, CHUNK, CHUNK), ls(CHUNK, D_CM),
        ls(CONV_K, CONV_DIM), ls(1, CONV_DIM), ls(1, LANES), ls(1, LANES),
        ls(1, D_INNER), ls(1, D_INNER),
        ls(D_CM, D_MODEL), ls(D_INNER, D_MODEL), ls(D_MODEL, D_MODEL),
        ls(1, D_MODEL), ls(1, D_MODEL),
        _const_spec((CHUNK, CHUNK)), _const_spec((LANES, D_INNER)),
    ]
    out_specs = [
        pl.BlockSpec((None, tl, D_MODEL), lambda b, j: (b, j, 0)),
        pl.BlockSpec((None, D_INNER, STATE), lambda b, j: (b, 0, 0)),
        pl.BlockSpec((None, CONV_K - 1, CONV_DIM), lambda b, j: (b, 0, 0)),
    ]
    out_shape = [
        jax.ShapeDtypeStruct((BATCH, SEQ, D_MODEL), F32),
        jax.ShapeDtypeStruct((BATCH, D_INNER, STATE), F32),
        jax.ShapeDtypeStruct((BATCH, CONV_K - 1, CONV_DIM), F32),
    ]
    return pl.pallas_call(
        functools.partial(_p1_kernel, tl=tl),
        grid=(BATCH, nj),
        in_specs=in_specs, out_specs=out_specs, out_shape=out_shape,
        scratch_shapes=[pltpu.VMEM((STATE, D_INNER), F32),
                        pltpu.VMEM((tl + 8, CONV_DIM), F32),
                        pltpu.VMEM((tl, D_INNER), F32)],
        compiler_params=pltpu.CompilerParams(
            dimension_semantics=("parallel", "arbitrary"), vmem_limit_bytes=VMEM_LIMIT),
        name="prompt_mixer",
    )(x, mod4, P["w_in"], P["b_gate"], P["ln_v_g"], P["ln_v_b"], P["w_spatial"], P["bsp_full"],
      P["conv_w"], P["conv_b"], P["dt_bias3"], P["a_log3"], P["dskip_exp"], P["ssd_norm_w"],
      P["w_cm_br"], P["w_ssd_br"], P["w_o"], P["ln1_g"], P["ln1_b"],
      consts["tri"], consts["e64"])


def _p2_kernel(x_ref, mod_ref, wg_ref, wu_ref, wd_ref, ln2g_ref, ln2b_ref, o_ref):
    o_ref[...] = _ffn(x_ref[...], mod_ref[3:4, :], mod_ref[4:5, :], mod_ref[5:6, :],
                      wg_ref, wu_ref, wd_ref, ln2g_ref[...], ln2b_ref[...])


def _prompt_ffn(layer, x1, mod4, P):
    tf = TILE_F
    ls = lambda *shape: _layer_spec(shape, layer)
    return pl.pallas_call(
        _p2_kernel,
        grid=(BATCH, SEQ // tf),
        in_specs=[
            pl.BlockSpec((None, tf, D_MODEL), lambda b, j: (b, j, 0)),
            _mod_prompt_spec(layer),
            ls(D_MODEL, D_FF), ls(D_MODEL, D_FF), ls(D_FF, D_MODEL),
            ls(1, D_MODEL), ls(1, D_MODEL),
        ],
        out_specs=pl.BlockSpec((None, tf, D_MODEL), lambda b, j: (b, j, 0)),
        out_shape=jax.ShapeDtypeStruct((BATCH, SEQ, D_MODEL), F32),
        compiler_params=pltpu.CompilerParams(
            dimension_semantics=("parallel", "parallel"), vmem_limit_bytes=VMEM_LIMIT),
        name="prompt_ffn",
    )(x1, mod4, P["w_ffn_gate"], P["w_ffn_up"], P["w_ffn_down"], P["ln2_g"], P["ln2_b"])


def _s_pre_kernel(x_ref, mod_ref, win_ref, bg_ref, lnvg_ref, lnvb_ref, wsp0_ref, bsp0_ref,
                  convw_ref, convb_ref, cstate_ref, dtb_ref, alog_ref, dskip_ref, wcm_ref, e64_ref,
                  v_ref, conv_ref, da_ref, xdtt_ref, b_ref, c_ref, cmpart_ref, zs_ref, gs_ref, xsd_ref):
    x = x_ref[...]
    sh1 = mod_ref[:, 0:D_MODEL]
    sc1 = mod_ref[:, D_MODEL:2 * D_MODEL]
    h = (x * (1.0 + sc1) + sh1).astype(BF16)

    u, v = _cm_branch(_bdot(h, win_ref[:, C_U:C_Z]), lnvg_ref[...], lnvb_ref[...])
    v_ref[...] = v
    out_cm = (u * (wsp0_ref[...] * v + bsp0_ref[...])).astype(BF16)
    gate_cm = _sigmoid(_bdot(h, win_ref[:, C_GCM:C_GSSD]) + bg_ref[:, :D_MODEL])
    cmpart_ref[...] = gate_cm * _bdot(out_cm, wcm_ref[...])

    xbc = _bdot(h, win_ref[:, C_XBC:C_GCM])
    acc = convb_ref[...]
    for k in range(CONV_K - 1):
        acc = acc + cstate_ref[:, k * CONV_DIM:(k + 1) * CONV_DIM] * convw_ref[k:k + 1, :]
    acc = acc + xbc * convw_ref[CONV_K - 1:CONV_K, :]
    conv_ref[:, 0:(CONV_K - 2) * CONV_DIM] = cstate_ref[:, CONV_DIM:(CONV_K - 1) * CONV_DIM]
    conv_ref[:, (CONV_K - 2) * CONV_DIM:] = xbc
    xbc_c = _silu(acc)
    xs = xbc_c[:, :D_INNER]
    b_ref[...] = xbc_c[:, D_INNER:D_INNER + SSD_GROUPS * STATE]
    c_ref[...] = xbc_c[:, D_INNER + SSD_GROUPS * STATE:]
    dt = _softplus(_bdot(h, win_ref[:, C_DT:IN_COLS_P]) + dtb_ref[...])
    da_ref[...] = jnp.exp(dt * (-jnp.exp(alog_ref[...])))
    xdt = xs * _bdot(_pack3(dt), e64_ref[...])
    xdtt_ref[...] = xdt.T.astype(BF16)
    xsd_ref[...] = xs * dskip_ref[...]
    zs_ref[...] = _silu(_bdot(h, win_ref[:, C_Z:C_XBC]))
    gs_ref[...] = _sigmoid(_bdot(h, win_ref[:, C_GSSD:C_DT]) + bg_ref[:, D_MODEL:])


def _sample_pre(layer, x, mod, cstate, P, consts):
    n = DEC_BATCH
    ls = lambda *shape: _layer_spec(shape, layer)
    in_specs = [
        _const_spec((n, D_MODEL)), ls(n, 6 * D_MODEL), ls(D_MODEL, IN_COLS_P),
        ls(1, 2 * D_MODEL), ls(1, D_CM), ls(1, D_CM), ls(1, D_CM), ls(1, D_CM),
        ls(CONV_K, CONV_DIM), ls(1, CONV_DIM), ls(n, (CONV_K - 1) * CONV_DIM),
        ls(1, LANES), ls(1, LANES), ls(1, D_INNER), ls(D_CM, D_MODEL), _const_spec((LANES, D_INNER)),
    ]
    outs = [((n, D_CM), F32), ((n, (CONV_K - 1) * CONV_DIM), F32), ((n, LANES), F32),
            ((D_INNER, n), BF16), ((n, SSD_GROUPS * STATE), F32), ((n, SSD_GROUPS * STATE), F32),
            ((n, D_MODEL), F32), ((n, D_INNER), F32), ((n, D_MODEL), F32), ((n, D_INNER), F32)]
    return pl.pallas_call(
        _s_pre_kernel,
        grid=(1,),
        in_specs=in_specs,
        out_specs=[_const_spec(s, single=False) for s, _ in outs],
        out_shape=[jax.ShapeDtypeStruct(s, d) for s, d in outs],
        compiler_params=pltpu.CompilerParams(
            dimension_semantics=("arbitrary",), vmem_limit_bytes=VMEM_LIMIT),
        name="sample_pre",
    )(x, mod, P["w_in"], P["b_gate"], P["ln_v_g"], P["ln_v_b"], P["wsp0"], P["bsp0"],
      P["conv_w"], P["conv_b"], cstate, P["dt_bias3"], P["a_log3"], P["dskip_exp"], P["w_cm_br"],
      consts["e64"])


def _s_ssd_kernel(da_ref, xdtt_ref, b_ref, c_ref, st_ref, *rest):
    sto_ref, y_ref = rest[-2:]
    i = pl.program_id(0)

    @pl.when(i == 0)
    def _():
        y_ref[...] = jnp.zeros_like(y_ref)

    rows = lax.broadcasted_iota(jnp.int32, (DEC_BATCH, STATE), 0)
    rows8 = lax.broadcasted_iota(jnp.int32, (8, STATE), 0)
    gw = HEADS_PER_GROUP * HEAD_DIM
    for q in range(SSD_BB):
        b = i * SSD_BB + q
        b8 = pl.multiple_of((b // 8) * 8, 8)
        for g in range(SSD_GROUPS):
            bg = jnp.where(rows == b, b_ref[:, g * STATE:(g + 1) * STATE], 0.0).astype(BF16)
            xb = _bdot(xdtt_ref[g * gw:(g + 1) * gw, :], bg)
            blocks = []
            for hh in range(HEADS_PER_GROUP):
                hd = g * HEADS_PER_GROUP + hh
                r = hd * HEAD_DIM
                blk = st_ref[q, r:r + HEAD_DIM, :] * da_ref[b, hd] + xb[hh * HEAD_DIM:(hh + 1) * HEAD_DIM, :]
                sto_ref[q, r:r + HEAD_DIM, :] = blk
                blocks.append(blk.astype(BF16))
            newg = jnp.concatenate(blocks, axis=0)
            cg = jnp.where(rows8 == b - b8, c_ref[pl.ds(b8, 8), g * STATE:(g + 1) * STATE], 0.0).astype(BF16)
            yg = lax.dot_general(cg, newg, (((1,), (1,)), ((), ())), preferred_element_type=F32)
            y_ref[pl.ds(b8, 8), g * gw:(g + 1) * gw] += yg


def _sample_ssd(layer, da, xdtt, bm, cm, state, stacked):
    n = DEC_BATCH
    in_specs = [
        pl.BlockSpec((D_INNER, n), lambda i, da: (0, 0)),
        pl.BlockSpec((n, SSD_GROUPS * STATE), lambda i, da: (0, 0)),
        pl.BlockSpec((n, SSD_GROUPS * STATE), lambda i, da: (0, 0)),
        pl.BlockSpec((None, SSD_BB, D_INNER, STATE), lambda i, da: (layer, i, 0, 0)),
    ]
    args = [da, xdtt, bm, cm, state]
    aliases = {}
    if stacked is not None:
        in_specs.append(pl.BlockSpec(memory_space=pl.ANY))
        args.append(stacked)
        aliases = {len(args) - 1: 0}
    grid_spec = pltpu.PrefetchScalarGridSpec(
        num_scalar_prefetch=1,
        grid=(n // SSD_BB,),
        in_specs=in_specs,
        out_specs=[
            pl.BlockSpec((None, SSD_BB, D_INNER, STATE), lambda i, da: (layer, i, 0, 0)),
            pl.BlockSpec((n, D_INNER), lambda i, da: (0, 0)),
        ],
    )
    return pl.pallas_call(
        _s_ssd_kernel,
        grid_spec=grid_spec,
        out_shape=[jax.ShapeDtypeStruct((DEPTH, n, D_INNER, STATE), F32),
                   jax.ShapeDtypeStruct((n, D_INNER), F32)],
        input_output_aliases=aliases,
        compiler_params=pltpu.CompilerParams(
            dimension_semantics=("arbitrary",), vmem_limit_bytes=VMEM_LIMIT),
        name="sample_ssd",
    )(*args)


def _s_post_kernel(x_ref, mod_ref, y_ref, xsd_ref, zs_ref, gs_ref, cmpart_ref, normw_ref,
                   wssd_ref, wo_ref, ln1g_ref, ln1b_ref, wg_ref, wu_ref, wd_ref, ln2g_ref, ln2b_ref, o_ref):
    md = lambda k: mod_ref[:, k * D_MODEL:(k + 1) * D_MODEL]
    yn = _rms_norm((y_ref[...] + xsd_ref[...]) * zs_ref[...], normw_ref[...]).astype(BF16)
    merged = cmpart_ref[...] + gs_ref[...] * _bdot(yn, wssd_ref[...])
    x1 = _post_mixer(x_ref[...], md(2), merged.astype(BF16), wo_ref, ln1g_ref[...], ln1b_ref[...])
    o_ref[...] = _ffn(x1, md(3), md(4), md(5), wg_ref, wu_ref, wd_ref, ln2g_ref[...], ln2b_ref[...])


def _sample_post(layer, x, mod, y, xsd, zs, gs, cmpart, P):
    n = DEC_BATCH
    ls = lambda *shape: _layer_spec(shape, layer)
    cs = _const_spec
    in_specs = [
        cs((n, D_MODEL)), ls(n, 6 * D_MODEL), cs((n, D_INNER)), cs((n, D_INNER)), cs((n, D_INNER)),
        cs((n, D_MODEL)), cs((n, D_MODEL)), ls(1, D_INNER), ls(D_INNER, D_MODEL), ls(D_MODEL, D_MODEL),
        ls(1, D_MODEL), ls(1, D_MODEL), ls(D_MODEL, D_FF), ls(D_MODEL, D_FF), ls(D_FF, D_MODEL),
        ls(1, D_MODEL), ls(1, D_MODEL),
    ]
    return pl.pallas_call(
        _s_post_kernel,
        grid=(1,),
        in_specs=in_specs,
        out_specs=_const_spec((n, D_MODEL), single=False),
        out_shape=jax.ShapeDtypeStruct((n, D_MODEL), F32),
        compiler_params=pltpu.CompilerParams(
            dimension_semantics=("arbitrary",), vmem_limit_bytes=VMEM_LIMIT),
        name="sample_post",
    )(x, mod, y, xsd, zs, gs, cmpart, P["ssd_norm_w"], P["w_ssd_br"], P["w_o"], P["ln1_g"], P["ln1_b"],
      P["w_ffn_gate"], P["w_ffn_up"], P["w_ffn_down"], P["ln2_g"], P["ln2_b"])


def _constants():
    tri = np.tril(np.ones((CHUNK, CHUNK), np.float32))
    e64 = np.zeros((LANES, D_INNER), np.float32)
    for r in range(DT_COPIES * HEADS):
        hd = r % HEADS
        e64[r, hd * HEAD_DIM:(hd + 1) * HEAD_DIM] = 1.0
    return {"tri": jnp.asarray(tri, BF16), "e64": jnp.asarray(e64, BF16)}


def _params(p):
    rows = lambda a: a[:, None, :]
    pad = jnp.zeros((DEPTH, LANES - DT_COPIES * HEADS), F32)
    return {
        "w_in": _win_prep(p["w_in"]),
        "b_gate": rows(p["b_gate"]),
        "ln_v_g": rows(p["ln_v_g"]), "ln_v_b": rows(p["ln_v_b"]),
        "w_spatial": p["w_spatial"],
        "bsp_full": jnp.repeat(jnp.swapaxes(p["b_spatial"], 1, 2), LANES, axis=2),
        "wsp0": rows(jnp.repeat(p["w_spatial"][:, :, 0, 0], LANES, axis=1)),
        "bsp0": rows(jnp.repeat(p["b_spatial"][:, :, 0], LANES, axis=1)),
        "conv_w": p["conv_w"], "conv_b": rows(p["conv_b"]),
        "dt_bias3": rows(jnp.concatenate([p["dt_bias"]] * DT_COPIES + [pad], axis=1)),
        "a_log3": rows(jnp.concatenate([p["a_log"]] * DT_COPIES + [pad], axis=1)),
        "dskip_exp": rows(jnp.repeat(p["d_skip"], HEAD_DIM, axis=1)),
        "ssd_norm_w": rows(p["ssd_norm_w"]),
        "w_cm_br": _to_bf16(p["w_cm_br"], 512), "w_ssd_br": _to_bf16(p["w_ssd_br"], 512),
        "w_o": _to_bf16(p["w_o"], 512),
        "ln1_g": rows(p["ln1_g"]), "ln1_b": rows(p["ln1_b"]),
        "w_ffn_gate": _to_bf16(p["w_ffn_gate"], 512), "w_ffn_up": _to_bf16(p["w_ffn_up"], 512),
        "w_ffn_down": _to_bf16(p["w_ffn_down"], D_FF // 4),
        "ln2_g": rows(p["ln2_g"]), "ln2_b": rows(p["ln2_b"]),
    }


def kernel(x_prompt, x_sample, state_ssd, state_conv, c_prompt, c_sample, w_ada, b_ada, w_in, b_gate,
           ln_v_g, ln_v_b, w_spatial, b_spatial, conv_w, conv_b, dt_bias, a_log, d_skip, ssd_norm_w,
           w_cm_br, w_ssd_br, w_o, ln1_g, ln1_b, w_ffn_gate, w_ffn_up, w_ffn_down, ln2_g, ln2_b):
    P = _params(dict(w_in=w_in, b_gate=b_gate, ln_v_g=ln_v_g, ln_v_b=ln_v_b, w_spatial=w_spatial,
                     b_spatial=b_spatial, conv_w=conv_w, conv_b=conv_b, dt_bias=dt_bias, a_log=a_log,
                     d_skip=d_skip, ssd_norm_w=ssd_norm_w, w_cm_br=w_cm_br, w_ssd_br=w_ssd_br, w_o=w_o,
                     ln1_g=ln1_g, ln1_b=ln1_b, w_ffn_gate=w_ffn_gate, w_ffn_up=w_ffn_up,
                     w_ffn_down=w_ffn_down, ln2_g=ln2_g, ln2_b=ln2_b))
    consts = _constants()
    mod, mod_p = _adaln_mod(jnp.concatenate([c_sample, c_prompt], axis=0), w_ada, b_ada)
    mod4 = mod_p.reshape(DEPTH, BATCH, 6, D_MODEL)
    xp = x_prompt
    xs = x_sample.reshape(DEC_BATCH, D_MODEL)
    ssd_p, conv_p, conv_s, v_s = [], [], [], []
    state_in = state_ssd.reshape(DEPTH, DEC_BATCH, D_INNER, STATE)
    cstate = state_conv.reshape(DEPTH, DEC_BATCH, (CONV_K - 1) * CONV_DIM)
    ssd_s = None
    for i in range(DEPTH):
        x1, sp, cp = _prompt_mixer(i, xp, mod4, P, consts)
        xp = _prompt_ffn(i, x1, mod4, P)
        ssd_p.append(sp.reshape(BATCH, HEADS, HEAD_DIM, STATE))
        conv_p.append(cp)
        v, cs, da, xdtt, bm, cm, cmpart, zs, gs, xsd = _sample_pre(i, xs, mod, cstate, P, consts)
        ssd_s, y = _sample_ssd(i, da[:, :HEADS], xdtt, bm, cm, state_in, ssd_s)
        xs = _sample_post(i, xs, mod, y, xsd, zs, gs, cmpart, P)
        conv_s.append(cs.reshape(DEC_BATCH, CONV_K - 1, CONV_DIM))
        v_s.append(v.reshape(DEC_BATCH, 1, D_CM))
    return (xp, xs.reshape(DEC_BATCH, 1, D_MODEL), jnp.stack(ssd_p), jnp.stack(conv_p),
            ssd_s.reshape(DEPTH, DEC_BATCH, HEADS, HEAD_DIM, STATE), jnp.stack(conv_s), jnp.stack(v_s))
```

```python
import functools
import math

import numpy as np
import jax
import jax.numpy as jnp
from jax import lax
from jax.experimental import pallas as pl
from jax.experimental.pallas import tpu as pltpu

D_MODEL = 1024
BATCH = 8
SEQ = 2048
DEPTH = 2
DEC_BATCH = 128
CHUNK = 128
D_CM = D_MODEL
CM_GROUPS = 8
D_INNER = 2048
HEAD_DIM = 64
HEADS = 32
SSD_GROUPS = 4
HEADS_PER_GROUP = HEADS // SSD_GROUPS
STATE = 128
CONV_K = 4
CONV_DIM = D_INNER + 2 * SSD_GROUPS * STATE
D_FF = 2816
ALPHA = (2 * DEPTH) ** 0.25
LN_EPS = 1e-5
REF_DT = 2 * D_CM + D_INNER + CONV_DIM
REF_GATES = REF_DT + HEADS
IN_COLS = REF_GATES + 2 * D_MODEL

LANES = 128
DT_COPIES = 3
C_U = 0
C_V = C_U + D_CM
C_Z = C_V + D_CM
C_XBC = C_Z + D_INNER
C_END = C_XBC + CONV_DIM

TILE_L = 256
TILE_F = 512
SSD_BB = 8
VMEM_LIMIT = 56 * 1024 * 1024

F32 = jnp.float32
BF16 = jnp.bfloat16


def _bdot(a, b):
    return jnp.dot(a, b, preferred_element_type=F32)


def _gelu(x):
    return 0.5 * x * (1.0 + lax.erf(x * np.float32(math.sqrt(0.5))))


def _sigmoid(x):
    return 1.0 / (1.0 + jnp.exp(-x))


def _silu(x):
    return x * _sigmoid(x)


def _softplus(x):
    return jnp.maximum(x, 0.0) + jnp.log1p(jnp.exp(-jnp.abs(x)))


def _layer_norm(x, g, b):
    mu = jnp.mean(x, axis=-1, keepdims=True)
    xc = x - mu
    var = jnp.mean(xc * xc, axis=-1, keepdims=True)
    return xc * lax.rsqrt(var + LN_EPS) * g + b


def _rms_norm(x, g):
    return x * lax.rsqrt(jnp.mean(x * x, axis=-1, keepdims=True) + LN_EPS) * g


def _pack3(x):
    hi = x.astype(BF16).astype(F32)
    r1 = x - hi
    mid = r1.astype(BF16).astype(F32)
    r2 = r1 - mid
    lane = lax.broadcasted_iota(jnp.int32, x.shape, 1)
    return jnp.where(lane < HEADS, x, jnp.where(lane < 2 * HEADS, r1, r2)).astype(BF16)


def _const_spec(shape, single=True):
    nd = len(shape)
    kw = {"pipeline_mode": pl.Buffered(1)} if single else {}
    return pl.BlockSpec(shape, lambda *_: (0,) * nd, **kw)


def _layer_spec(shape, layer):
    nd = len(shape)
    return pl.BlockSpec((None,) + tuple(shape), lambda *_: (layer,) + (0,) * nd,
                        pipeline_mode=pl.Buffered(1))


def _cast_kernel(w_ref, o_ref):
    o_ref[...] = w_ref[...].astype(BF16)


def _to_bf16(w, rows):
    depth, r, c = w.shape
    return pl.pallas_call(
        _cast_kernel,
        grid=(depth, r // rows),
        in_specs=[pl.BlockSpec((None, rows, c), lambda l, i: (l, i, 0))],
        out_specs=pl.BlockSpec((None, rows, c), lambda l, i: (l, i, 0)),
        out_shape=jax.ShapeDtypeStruct((depth, r, c), BF16),
        compiler_params=pltpu.CompilerParams(
            dimension_semantics=("parallel", "parallel"), vmem_limit_bytes=VMEM_LIMIT),
        name="cast_bf16",
    )(w)


def _tcast_kernel(w_ref, o_ref):
    o_ref[...] = w_ref[...].T.astype(BF16)


def _tcast_dt_kernel(w_ref, o_ref):
    t = w_ref[...].T
    lane = lax.broadcasted_iota(jnp.int32, t.shape, 1)
    o_ref[...] = jnp.where(lane < HEADS, t, 0.0).astype(BF16)


def _win_prep(w_in, rows=512):
    wt = jnp.swapaxes(w_in, 1, 2)
    params = pltpu.CompilerParams(dimension_semantics=("parallel", "parallel"), vmem_limit_bytes=VMEM_LIMIT)

    def piece(kern, n_out, in_spec, nblk, blk):
        return pl.pallas_call(
            kern, grid=(DEPTH, nblk), in_specs=[in_spec],
            out_specs=pl.BlockSpec((None, D_MODEL, blk), lambda l, i: (l, 0, i)),
            out_shape=jax.ShapeDtypeStruct((DEPTH, D_MODEL, n_out), BF16),
            compiler_params=params, name="w_in_prep")(wt)

    main = piece(_tcast_kernel, C_END, pl.BlockSpec((None, rows, D_MODEL), lambda l, i: (l, i, 0)),
                 C_END // rows, rows)
    gates = piece(lambda w_ref, o_ref: _tcast_kernel(w_ref.at[0], o_ref), 2 * D_MODEL,
                  pl.BlockSpec((pl.Element(1), pl.Element(rows), pl.Element(D_MODEL)),
                               lambda l, i: (l, pl.multiple_of(REF_GATES + i * rows, 8), 0)),
                  2 * D_MODEL // rows, rows)
    dt = piece(_tcast_dt_kernel, LANES,
               pl.BlockSpec((None, LANES, D_MODEL), lambda l, i: (l, REF_DT // LANES, 0)), 1, LANES)
    return {"w_in": main, "w_gates": gates, "w_dt": dt}


def _dt_cols(h, wdt_ref):
    d = _bdot(h, wdt_ref[...])
    out = d
    for k in range(1, DT_COPIES):
        out = out + pltpu.roll(d, k * HEADS, 1)
    return out


def _mod_kernel(c_ref, w_ref, b_ref, os_ref, op_ref):
    s = _silu(c_ref[...])
    res = jnp.dot(s, w_ref[...], preferred_element_type=F32,
                  precision=lax.Precision.HIGHEST) + b_ref[...]
    os_ref[...] = res[:DEC_BATCH]
    op_ref[...] = res[DEC_BATCH:]


def _adaln_mod(c_all, w_ada, b_ada):
    rows = c_all.shape[0]
    nblk = 6
    return pl.pallas_call(
        _mod_kernel,
        grid=(DEPTH, nblk),
        in_specs=[
            pl.BlockSpec((rows, D_MODEL), lambda l, k: (0, 0)),
            pl.BlockSpec((None, D_MODEL, D_MODEL), lambda l, k: (l, 0, k)),
            pl.BlockSpec((None, 1, D_MODEL), lambda l, k: (l, 0, k)),
        ],
        out_specs=[pl.BlockSpec((None, DEC_BATCH, D_MODEL), lambda l, k: (l, 0, k)),
                   pl.BlockSpec((None, BATCH, D_MODEL), lambda l, k: (l, 0, k))],
        out_shape=[jax.ShapeDtypeStruct((DEPTH, DEC_BATCH, nblk * D_MODEL), F32),
                   jax.ShapeDtypeStruct((DEPTH, BATCH, nblk * D_MODEL), F32)],
        compiler_params=pltpu.CompilerParams(
            dimension_semantics=("parallel", "parallel"), vmem_limit_bytes=VMEM_LIMIT),
        name="adaln_mod",
    )(c_all, w_ada, b_ada.reshape(DEPTH, 1, nblk * D_MODEL))


def _cm_branch(u_v, lnvg, lnvb):
    u = _gelu(u_v[:, :D_CM])
    v = _layer_norm(_gelu(u_v[:, D_CM:]), lnvg, lnvb)
    return u, v


def _post_mixer(x, g1, merged_bf16, wo_ref, ln1g, ln1b):
    o = _bdot(merged_bf16, wo_ref[...])
    return _layer_norm(ALPHA * x + g1 * o, ln1g, ln1b)


def _ffn(x1, sh2, sc2, g2, wg_ref, wu_ref, wd_ref, ln2g, ln2b):
    h2 = (x1 * (1.0 + sc2) + sh2).astype(BF16)
    gate = _bdot(h2, wg_ref[...])
    up = _bdot(h2, wu_ref[...])
    act = (_silu(gate) * up).astype(BF16)
    ffn = _bdot(act, wd_ref[...])
    return _layer_norm(ALPHA * x1 + g2 * ffn, ln2g, ln2b)


def _ssd_chunk(xs_c, b_c, c_c, dtseg, dtb, alog, tri, e64, state_ref, y_ref, r0):
    row = lax.broadcasted_iota(jnp.int32, (CHUNK, LANES), 0)
    lane = lax.broadcasted_iota(jnp.int32, (CHUNK, LANES), 1)
    tril = row >= lane
    low_half = lane < HEAD_DIM

    dt = _softplus(dtseg + dtb)
    a = -jnp.exp(alog)
    da = dt * a
    hi = da.astype(BF16)
    r1 = da - hi.astype(F32)
    mid = r1.astype(BF16)
    lo = (r1 - mid.astype(F32)).astype(BF16)
    acum = _bdot(tri, hi) + _bdot(tri, mid) + _bdot(tri, lo)
    shift_t = acum.T - jnp.log(dt.T)
    alast = acum[CHUNK - 1:CHUNK, :]
    w = dt * jnp.exp(alast - acum)
    wexp = _bdot(_pack3(w), e64)
    eexp = _bdot(_pack3(jnp.exp(acum)), e64)
    cd = jnp.broadcast_to(jnp.exp(alast), (8, LANES))
    cdexp = _bdot(_pack3(cd), e64)[0:1, :]
    xs_b = xs_c.astype(BF16)
    zero_b = jnp.zeros((CHUNK, LANES), BF16)

    gw = HEADS_PER_GROUP * HEAD_DIM
    for g in range(SSD_GROUPS):
        gc = g * gw
        bg = b_c[:, g * STATE:(g + 1) * STATE]
        cgb = c_c[:, g * STATE:(g + 1) * STATE].astype(BF16)
        cb = lax.dot_general(cgb, bg.astype(BF16), (((1,), (1,)), ((), ())),
                             preferred_element_type=F32)
        y_off = eexp[:, gc:gc + gw] * _bdot(cgb, state_ref[:, gc:gc + gw].astype(BF16))
        for pr in range(HEADS_PER_GROUP // 2):
            h0 = g * HEADS_PER_GROUP + 2 * pr
            pieces = []
            for h in (h0, h0 + 1):
                seg = jnp.broadcast_to(acum[:, h:h + 1], (CHUNK, LANES)) - shift_t[h:h + 1, :]
                pieces.append((cb * jnp.exp(jnp.where(tril, seg, -jnp.inf))).astype(BF16))
            lhs = jnp.concatenate(pieces, axis=1)
            pc = h0 * HEAD_DIM
            xs_pair = xs_b[:, pc:pc + LANES]
            rhs = jnp.concatenate([jnp.where(low_half, xs_pair, zero_b),
                                   jnp.where(low_half, zero_b, xs_pair)], axis=0)
            y_ref[r0:r0 + CHUNK, pc:pc + LANES] = (
                _bdot(lhs, rhs) + y_off[:, pc - gc:pc - gc + LANES])
        xw = (xs_c[:, gc:gc + gw] * wexp[:, gc:gc + gw]).astype(BF16)
        upd = _bdot(bg.T.astype(BF16), xw)
        state_ref[:, gc:gc + gw] = state_ref[:, gc:gc + gw] * cdexp[:, gc:gc + gw] + upd


def _p1_kernel(x_ref, mod_ref, win_ref, wgate_ref, wdt_ref, bg_ref, lnvg_ref, lnvb_ref, wsp_ref, bsp_ref,
               convw_ref, convb_ref, dtb_ref, alog_ref, dskip_ref, normw_ref,
               wcm_ref, wssd_ref, wo_ref, ln1g_ref, ln1b_ref, tri_ref, e64_ref,
               x1_ref, ssd_ref, conv_ref,
               state_ref, cbuf_ref, y_ref, *, tl):
    j = pl.program_id(1)
    last = pl.num_programs(1) - 1
    nc = tl // CHUNK

    @pl.when(j == 0)
    def _():
        state_ref[...] = jnp.zeros_like(state_ref)
        cbuf_ref[0:8, :] = jnp.zeros((8, CONV_DIM), F32)

    x = x_ref[...]
    sh1 = mod_ref[0:1, :]
    sc1 = mod_ref[1:2, :]
    g1 = mod_ref[2:3, :]
    h = (x * (1.0 + sc1) + sh1).astype(BF16)

    cbuf_ref[8:8 + tl, :] = _bdot(h, win_ref[:, C_XBC:C_END])
    dtseg = _dt_cols(h, wdt_ref)
    u_v = _bdot(h, win_ref[:, C_U:C_Z])

    acc = convb_ref[...]
    for k in range(CONV_K):
        acc = acc + cbuf_ref[5 + k:5 + k + tl, :] * convw_ref[k:k + 1, :]
    tail = cbuf_ref[tl + 5:tl + 8, :]
    cbuf_ref[5:8, :] = tail
    xbc_c = _silu(acc)
    xs = xbc_c[:, :D_INNER]

    z = _bdot(h, win_ref[:, C_Z:C_XBC])
    gcm_logit = _bdot(h, wgate_ref[:, :D_MODEL])
    gssd_logit = _bdot(h, wgate_ref[:, D_MODEL:])

    u, v = _cm_branch(u_v, lnvg_ref[...], lnvb_ref[...])
    vb = v.astype(BF16)
    row = lax.broadcasted_iota(jnp.int32, (CHUNK, CHUNK), 0)
    col = lax.broadcasted_iota(jnp.int32, (CHUNK, CHUNK), 1)
    wmix = [jnp.where(row >= col, wsp_ref[g], 0.0).astype(BF16) for g in range(CM_GROUPS)]

    for c in range(nc):
        r0 = c * CHUNK
        _ssd_chunk(xs[r0:r0 + CHUNK], xbc_c[r0:r0 + CHUNK, D_INNER:D_INNER + SSD_GROUPS * STATE],
                   xbc_c[r0:r0 + CHUNK, D_INNER + SSD_GROUPS * STATE:], dtseg[r0:r0 + CHUNK],
                   dtb_ref[...], alog_ref[...], tri_ref[...], e64_ref[...],
                   state_ref, y_ref, r0)

    rows = []
    for c in range(nc):
        cols = []
        for g in range(CM_GROUPS):
            cols.append(_bdot(wmix[g], vb[c * CHUNK:(c + 1) * CHUNK, g * LANES:(g + 1) * LANES]))
        rows.append(jnp.concatenate(cols, axis=1) + bsp_ref[...])
    mix = jnp.concatenate(rows, axis=0) if nc > 1 else rows[0]
    out_cm = (u * mix).astype(BF16)
    merged = _sigmoid(gcm_logit + bg_ref[:, :D_MODEL]) * _bdot(out_cm, wcm_ref[...])

    yz = (y_ref[...] + xs * dskip_ref[...]) * _silu(z)
    yn = _rms_norm(yz, normw_ref[...]).astype(BF16)
    merged = merged + _sigmoid(gssd_logit + bg_ref[:, D_MODEL:]) * _bdot(yn, wssd_ref[...])
    x1_ref[...] = _post_mixer(x, g1, merged.astype(BF16), wo_ref, ln1g_ref[...], ln1b_ref[...])

    @pl.when(j == last)
    def _():
        conv_ref[...] = cbuf_ref[5:8, :]
        ssd_ref[...] = state_ref[...].T


def _mod_prompt_spec(layer):
    return pl.BlockSpec((None, None, 6, D_MODEL), lambda b, j: (layer, b, 0, 0))


def _prompt_mixer(layer, x, mod4, P, consts):
    tl = TILE_L
    nj = SEQ // tl
    ls = lambda *shape: _layer_spec(shape, layer)
    in_specs = [
        pl.BlockSpec((None, tl, D_MODEL), lambda b, j: (b, j, 0)),
        _mod_prompt_spec(layer),
        ls(D_MODEL, C_END), ls(D_MODEL, 2 * D_MODEL), ls(D_MODEL, LANES),
        ls(1, 2 * D_MODEL), ls(1, D_CM), ls(1, D_CM),
        ls(CM_GROUPS, CHUNK, CHUNK), ls(CHUNK, D_CM),
        ls(CONV_K, CONV_DIM), ls(1, CONV_DIM), ls(1, LANES), ls(1, LANES),
        ls(1, D_INNER), ls(1, D_INNER),
        ls(D_CM, D_MODEL), ls(D_INNER, D_MODEL), ls(D_MODEL, D_MODEL),
        ls(1, D_MODEL), ls(1, D_MODEL),
        _const_spec((CHUNK, CHUNK)), _const_spec((LANES, D_INNER)),
    ]
    out_specs = [
        pl.BlockSpec((None, tl, D_MODEL), lambda b, j: (b, j, 0)),
        pl.BlockSpec((None, D_INNER, STATE), lambda b, j: (b, 0, 0)),
        pl.BlockSpec((None, CONV_K - 1, CONV_DIM), lambda b, j: (b, 0, 0)),
    ]
    out_shape = [
        jax.ShapeDtypeStruct((BATCH, SEQ, D_MODEL), F32),
        jax.ShapeDtypeStruct((BATCH, D_INNER, STATE), F32),
        jax.ShapeDtypeStruct((BATCH, CONV_K - 1, CONV_DIM), F32),
    ]
    return pl.pallas_call(
        functools.partial(_p1_kernel, tl=tl),
        grid=(BATCH, nj),
        in_specs=in_specs, out_specs=out_specs, out_shape=out_shape,
        scratch_shapes=[pltpu.VMEM((STATE, D_INNER), F32),
                        pltpu.VMEM((tl + 8, CONV_DIM), F32),
                        pltpu.VMEM((tl, D_INNER), F32)],
        compiler_params=pltpu.CompilerParams(
            dimension_semantics=("parallel", "arbitrary"), vmem_limit_bytes=VMEM_LIMIT),
        name="prompt_mixer",
    )(x, mod4, P["w_in"], P["w_gates"], P["w_dt"], P["b_gate"], P["ln_v_g"], P["ln_v_b"], P["w_spatial"], P["bsp_full"],
      P["conv_w"], P["conv_b"], P["dt_bias3"], P["a_log3"], P["dskip_exp"], P["ssd_norm_w"],
      P["w_cm_br"], P["w_ssd_br"], P["w_o"], P["ln1_g"], P["ln1_b"],
      consts["tri"], consts["e64"])


def _p2_kernel(x_ref, mod_ref, wg_ref, wu_ref, wd_ref, ln2g_ref, ln2b_ref, o_ref):
    o_ref[...] = _ffn(x_ref[...], mod_ref[3:4, :], mod_ref[4:5, :], mod_ref[5:6, :],
                      wg_ref, wu_ref, wd_ref, ln2g_ref[...], ln2b_ref[...])


def _prompt_ffn(layer, x1, mod4, P):
    tf = TILE_F
    ls = lambda *shape: _layer_spec(shape, layer)
    return pl.pallas_call(
        _p2_kernel,
        grid=(BATCH, SEQ // tf),
        in_specs=[
            pl.BlockSpec((None, tf, D_MODEL), lambda b, j: (b, j, 0)),
            _mod_prompt_spec(layer),
            ls(D_MODEL, D_FF), ls(D_MODEL, D_FF), ls(D_FF, D_MODEL),
            ls(1, D_MODEL), ls(1, D_MODEL),
        ],
        out_specs=pl.BlockSpec((None, tf, D_MODEL), lambda b, j: (b, j, 0)),
        out_shape=jax.ShapeDtypeStruct((BATCH, SEQ, D_MODEL), F32),
        compiler_params=pltpu.CompilerParams(
            dimension_semantics=("parallel", "parallel"), vmem_limit_bytes=VMEM_LIMIT),
        name="prompt_ffn",
    )(x1, mod4, P["w_ffn_gate"], P["w_ffn_up"], P["w_ffn_down"], P["ln2_g"], P["ln2_b"])


def _s_pre_kernel(x_ref, mod_ref, win_ref, wgate_ref, wdt_ref, bg_ref, lnvg_ref, lnvb_ref, wsp0_ref, bsp0_ref,
                  convw_ref, convb_ref, cstate_ref, dtb_ref, alog_ref, dskip_ref, wcm_ref, e64_ref,
                  v_ref, conv_ref, da_ref, xdtt_ref, b_ref, c_ref, cmpart_ref, zs_ref, gs_ref, xsd_ref):
    x = x_ref[...]
    sh1 = mod_ref[:, 0:D_MODEL]
    sc1 = mod_ref[:, D_MODEL:2 * D_MODEL]
    h = (x * (1.0 + sc1) + sh1).astype(BF16)

    u, v = _cm_branch(_bdot(h, win_ref[:, C_U:C_Z]), lnvg_ref[...], lnvb_ref[...])
    v_ref[...] = v
    out_cm = (u * (wsp0_ref[...] * v + bsp0_ref[...])).astype(BF16)
    gate_cm = _sigmoid(_bdot(h, wgate_ref[:, :D_MODEL]) + bg_ref[:, :D_MODEL])
    cmpart_ref[...] = gate_cm * _bdot(out_cm, wcm_ref[...])

    xbc = _bdot(h, win_ref[:, C_XBC:C_END])
    acc = convb_ref[...]
    for k in range(CONV_K - 1):
        acc = acc + cstate_ref[:, k * CONV_DIM:(k + 1) * CONV_DIM] * convw_ref[k:k + 1, :]
    acc = acc + xbc * convw_ref[CONV_K - 1:CONV_K, :]
    conv_ref[:, 0:(CONV_K - 2) * CONV_DIM] = cstate_ref[:, CONV_DIM:(CONV_K - 1) * CONV_DIM]
    conv_ref[:, (CONV_K - 2) * CONV_DIM:] = xbc
    xbc_c = _silu(acc)
    xs = xbc_c[:, :D_INNER]
    b_ref[...] = xbc_c[:, D_INNER:D_INNER + SSD_GROUPS * STATE]
    c_ref[...] = xbc_c[:, D_INNER + SSD_GROUPS * STATE:]
    dt = _softplus(_dt_cols(h, wdt_ref) + dtb_ref[...])
    da_ref[...] = jnp.exp(dt * (-jnp.exp(alog_ref[...])))
    xdt = xs * _bdot(_pack3(dt), e64_ref[...])
    xdtt_ref[...] = xdt.T.astype(BF16)
    xsd_ref[...] = xs * dskip_ref[...]
    zs_ref[...] = _silu(_bdot(h, win_ref[:, C_Z:C_XBC]))
    gs_ref[...] = _sigmoid(_bdot(h, wgate_ref[:, D_MODEL:]) + bg_ref[:, D_MODEL:])


def _sample_pre(layer, x, mod, cstate, P, consts):
    n = DEC_BATCH
    ls = lambda *shape: _layer_spec(shape, layer)
    in_specs = [
        _const_spec((n, D_MODEL)), ls(n, 6 * D_MODEL),
        ls(D_MODEL, C_END), ls(D_MODEL, 2 * D_MODEL), ls(D_MODEL, LANES),
        ls(1, 2 * D_MODEL), ls(1, D_CM), ls(1, D_CM), ls(1, D_CM), ls(1, D_CM),
        ls(CONV_K, CONV_DIM), ls(1, CONV_DIM), ls(n, (CONV_K - 1) * CONV_DIM),
        ls(1, LANES), ls(1, LANES), ls(1, D_INNER), ls(D_CM, D_MODEL), _const_spec((LANES, D_INNER)),
    ]
    outs = [((n, D_CM), F32), ((n, (CONV_K - 1) * CONV_DIM), F32), ((n, LANES), F32),
            ((D_INNER, n), BF16), ((n, SSD_GROUPS * STATE), F32), ((n, SSD_GROUPS * STATE), F32),
            ((n, D_MODEL), F32), ((n, D_INNER), F32), ((n, D_MODEL), F32), ((n, D_INNER), F32)]
    return pl.pallas_call(
        _s_pre_kernel,
        grid=(1,),
        in_specs=in_specs,
        out_specs=[_const_spec(s, single=False) for s, _ in outs],
        out_shape=[jax.ShapeDtypeStruct(s, d) for s, d in outs],
        compiler_params=pltpu.CompilerParams(
            dimension_semantics=("arbitrary",), vmem_limit_bytes=VMEM_LIMIT),
        name="sample_pre",
    )(x, mod, P["w_in"], P["w_gates"], P["w_dt"], P["b_gate"], P["ln_v_g"], P["ln_v_b"], P["wsp0"], P["bsp0"],
      P["conv_w"], P["conv_b"], cstate, P["dt_bias3"], P["a_log3"], P["dskip_exp"], P["w_cm_br"],
      consts["e64"])


def _s_ssd_kernel(da_ref, xdtt_ref, b_ref, c_ref, st_ref, *rest):
    sto_ref, y_ref = rest[-2:]
    i = pl.program_id(0)

    @pl.when(i == 0)
    def _():
        y_ref[...] = jnp.zeros_like(y_ref)

    rows = lax.broadcasted_iota(jnp.int32, (DEC_BATCH, STATE), 0)
    rows8 = lax.broadcasted_iota(jnp.int32, (8, STATE), 0)
    gw = HEADS_PER_GROUP * HEAD_DIM
    for q in range(SSD_BB):
        b = i * SSD_BB + q
        b8 = pl.multiple_of((b // 8) * 8, 8)
        for g in range(SSD_GROUPS):
            bg = jnp.where(rows == b, b_ref[:, g * STATE:(g + 1) * STATE], 0.0).astype(BF16)
            xb = _bdot(xdtt_ref[g * gw:(g + 1) * gw, :], bg)
            blocks = []
            for hh in range(HEADS_PER_GROUP):
                hd = g * HEADS_PER_GROUP + hh
                r = hd * HEAD_DIM
                blk = st_ref[q, r:r + HEAD_DIM, :] * da_ref[b, hd] + xb[hh * HEAD_DIM:(hh + 1) * HEAD_DIM, :]
                sto_ref[q, r:r + HEAD_DIM, :] = blk
                blocks.append(blk.astype(BF16))
            newg = jnp.concatenate(blocks, axis=0)
            cg = jnp.where(rows8 == b - b8, c_ref[pl.ds(b8, 8), g * STATE:(g + 1) * STATE], 0.0).astype(BF16)
            yg = lax.dot_general(cg, newg, (((1,), (1,)), ((), ())), preferred_element_type=F32)
            y_ref[pl.ds(b8, 8), g * gw:(g + 1) * gw] += yg


def _sample_ssd(layer, da, xdtt, bm, cm, state, stacked):
    n = DEC_BATCH
    in_specs = [
        pl.BlockSpec((D_INNER, n), lambda i, da: (0, 0)),
        pl.BlockSpec((n, SSD_GROUPS * STATE), lambda i, da: (0, 0)),
        pl.BlockSpec((n, SSD_GROUPS * STATE), lambda i, da: (0, 0)),
        pl.BlockSpec((None, SSD_BB, D_INNER, STATE), lambda i, da: (layer, i, 0, 0)),
    ]
    args = [da, xdtt, bm, cm, state]
    aliases = {}
    if stacked is not None:
        in_specs.append(pl.BlockSpec(memory_space=pl.ANY))
        args.append(stacked)
        aliases = {len(args) - 1: 0}
    grid_spec = pltpu.PrefetchScalarGridSpec(
        num_scalar_prefetch=1,
        grid=(n // SSD_BB,),
        in_specs=in_specs,
        out_specs=[
            pl.BlockSpec((None, SSD_BB, D_INNER, STATE), lambda i, da: (layer, i, 0, 0)),
            pl.BlockSpec((n, D_INNER), lambda i, da: (0, 0)),
        ],
    )
    return pl.pallas_call(
        _s_ssd_kernel,
        grid_spec=grid_spec,
        out_shape=[jax.ShapeDtypeStruct((DEPTH, n, D_INNER, STATE), F32),
                   jax.ShapeDtypeStruct((n, D_INNER), F32)],
        input_output_aliases=aliases,
        compiler_params=pltpu.CompilerParams(
            dimension_semantics=("arbitrary",), vmem_limit_bytes=VMEM_LIMIT),
        name="sample_ssd",
    )(*args)


def _s_post_kernel(x_ref, mod_ref, y_ref, xsd_ref, zs_ref, gs_ref, cmpart_ref, normw_ref,
                   wssd_ref, wo_ref, ln1g_ref, ln1b_ref, wg_ref, wu_ref, wd_ref, ln2g_ref, ln2b_ref, o_ref):
    md = lambda k: mod_ref[:, k * D_MODEL:(k + 1) * D_MODEL]
    yn = _rms_norm((y_ref[...] + xsd_ref[...]) * zs_ref[...], normw_ref[...]).astype(BF16)
    merged = cmpart_ref[...] + gs_ref[...] * _bdot(yn, wssd_ref[...])
    x1 = _post_mixer(x_ref[...], md(2), merged.astype(BF16), wo_ref, ln1g_ref[...], ln1b_ref[...])
    o_ref[...] = _ffn(x1, md(3), md(4), md(5), wg_ref, wu_ref, wd_ref, ln2g_ref[...], ln2b_ref[...])


def _sample_post(layer, x, mod, y, xsd, zs, gs, cmpart, P):
    n = DEC_BATCH
    ls = lambda *shape: _layer_spec(shape, layer)
    cs = _const_spec
    in_specs = [
        cs((n, D_MODEL)), ls(n, 6 * D_MODEL), cs((n, D_INNER)), cs((n, D_INNER)), cs((n, D_INNER)),
        cs((n, D_MODEL)), cs((n, D_MODEL)), ls(1, D_INNER), ls(D_INNER, D_MODEL), ls(D_MODEL, D_MODEL),
        ls(1, D_MODEL), ls(1, D_MODEL), ls(D_MODEL, D_FF), ls(D_MODEL, D_FF), ls(D_FF, D_MODEL),
        ls(1, D_MODEL), ls(1, D_MODEL),
    ]
    return pl.pallas_call(
        _s_post_kernel,
        grid=(1,),
        in_specs=in_specs,
        out_specs=_const_spec((n, D_MODEL), single=False),
        out_shape=jax.ShapeDtypeStruct((n, D_MODEL), F32),
        compiler_params=pltpu.CompilerParams(
            dimension_semantics=("arbitrary",), vmem_limit_bytes=VMEM_LIMIT),
        name="sample_post",
    )(x, mod, y, xsd, zs, gs, cmpart, P["ssd_norm_w"], P["w_ssd_br"], P["w_o"], P["ln1_g"], P["ln1_b"],
      P["w_ffn_gate"], P["w_ffn_up"], P["w_ffn_down"], P["ln2_g"], P["ln2_b"])


def _constants():
    tri = np.tril(np.ones((CHUNK, CHUNK), np.float32))
    e64 = np.zeros((LANES, D_INNER), np.float32)
    for r in range(DT_COPIES * HEADS):
        hd = r % HEADS
        e64[r, hd * HEAD_DIM:(hd + 1) * HEAD_DIM] = 1.0
    return {"tri": jnp.asarray(tri, BF16), "e64": jnp.asarray(e64, BF16)}


def _params(p):
    rows = lambda a: a[:, None, :]
    pad = jnp.zeros((DEPTH, LANES - DT_COPIES * HEADS), F32)
    return {
        **_win_prep(p["w_in"]),
        "b_gate": rows(p["b_gate"]),
        "ln_v_g": rows(p["ln_v_g"]), "ln_v_b": rows(p["ln_v_b"]),
        "w_spatial": p["w_spatial"],
        "bsp_full": jnp.repeat(jnp.swapaxes(p["b_spatial"], 1, 2), LANES, axis=2),
        "wsp0": rows(jnp.repeat(p["w_spatial"][:, :, 0, 0], LANES, axis=1)),
        "bsp0": rows(jnp.repeat(p["b_spatial"][:, :, 0], LANES, axis=1)),
        "conv_w": p["conv_w"], "conv_b": rows(p["conv_b"]),
        "dt_bias3": rows(jnp.concatenate([p["dt_bias"]] * DT_COPIES + [pad], axis=1)),
        "a_log3": rows(jnp.concatenate([p["a_log"]] * DT_COPIES + [pad], axis=1)),
        "dskip_exp": rows(jnp.repeat(p["d_skip"], HEAD_DIM, axis=1)),
        "ssd_norm_w": rows(p["ssd_norm_w"]),
        "w_cm_br": _to_bf16(p["w_cm_br"], 512), "w_ssd_br": _to_bf16(p["w_ssd_br"], 512),
        "w_o": _to_bf16(p["w_o"], 512),
        "ln1_g": rows(p["ln1_g"]), "ln1_b": rows(p["ln1_b"]),
        "w_ffn_gate": _to_bf16(p["w_ffn_gate"], 512), "w_ffn_up": _to_bf16(p["w_ffn_up"], 512),
        "w_ffn_down": _to_bf16(p["w_ffn_down"], D_FF // 4),
        "ln2_g": rows(p["ln2_g"]), "ln2_b": rows(p["ln2_b"]),
    }


def kernel(x_prompt, x_sample, state_ssd, state_conv, c_prompt, c_sample, w_ada, b_ada, w_in, b_gate,
           ln_v_g, ln_v_b, w_spatial, b_spatial, conv_w, conv_b, dt_bias, a_log, d_skip, ssd_norm_w,
           w_cm_br, w_ssd_br, w_o, ln1_g, ln1_b, w_ffn_gate, w_ffn_up, w_ffn_down, ln2_g, ln2_b):
    P = _params(dict(w_in=w_in, b_gate=b_gate, ln_v_g=ln_v_g, ln_v_b=ln_v_b, w_spatial=w_spatial,
                     b_spatial=b_spatial, conv_w=conv_w, conv_b=conv_b, dt_bias=dt_bias, a_log=a_log,
                     d_skip=d_skip, ssd_norm_w=ssd_norm_w, w_cm_br=w_cm_br, w_ssd_br=w_ssd_br, w_o=w_o,
                     ln1_g=ln1_g, ln1_b=ln1_b, w_ffn_gate=w_ffn_gate, w_ffn_up=w_ffn_up,
                     w_ffn_down=w_ffn_down, ln2_g=ln2_g, ln2_b=ln2_b))
    consts = _constants()
    mod, mod_p = _adaln_mod(jnp.concatenate([c_sample, c_prompt], axis=0), w_ada, b_ada)
    mod4 = mod_p.reshape(DEPTH, BATCH, 6, D_MODEL)
    xp = x_prompt
    xs = x_sample.reshape(DEC_BATCH, D_MODEL)
    ssd_p, conv_p, conv_s, v_s = [], [], [], []
    state_in = state_ssd.reshape(DEPTH, DEC_BATCH, D_INNER, STATE)
    cstate = state_conv.reshape(DEPTH, DEC_BATCH, (CONV_K - 1) * CONV_DIM)
    ssd_s = None
    for i in range(DEPTH):
        x1, sp, cp = _prompt_mixer(i, xp, mod4, P, consts)
        xp = _prompt_ffn(i, x1, mod4, P)
        ssd_p.append(sp.reshape(BATCH, HEADS, HEAD_DIM, STATE))
        conv_p.append(cp)
        v, cs, da, xdtt, bm, cm, cmpart, zs, gs, xsd = _sample_pre(i, xs, mod, cstate, P, consts)
        ssd_s, y = _sample_ssd(i, da[:, :HEADS], xdtt, bm, cm, state_in, ssd_s)
        xs = _sample_post(i, xs, mod, y, xsd, zs, gs, cmpart, P)
        conv_s.append(cs.reshape(DEC_BATCH, CONV_K - 1, CONV_DIM))
        v_s.append(v.reshape(DEC_BATCH, 1, D_CM))
    return (xp, xs.reshape(DEC_BATCH, 1, D_MODEL), jnp.stack(ssd_p), jnp.stack(conv_p),
            ssd_s.reshape(DEPTH, DEC_BATCH, HEADS, HEAD_DIM, STATE), jnp.stack(conv_s), jnp.stack(v_s))
```

```python
import functools
import math

import numpy as np
import jax
import jax.numpy as jnp
from jax import lax
from jax.experimental import pallas as pl
from jax.experimental.pallas import tpu as pltpu

D_MODEL = 1024
BATCH = 8
SEQ = 2048
DEPTH = 2
DEC_BATCH = 128
CHUNK = 128
D_CM = D_MODEL
CM_GROUPS = 8
D_INNER = 2048
HEAD_DIM = 64
HEADS = 32
SSD_GROUPS = 4
HEADS_PER_GROUP = HEADS // SSD_GROUPS
STATE = 128
CONV_K = 4
CONV_DIM = D_INNER + 2 * SSD_GROUPS * STATE
D_FF = 2816
ALPHA = (2 * DEPTH) ** 0.25
LN_EPS = 1e-5
REF_DT = 2 * D_CM + D_INNER + CONV_DIM
REF_GATES = REF_DT + HEADS

LANES = 128
DT_COPIES = 3
C_U = 0
C_V = C_U + D_CM
C_Z = C_V + D_CM
C_XBC = C_Z + D_INNER
C_END = C_XBC + CONV_DIM

TILE_L = 256
TILE_F = 512
SSD_BB = 8
VMEM_LIMIT = 56 * 1024 * 1024

F32 = jnp.float32
BF16 = jnp.bfloat16


def _bdot(a, b):
    return jnp.dot(a, b, preferred_element_type=F32)


def _gelu(x):
    return 0.5 * x * (1.0 + lax.erf(x * np.float32(math.sqrt(0.5))))


def _sigmoid(x):
    return 1.0 / (1.0 + jnp.exp(-x))


def _silu(x):
    return x * _sigmoid(x)


def _softplus(x):
    return jnp.maximum(x, 0.0) + jnp.log(1.0 + jnp.exp(-jnp.abs(x)))


def _layer_norm(x, g, b):
    mu = jnp.mean(x, axis=-1, keepdims=True)
    xc = x - mu
    var = jnp.mean(xc * xc, axis=-1, keepdims=True)
    return xc * lax.rsqrt(var + LN_EPS) * g + b


def _rms_norm(x, g):
    return x * lax.rsqrt(jnp.mean(x * x, axis=-1, keepdims=True) + LN_EPS) * g


def _pack3(x):
    hi = x.astype(BF16).astype(F32)
    r1 = x - hi
    mid = r1.astype(BF16).astype(F32)
    r2 = r1 - mid
    lane = lax.broadcasted_iota(jnp.int32, x.shape, 1)
    return jnp.where(lane < HEADS, x, jnp.where(lane < 2 * HEADS, r1, r2)).astype(BF16)


def _const_spec(shape, single=True):
    nd = len(shape)
    kw = {"pipeline_mode": pl.Buffered(1)} if single else {}
    return pl.BlockSpec(shape, lambda *_: (0,) * nd, **kw)


def _layer_spec(shape, layer):
    nd = len(shape)
    return pl.BlockSpec((None,) + tuple(shape), lambda *_: (layer,) + (0,) * nd,
                        pipeline_mode=pl.Buffered(1))


def _cast_kernel(w_ref, o_ref):
    o_ref[...] = w_ref[...].astype(BF16)


def _to_bf16(w, rows):
    depth, r, c = w.shape
    return pl.pallas_call(
        _cast_kernel,
        grid=(depth, r // rows),
        in_specs=[pl.BlockSpec((None, rows, c), lambda l, i: (l, i, 0))],
        out_specs=pl.BlockSpec((None, rows, c), lambda l, i: (l, i, 0)),
        out_shape=jax.ShapeDtypeStruct((depth, r, c), BF16),
        compiler_params=pltpu.CompilerParams(
            dimension_semantics=("parallel", "parallel"), vmem_limit_bytes=VMEM_LIMIT),
        name="cast_bf16",
    )(w)


def _tcast_kernel(w_ref, o_ref):
    o_ref[...] = w_ref[...].T.astype(BF16)


def _tcast_dt_kernel(w_ref, o_ref):
    t = w_ref[...].T
    lane = lax.broadcasted_iota(jnp.int32, t.shape, 1)
    o_ref[...] = jnp.where(lane < HEADS, t, 0.0).astype(BF16)


def _win_prep(w_in, rows=512):
    wt = jnp.swapaxes(w_in, 1, 2)
    params = pltpu.CompilerParams(dimension_semantics=("parallel", "parallel"), vmem_limit_bytes=VMEM_LIMIT)

    def piece(kern, n_out, in_spec, nblk, blk):
        return pl.pallas_call(
            kern, grid=(DEPTH, nblk), in_specs=[in_spec],
            out_specs=pl.BlockSpec((None, D_MODEL, blk), lambda l, i: (l, 0, i)),
            out_shape=jax.ShapeDtypeStruct((DEPTH, D_MODEL, n_out), BF16),
            compiler_params=params, name="w_in_prep")(wt)

    main = piece(_tcast_kernel, C_END, pl.BlockSpec((None, rows, D_MODEL), lambda l, i: (l, i, 0)),
                 C_END // rows, rows)
    gates = piece(lambda w_ref, o_ref: _tcast_kernel(w_ref.at[0], o_ref), 2 * D_MODEL,
                  pl.BlockSpec((pl.Element(1), pl.Element(rows), pl.Element(D_MODEL)),
                               lambda l, i: (l, pl.multiple_of(REF_GATES + i * rows, 8), 0)),
                  2 * D_MODEL // rows, rows)
    dt = piece(_tcast_dt_kernel, LANES,
               pl.BlockSpec((None, LANES, D_MODEL), lambda l, i: (l, REF_DT // LANES, 0)), 1, LANES)
    return {"w_in": main, "w_gates": gates, "w_dt": dt}


def _dt_cols(h, wdt_ref):
    d = _bdot(h, wdt_ref[...])
    out = d
    for k in range(1, DT_COPIES):
        out = out + pltpu.roll(d, k * HEADS, 1)
    return out


def _mod_kernel(c_ref, w_ref, b_ref, os_ref, op_ref):
    s = _silu(c_ref[...])
    res = _bdot(s.astype(BF16), w_ref[...].astype(BF16)) + b_ref[...]
    os_ref[...] = res[:DEC_BATCH]
    op_ref[...] = res[DEC_BATCH:]


def _adaln_mod(c_all, w_ada, b_ada):
    rows = c_all.shape[0]
    nblk = 6
    return pl.pallas_call(
        _mod_kernel,
        grid=(DEPTH, nblk),
        in_specs=[
            pl.BlockSpec((rows, D_MODEL), lambda l, k: (0, 0)),
            pl.BlockSpec((None, D_MODEL, D_MODEL), lambda l, k: (l, 0, k)),
            pl.BlockSpec((None, 1, D_MODEL), lambda l, k: (l, 0, k)),
        ],
        out_specs=[pl.BlockSpec((None, DEC_BATCH, D_MODEL), lambda l, k: (l, 0, k)),
                   pl.BlockSpec((None, BATCH, D_MODEL), lambda l, k: (l, 0, k))],
        out_shape=[jax.ShapeDtypeStruct((DEPTH, DEC_BATCH, nblk * D_MODEL), F32),
                   jax.ShapeDtypeStruct((DEPTH, BATCH, nblk * D_MODEL), F32)],
        compiler_params=pltpu.CompilerParams(
            dimension_semantics=("parallel", "parallel"), vmem_limit_bytes=VMEM_LIMIT),
        name="adaln_mod",
    )(c_all, w_ada, b_ada.reshape(DEPTH, 1, nblk * D_MODEL))


def _cm_branch(u_v, lnvg, lnvb):
    u = _gelu(u_v[:, :D_CM])
    v = _layer_norm(_gelu(u_v[:, D_CM:]), lnvg, lnvb)
    return u, v


def _post_mixer(x, g1, merged_bf16, wo_ref, ln1g, ln1b):
    o = _bdot(merged_bf16, wo_ref[...])
    return _layer_norm(ALPHA * x + g1 * o, ln1g, ln1b)


def _ffn(x1, sh2, sc2, g2, wg_ref, wu_ref, wd_ref, ln2g, ln2b):
    h2 = (x1 * (1.0 + sc2) + sh2).astype(BF16)
    gate = _bdot(h2, wg_ref[...])
    up = _bdot(h2, wu_ref[...])
    act = (_silu(gate) * up).astype(BF16)
    ffn = _bdot(act, wd_ref[...])
    return _layer_norm(ALPHA * x1 + g2 * ffn, ln2g, ln2b)


def _ssd_chunk(xs_c, b_c, c_c, dtseg, dtb, alog, tri, e64, state_ref, y_ref, r0):
    row = lax.broadcasted_iota(jnp.int32, (CHUNK, LANES), 0)
    lane = lax.broadcasted_iota(jnp.int32, (CHUNK, LANES), 1)
    tril = row >= lane
    low_half = lane < HEAD_DIM

    dt = _softplus(dtseg + dtb)
    a = -jnp.exp(alog)
    da = dt * a
    hi = da.astype(BF16)
    r1 = da - hi.astype(F32)
    mid = r1.astype(BF16)
    lo = (r1 - mid.astype(F32)).astype(BF16)
    acum = _bdot(tri, hi) + _bdot(tri, mid) + _bdot(tri, lo)
    shift_t = acum.T - jnp.log(dt.T)
    alast = acum[CHUNK - 1:CHUNK, :]
    w = dt * jnp.exp(alast - acum)
    wexp = _bdot(_pack3(w), e64)
    eexp = _bdot(_pack3(jnp.exp(acum)), e64)
    cd = jnp.broadcast_to(jnp.exp(alast), (8, LANES))
    cdexp = _bdot(_pack3(cd), e64)[0:1, :]
    xs_b = xs_c.astype(BF16)
    zero_b = jnp.zeros((CHUNK, LANES), BF16)

    gw = HEADS_PER_GROUP * HEAD_DIM
    for g in range(SSD_GROUPS):
        gc = g * gw
        bg = b_c[:, g * STATE:(g + 1) * STATE]
        cgb = c_c[:, g * STATE:(g + 1) * STATE].astype(BF16)
        cb = lax.dot_general(cgb, bg.astype(BF16), (((1,), (1,)), ((), ())),
                             preferred_element_type=F32)
        y_off = eexp[:, gc:gc + gw] * _bdot(cgb, state_ref[:, gc:gc + gw].astype(BF16))
        for pr in range(HEADS_PER_GROUP // 2):
            h0 = g * HEADS_PER_GROUP + 2 * pr
            pieces = []
            for h in (h0, h0 + 1):
                seg = jnp.broadcast_to(acum[:, h:h + 1], (CHUNK, LANES)) - shift_t[h:h + 1, :]
                pieces.append((cb * jnp.exp(jnp.where(tril, seg, -jnp.inf))).astype(BF16))
            lhs = jnp.concatenate(pieces, axis=1)
            pc = h0 * HEAD_DIM
            xs_pair = xs_b[:, pc:pc + LANES]
            rhs = jnp.concatenate([jnp.where(low_half, xs_pair, zero_b),
                                   jnp.where(low_half, zero_b, xs_pair)], axis=0)
            y_ref[r0:r0 + CHUNK, pc:pc + LANES] = (
                _bdot(lhs, rhs) + y_off[:, pc - gc:pc - gc + LANES])
        xw = (xs_c[:, gc:gc + gw] * wexp[:, gc:gc + gw]).astype(BF16)
        upd = _bdot(bg.T.astype(BF16), xw)
        state_ref[:, gc:gc + gw] = state_ref[:, gc:gc + gw] * cdexp[:, gc:gc + gw] + upd


def _p1_kernel(x_ref, mod_ref, win_ref, wgate_ref, wdt_ref, bg_ref, lnvg_ref, lnvb_ref, wsp_ref, bsp_ref,
               convw_ref, convb_ref, dtb_ref, alog_ref, dskip_ref, normw_ref,
               wcm_ref, wssd_ref, wo_ref, ln1g_ref, ln1b_ref, tri_ref, e64_ref,
               x1_ref, ssd_ref, conv_ref,
               state_ref, cbuf_ref, y_ref, *, tl):
    j = pl.program_id(1)
    last = pl.num_programs(1) - 1
    nc = tl // CHUNK

    @pl.when(j == 0)
    def _():
        state_ref[...] = jnp.zeros_like(state_ref)
        cbuf_ref[0:8, :] = jnp.zeros((8, CONV_DIM), F32)

    x = x_ref[...]
    sh1 = mod_ref[0:1, :]
    sc1 = mod_ref[1:2, :]
    g1 = mod_ref[2:3, :]
    h = (x * (1.0 + sc1) + sh1).astype(BF16)

    cbuf_ref[8:8 + tl, :] = _bdot(h, win_ref[:, C_XBC:C_END])
    dtseg = _dt_cols(h, wdt_ref)
    u_v = _bdot(h, win_ref[:, C_U:C_Z])

    acc = convb_ref[...]
    for k in range(CONV_K):
        acc = acc + cbuf_ref[5 + k:5 + k + tl, :] * convw_ref[k:k + 1, :]
    tail = cbuf_ref[tl + 5:tl + 8, :]
    cbuf_ref[5:8, :] = tail
    xbc_c = _silu(acc)
    xs = xbc_c[:, :D_INNER]

    z = _bdot(h, win_ref[:, C_Z:C_XBC])
    gcm_logit = _bdot(h, wgate_ref[:, :D_MODEL])
    gssd_logit = _bdot(h, wgate_ref[:, D_MODEL:])

    u, v = _cm_branch(u_v, lnvg_ref[...], lnvb_ref[...])
    vb = v.astype(BF16)
    row = lax.broadcasted_iota(jnp.int32, (CHUNK, CHUNK), 0)
    col = lax.broadcasted_iota(jnp.int32, (CHUNK, CHUNK), 1)
    wmix = [jnp.where(row >= col, wsp_ref[g], 0.0).astype(BF16) for g in range(CM_GROUPS)]

    for c in range(nc):
        r0 = c * CHUNK
        _ssd_chunk(xs[r0:r0 + CHUNK], xbc_c[r0:r0 + CHUNK, D_INNER:D_INNER + SSD_GROUPS * STATE],
                   xbc_c[r0:r0 + CHUNK, D_INNER + SSD_GROUPS * STATE:], dtseg[r0:r0 + CHUNK],
                   dtb_ref[...], alog_ref[...], tri_ref[...], e64_ref[...],
                   state_ref, y_ref, r0)

    rows = []
    for c in range(nc):
        cols = []
        for g in range(CM_GROUPS):
            cols.append(_bdot(wmix[g], vb[c * CHUNK:(c + 1) * CHUNK, g * LANES:(g + 1) * LANES]))
        rows.append(jnp.concatenate(cols, axis=1) + bsp_ref[...])
    mix = jnp.concatenate(rows, axis=0) if nc > 1 else rows[0]
    out_cm = (u * mix).astype(BF16)
    merged = _sigmoid(gcm_logit + bg_ref[:, :D_MODEL]) * _bdot(out_cm, wcm_ref[...])

    yz = (y_ref[...] + xs * dskip_ref[...]) * _silu(z)
    yn = _rms_norm(yz, normw_ref[...]).astype(BF16)
    merged = merged + _sigmoid(gssd_logit + bg_ref[:, D_MODEL:]) * _bdot(yn, wssd_ref[...])
    x1_ref[...] = _post_mixer(x, g1, merged.astype(BF16), wo_ref, ln1g_ref[...], ln1b_ref[...])

    @pl.when(j == last)
    def _():
        conv_ref[...] = cbuf_ref[5:8, :]
        ssd_ref[...] = state_ref[...].T


def _mod_prompt_spec(layer):
    return pl.BlockSpec((None, None, 6, D_MODEL), lambda b, j: (layer, b, 0, 0))


def _prompt_mixer(layer, x, mod4, P, consts):
    tl = TILE_L
    nj = SEQ // tl
    ls = lambda *shape: _layer_spec(shape, layer)
    in_specs = [
        pl.BlockSpec((None, tl, D_MODEL), lambda b, j: (b, j, 0)),
        _mod_prompt_spec(layer),
        ls(D_MODEL, C_END), ls(D_MODEL, 2 * D_MODEL), ls(D_MODEL, LANES),
        ls(1, 2 * D_MODEL), ls(1, D_CM), ls(1, D_CM),
        ls(CM_GROUPS, CHUNK, CHUNK), ls(CHUNK, D_CM),
        ls(CONV_K, CONV_DIM), ls(1, CONV_DIM), ls(1, LANES), ls(1, LANES),
        ls(1, D_INNER), ls(1, D_INNER),
        ls(D_CM, D_MODEL), ls(D_INNER, D_MODEL), ls(D_MODEL, D_MODEL),
        ls(1, D_MODEL), ls(1, D_MODEL),
        _const_spec((CHUNK, CHUNK)), _const_spec((LANES, D_INNER)),
    ]
    out_specs = [
        pl.BlockSpec((None, tl, D_MODEL), lambda b, j: (b, j, 0)),
        pl.BlockSpec((None, D_INNER, STATE), lambda b, j: (b, 0, 0)),
        pl.BlockSpec((None, CONV_K - 1, CONV_DIM), lambda b, j: (b, 0, 0)),
    ]
    out_shape = [
        jax.ShapeDtypeStruct((BATCH, SEQ, D_MODEL), F32),
        jax.ShapeDtypeStruct((BATCH, D_INNER, STATE), F32),
        jax.ShapeDtypeStruct((BATCH, CONV_K - 1, CONV_DIM), F32),
    ]
    return pl.pallas_call(
        functools.partial(_p1_kernel, tl=tl),
        grid=(BATCH, nj),
        in_specs=in_specs, out_specs=out_specs, out_shape=out_shape,
        scratch_shapes=[pltpu.VMEM((STATE, D_INNER), F32),
                        pltpu.VMEM((tl + 8, CONV_DIM), F32),
                        pltpu.VMEM((tl, D_INNER), F32)],
        compiler_params=pltpu.CompilerParams(
            dimension_semantics=("parallel", "arbitrary"), vmem_limit_bytes=VMEM_LIMIT),
        name="prompt_mixer",
    )(x, mod4, P["w_in"], P["w_gates"], P["w_dt"], P["b_gate"], P["ln_v_g"], P["ln_v_b"], P["w_spatial"], P["bsp_full"],
      P["conv_w"], P["conv_b"], P["dt_bias3"], P["a_log3"], P["dskip_exp"], P["ssd_norm_w"],
      P["w_cm_br"], P["w_ssd_br"], P["w_o"], P["ln1_g"], P["ln1_b"],
      consts["tri"], consts["e64"])


def _p2_kernel(x_ref, mod_ref, wg_ref, wu_ref, wd_ref, ln2g_ref, ln2b_ref, o_ref):
    o_ref[...] = _ffn(x_ref[...], mod_ref[3:4, :], mod_ref[4:5, :], mod_ref[5:6, :],
                      wg_ref, wu_ref, wd_ref, ln2g_ref[...], ln2b_ref[...])


def _prompt_ffn(layer, x1, mod4, P):
    tf = TILE_F
    ls = lambda *shape: _layer_spec(shape, layer)
    return pl.pallas_call(
        _p2_kernel,
        grid=(BATCH, SEQ // tf),
        in_specs=[
            pl.BlockSpec((None, tf, D_MODEL), lambda b, j: (b, j, 0)),
            _mod_prompt_spec(layer),
            ls(D_MODEL, D_FF), ls(D_MODEL, D_FF), ls(D_FF, D_MODEL),
            ls(1, D_MODEL), ls(1, D_MODEL),
        ],
        out_specs=pl.BlockSpec((None, tf, D_MODEL), lambda b, j: (b, j, 0)),
        out_shape=jax.ShapeDtypeStruct((BATCH, SEQ, D_MODEL), F32),
        compiler_params=pltpu.CompilerParams(
            dimension_semantics=("parallel", "parallel"), vmem_limit_bytes=VMEM_LIMIT),
        name="prompt_ffn",
    )(x1, mod4, P["w_ffn_gate"], P["w_ffn_up"], P["w_ffn_down"], P["ln2_g"], P["ln2_b"])


def _s_pre_kernel(x_ref, mod_ref, win_ref, wgate_ref, wdt_ref, bg_ref, lnvg_ref, lnvb_ref, wsp0_ref, bsp0_ref,
                  convw_ref, convb_ref, cstate_ref, dtb_ref, alog_ref, dskip_ref, wcm_ref, e64_ref,
                  v_ref, conv_ref, da_ref, xdtt_ref, b_ref, c_ref, cmpart_ref, zs_ref, gs_ref, xsd_ref):
    x = x_ref[...]
    sh1 = mod_ref[:, 0:D_MODEL]
    sc1 = mod_ref[:, D_MODEL:2 * D_MODEL]
    h = (x * (1.0 + sc1) + sh1).astype(BF16)

    u, v = _cm_branch(_bdot(h, win_ref[:, C_U:C_Z]), lnvg_ref[...], lnvb_ref[...])
    v_ref[...] = v
    out_cm = (u * (wsp0_ref[...] * v + bsp0_ref[...])).astype(BF16)
    gate_cm = _sigmoid(_bdot(h, wgate_ref[:, :D_MODEL]) + bg_ref[:, :D_MODEL])
    cmpart_ref[...] = gate_cm * _bdot(out_cm, wcm_ref[...])

    xbc = _bdot(h, win_ref[:, C_XBC:C_END])
    acc = convb_ref[...]
    for k in range(CONV_K - 1):
        acc = acc + cstate_ref[:, k * CONV_DIM:(k + 1) * CONV_DIM] * convw_ref[k:k + 1, :]
    acc = acc + xbc * convw_ref[CONV_K - 1:CONV_K, :]
    conv_ref[:, 0:(CONV_K - 2) * CONV_DIM] = cstate_ref[:, CONV_DIM:(CONV_K - 1) * CONV_DIM]
    conv_ref[:, (CONV_K - 2) * CONV_DIM:] = xbc
    xbc_c = _silu(acc)
    xs = xbc_c[:, :D_INNER]
    b_ref[...] = xbc_c[:, D_INNER:D_INNER + SSD_GROUPS * STATE]
    c_ref[...] = xbc_c[:, D_INNER + SSD_GROUPS * STATE:]
    dt = _softplus(_dt_cols(h, wdt_ref) + dtb_ref[...])
    da_ref[...] = jnp.exp(dt * (-jnp.exp(alog_ref[...])))
    xdt = xs * _bdot(_pack3(dt), e64_ref[...])
    xdtt_ref[...] = xdt.T.astype(BF16)
    xsd_ref[...] = xs * dskip_ref[...]
    zs_ref[...] = _silu(_bdot(h, win_ref[:, C_Z:C_XBC]))
    gs_ref[...] = _sigmoid(_bdot(h, wgate_ref[:, D_MODEL:]) + bg_ref[:, D_MODEL:])


def _sample_pre(layer, x, mod, cstate, P, consts):
    n = DEC_BATCH
    ls = lambda *shape: _layer_spec(shape, layer)
    in_specs = [
        _const_spec((n, D_MODEL)), ls(n, 6 * D_MODEL),
        ls(D_MODEL, C_END), ls(D_MODEL, 2 * D_MODEL), ls(D_MODEL, LANES),
        ls(1, 2 * D_MODEL), ls(1, D_CM), ls(1, D_CM), ls(1, D_CM), ls(1, D_CM),
        ls(CONV_K, CONV_DIM), ls(1, CONV_DIM), ls(n, (CONV_K - 1) * CONV_DIM),
        ls(1, LANES), ls(1, LANES), ls(1, D_INNER), ls(D_CM, D_MODEL), _const_spec((LANES, D_INNER)),
    ]
    outs = [((n, D_CM), F32), ((n, (CONV_K - 1) * CONV_DIM), F32), ((n, LANES), F32),
            ((D_INNER, n), BF16), ((n, SSD_GROUPS * STATE), F32), ((n, SSD_GROUPS * STATE), F32),
            ((n, D_MODEL), F32), ((n, D_INNER), F32), ((n, D_MODEL), F32), ((n, D_INNER), F32)]
    return pl.pallas_call(
        _s_pre_kernel,
        grid=(1,),
        in_specs=in_specs,
        out_specs=[_const_spec(s, single=False) for s, _ in outs],
        out_shape=[jax.ShapeDtypeStruct(s, d) for s, d in outs],
        compiler_params=pltpu.CompilerParams(
            dimension_semantics=("arbitrary",), vmem_limit_bytes=VMEM_LIMIT),
        name="sample_pre",
    )(x, mod, P["w_in"], P["w_gates"], P["w_dt"], P["b_gate"], P["ln_v_g"], P["ln_v_b"], P["wsp0"], P["bsp0"],
      P["conv_w"], P["conv_b"], cstate, P["dt_bias3"], P["a_log3"], P["dskip_exp"], P["w_cm_br"],
      consts["e64"])


def _s_ssd_kernel(da_ref, xdtt_ref, b_ref, c_ref, st_ref, *rest):
    sto_ref, y_ref = rest[-2:]
    i = pl.program_id(0)

    @pl.when(i == 0)
    def _():
        y_ref[...] = jnp.zeros_like(y_ref)

    rows = lax.broadcasted_iota(jnp.int32, (DEC_BATCH, STATE), 0)
    rows8 = lax.broadcasted_iota(jnp.int32, (8, STATE), 0)
    gw = HEADS_PER_GROUP * HEAD_DIM
    for q in range(SSD_BB):
        b = i * SSD_BB + q
        b8 = pl.multiple_of((b // 8) * 8, 8)
        for g in range(SSD_GROUPS):
            bg = jnp.where(rows == b, b_ref[:, g * STATE:(g + 1) * STATE], 0.0).astype(BF16)
            xb = _bdot(xdtt_ref[g * gw:(g + 1) * gw, :], bg)
            blocks = []
            for hh in range(HEADS_PER_GROUP):
                hd = g * HEADS_PER_GROUP + hh
                r = hd * HEAD_DIM
                blk = st_ref[q, r:r + HEAD_DIM, :] * da_ref[b, hd] + xb[hh * HEAD_DIM:(hh + 1) * HEAD_DIM, :]
                sto_ref[q, r:r + HEAD_DIM, :] = blk
                blocks.append(blk.astype(BF16))
            newg = jnp.concatenate(blocks, axis=0)
            cg = jnp.where(rows8 == b - b8, c_ref[pl.ds(b8, 8), g * STATE:(g + 1) * STATE], 0.0).astype(BF16)
            yg = lax.dot_general(cg, newg, (((1,), (1,)), ((), ())), preferred_element_type=F32)
            y_ref[pl.ds(b8, 8), g * gw:(g + 1) * gw] += yg


def _sample_ssd(layer, da, xdtt, bm, cm, state, stacked):
    n = DEC_BATCH
    in_specs = [
        pl.BlockSpec((D_INNER, n), lambda i, da: (0, 0)),
        pl.BlockSpec((n, SSD_GROUPS * STATE), lambda i, da: (0, 0)),
        pl.BlockSpec((n, SSD_GROUPS * STATE), lambda i, da: (0, 0)),
        pl.BlockSpec((None, SSD_BB, D_INNER, STATE), lambda i, da: (layer, i, 0, 0)),
    ]
    args = [da, xdtt, bm, cm, state]
    aliases = {}
    if stacked is not None:
        in_specs.append(pl.BlockSpec(memory_space=pl.ANY))
        args.append(stacked)
        aliases = {len(args) - 1: 0}
    grid_spec = pltpu.PrefetchScalarGridSpec(
        num_scalar_prefetch=1,
        grid=(n // SSD_BB,),
        in_specs=in_specs,
        out_specs=[
            pl.BlockSpec((None, SSD_BB, D_INNER, STATE), lambda i, da: (layer, i, 0, 0)),
            pl.BlockSpec((n, D_INNER), lambda i, da: (0, 0)),
        ],
    )
    return pl.pallas_call(
        _s_ssd_kernel,
        grid_spec=grid_spec,
        out_shape=[jax.ShapeDtypeStruct((DEPTH, n, D_INNER, STATE), F32),
                   jax.ShapeDtypeStruct((n, D_INNER), F32)],
        input_output_aliases=aliases,
        compiler_params=pltpu.CompilerParams(
            dimension_semantics=("arbitrary",), vmem_limit_bytes=VMEM_LIMIT),
        name="sample_ssd",
    )(*args)


def _s_post_kernel(x_ref, mod_ref, y_ref, xsd_ref, zs_ref, gs_ref, cmpart_ref, normw_ref,
                   wssd_ref, wo_ref, ln1g_ref, ln1b_ref, wg_ref, wu_ref, wd_ref, ln2g_ref, ln2b_ref, o_ref):
    md = lambda k: mod_ref[:, k * D_MODEL:(k + 1) * D_MODEL]
    yn = _rms_norm((y_ref[...] + xsd_ref[...]) * zs_ref[...], normw_ref[...]).astype(BF16)
    merged = cmpart_ref[...] + gs_ref[...] * _bdot(yn, wssd_ref[...])
    x1 = _post_mixer(x_ref[...], md(2), merged.astype(BF16), wo_ref, ln1g_ref[...], ln1b_ref[...])
    o_ref[...] = _ffn(x1, md(3), md(4), md(5), wg_ref, wu_ref, wd_ref, ln2g_ref[...], ln2b_ref[...])


def _sample_post(layer, x, mod, y, xsd, zs, gs, cmpart, P):
    n = DEC_BATCH
    ls = lambda *shape: _layer_spec(shape, layer)
    cs = _const_spec
    in_specs = [
        cs((n, D_MODEL)), ls(n, 6 * D_MODEL), cs((n, D_INNER)), cs((n, D_INNER)), cs((n, D_INNER)),
        cs((n, D_MODEL)), cs((n, D_MODEL)), ls(1, D_INNER), ls(D_INNER, D_MODEL), ls(D_MODEL, D_MODEL),
        ls(1, D_MODEL), ls(1, D_MODEL), ls(D_MODEL, D_FF), ls(D_MODEL, D_FF), ls(D_FF, D_MODEL),
        ls(1, D_MODEL), ls(1, D_MODEL),
    ]
    return pl.pallas_call(
        _s_post_kernel,
        grid=(1,),
        in_specs=in_specs,
        out_specs=_const_spec((n, D_MODEL), single=False),
        out_shape=jax.ShapeDtypeStruct((n, D_MODEL), F32),
        compiler_params=pltpu.CompilerParams(
            dimension_semantics=("arbitrary",), vmem_limit_bytes=VMEM_LIMIT),
        name="sample_post",
    )(x, mod, y, xsd, zs, gs, cmpart, P["ssd_norm_w"], P["w_ssd_br"], P["w_o"], P["ln1_g"], P["ln1_b"],
      P["w_ffn_gate"], P["w_ffn_up"], P["w_ffn_down"], P["ln2_g"], P["ln2_b"])


def _constants():
    tri = np.tril(np.ones((CHUNK, CHUNK), np.float32))
    e64 = np.zeros((LANES, D_INNER), np.float32)
    for r in range(DT_COPIES * HEADS):
        hd = r % HEADS
        e64[r, hd * HEAD_DIM:(hd + 1) * HEAD_DIM] = 1.0
    return {"tri": jnp.asarray(tri, BF16), "e64": jnp.asarray(e64, BF16)}


def _params(p):
    rows = lambda a: a[:, None, :]
    pad = jnp.zeros((DEPTH, LANES - DT_COPIES * HEADS), F32)
    return {
        **_win_prep(p["w_in"]),
        "b_gate": rows(p["b_gate"]),
        "ln_v_g": rows(p["ln_v_g"]), "ln_v_b": rows(p["ln_v_b"]),
        "w_spatial": p["w_spatial"],
        "bsp_full": jnp.repeat(jnp.swapaxes(p["b_spatial"], 1, 2), LANES, axis=2),
        "wsp0": rows(jnp.repeat(p["w_spatial"][:, :, 0, 0], LANES, axis=1)),
        "bsp0": rows(jnp.repeat(p["b_spatial"][:, :, 0], LANES, axis=1)),
        "conv_w": p["conv_w"], "conv_b": rows(p["conv_b"]),
        "dt_bias3": rows(jnp.concatenate([p["dt_bias"]] * DT_COPIES + [pad], axis=1)),
        "a_log3": rows(jnp.concatenate([p["a_log"]] * DT_COPIES + [pad], axis=1)),
        "dskip_exp": rows(jnp.repeat(p["d_skip"], HEAD_DIM, axis=1)),
        "ssd_norm_w": rows(p["ssd_norm_w"]),
        "w_cm_br": _to_bf16(p["w_cm_br"], 512), "w_ssd_br": _to_bf16(p["w_ssd_br"], 512),
        "w_o": _to_bf16(p["w_o"], 512),
        "ln1_g": rows(p["ln1_g"]), "ln1_b": rows(p["ln1_b"]),
        "w_ffn_gate": _to_bf16(p["w_ffn_gate"], 512), "w_ffn_up": _to_bf16(p["w_ffn_up"], 512),
        "w_ffn_down": _to_bf16(p["w_ffn_down"], D_FF // 4),
        "ln2_g": rows(p["ln2_g"]), "ln2_b": rows(p["ln2_b"]),
    }


def kernel(x_prompt, x_sample, state_ssd, state_conv, c_prompt, c_sample, w_ada, b_ada, w_in, b_gate,
           ln_v_g, ln_v_b, w_spatial, b_spatial, conv_w, conv_b, dt_bias, a_log, d_skip, ssd_norm_w,
           w_cm_br, w_ssd_br, w_o, ln1_g, ln1_b, w_ffn_gate, w_ffn_up, w_ffn_down, ln2_g, ln2_b):
    P = _params(dict(w_in=w_in, b_gate=b_gate, ln_v_g=ln_v_g, ln_v_b=ln_v_b, w_spatial=w_spatial,
                     b_spatial=b_spatial, conv_w=conv_w, conv_b=conv_b, dt_bias=dt_bias, a_log=a_log,
                     d_skip=d_skip, ssd_norm_w=ssd_norm_w, w_cm_br=w_cm_br, w_ssd_br=w_ssd_br, w_o=w_o,
                     ln1_g=ln1_g, ln1_b=ln1_b, w_ffn_gate=w_ffn_gate, w_ffn_up=w_ffn_up,
                     w_ffn_down=w_ffn_down, ln2_g=ln2_g, ln2_b=ln2_b))
    consts = _constants()
    mod, mod_p = _adaln_mod(jnp.concatenate([c_sample, c_prompt], axis=0), w_ada, b_ada)
    mod4 = mod_p.reshape(DEPTH, BATCH, 6, D_MODEL)
    xp = x_prompt
    xs = x_sample.reshape(DEC_BATCH, D_MODEL)
    ssd_p, conv_p, conv_s, v_s = [], [], [], []
    state_in = state_ssd.reshape(DEPTH, DEC_BATCH, D_INNER, STATE)
    cstate = state_conv.reshape(DEPTH, DEC_BATCH, (CONV_K - 1) * CONV_DIM)
    ssd_s = None
    for i in range(DEPTH):
        x1, sp, cp = _prompt_mixer(i, xp, mod4, P, consts)
        xp = _prompt_ffn(i, x1, mod4, P)
        ssd_p.append(sp.reshape(BATCH, HEADS, HEAD_DIM, STATE))
        conv_p.append(cp)
        v, cs, da, xdtt, bm, cm, cmpart, zs, gs, xsd = _sample_pre(i, xs, mod, cstate, P, consts)
        ssd_s, y = _sample_ssd(i, da[:, :HEADS], xdtt, bm, cm, state_in, ssd_s)
        xs = _sample_post(i, xs, mod, y, xsd, zs, gs, cmpart, P)
        conv_s.append(cs.reshape(DEC_BATCH, CONV_K - 1, CONV_DIM))
        v_s.append(v.reshape(DEC_BATCH, 1, D_CM))
    return (xp, xs.reshape(DEC_BATCH, 1, D_MODEL), jnp.stack(ssd_p), jnp.stack(conv_p),
            ssd_s.reshape(DEPTH, DEC_BATCH, HEADS, HEAD_DIM, STATE), jnp.stack(conv_s), jnp.stack(v_s))
```

```python
import functools
import math

import numpy as np
import jax
import jax.numpy as jnp
from jax import lax
from jax.experimental import pallas as pl
from jax.experimental.pallas import tpu as pltpu

D_MODEL = 1024
BATCH = 8
SEQ = 2048
DEPTH = 2
DEC_BATCH = 128
CHUNK = 128
D_CM = D_MODEL
CM_GROUPS = 8
D_INNER = 2048
HEAD_DIM = 64
HEADS = 32
SSD_GROUPS = 4
HEADS_PER_GROUP = HEADS // SSD_GROUPS
STATE = 128
CONV_K = 4
CONV_DIM = D_INNER + 2 * SSD_GROUPS * STATE
D_FF = 2816
ALPHA = (2 * DEPTH) ** 0.25
LN_EPS = 1e-5
REF_DT = 2 * D_CM + D_INNER + CONV_DIM
REF_GATES = REF_DT + HEADS

LANES = 128
DT_COPIES = 3
C_U = 0
C_V = C_U + D_CM
C_Z = C_V + D_CM
C_XBC = C_Z + D_INNER
C_END = C_XBC + CONV_DIM

TILE_L = 256
TILE_F = 512
SSD_BB = 8
VMEM_LIMIT = 56 * 1024 * 1024

F32 = jnp.float32
BF16 = jnp.bfloat16


def _bdot(a, b):
    return jnp.dot(a, b, preferred_element_type=F32)


def _gelu(x):
    return 0.5 * x * (1.0 + lax.erf(x * np.float32(math.sqrt(0.5))))


def _sigmoid(x):
    return 1.0 / (1.0 + jnp.exp(-x))


def _silu(x):
    return x * _sigmoid(x)


def _softplus(x):
    return jnp.maximum(x, 0.0) + jnp.log(1.0 + jnp.exp(-jnp.abs(x)))


def _layer_norm(x, g, b):
    mu = jnp.mean(x, axis=-1, keepdims=True)
    xc = x - mu
    var = jnp.mean(xc * xc, axis=-1, keepdims=True)
    return xc * lax.rsqrt(var + LN_EPS) * g + b


def _rms_norm(x, g):
    return x * lax.rsqrt(jnp.mean(x * x, axis=-1, keepdims=True) + LN_EPS) * g


def _pack3(x):
    hi = x.astype(BF16).astype(F32)
    r1 = x - hi
    mid = r1.astype(BF16).astype(F32)
    r2 = r1 - mid
    lane = lax.broadcasted_iota(jnp.int32, x.shape, 1)
    return jnp.where(lane < HEADS, x, jnp.where(lane < 2 * HEADS, r1, r2)).astype(BF16)


def _const_spec(shape, single=True):
    nd = len(shape)
    kw = {"pipeline_mode": pl.Buffered(1)} if single else {}
    return pl.BlockSpec(shape, lambda *_: (0,) * nd, **kw)


def _layer_spec(shape, layer):
    nd = len(shape)
    return pl.BlockSpec((None,) + tuple(shape), lambda *_: (layer,) + (0,) * nd,
                        pipeline_mode=pl.Buffered(1))


def _cast_kernel(w_ref, o_ref):
    o_ref[...] = w_ref[...].astype(BF16)


def _to_bf16(w, rows):
    depth, r, c = w.shape
    return pl.pallas_call(
        _cast_kernel,
        grid=(depth, r // rows),
        in_specs=[pl.BlockSpec((None, rows, c), lambda l, i: (l, i, 0))],
        out_specs=pl.BlockSpec((None, rows, c), lambda l, i: (l, i, 0)),
        out_shape=jax.ShapeDtypeStruct((depth, r, c), BF16),
        compiler_params=pltpu.CompilerParams(
            dimension_semantics=("parallel", "parallel"), vmem_limit_bytes=VMEM_LIMIT),
        name="cast_bf16",
    )(w)


def _tcast_kernel(w_ref, o_ref):
    o_ref[...] = w_ref[...].T.astype(BF16)


def _tcast_dt_kernel(w_ref, o_ref):
    t = w_ref[...].T
    lane = lax.broadcasted_iota(jnp.int32, t.shape, 1)
    o_ref[...] = jnp.where(lane < HEADS, t, 0.0).astype(BF16)


def _win_prep(w_in, rows=512):
    wt = jnp.swapaxes(w_in, 1, 2)
    params = pltpu.CompilerParams(dimension_semantics=("parallel", "parallel"), vmem_limit_bytes=VMEM_LIMIT)

    def piece(kern, n_out, in_spec, nblk, blk):
        return pl.pallas_call(
            kern, grid=(DEPTH, nblk), in_specs=[in_spec],
            out_specs=pl.BlockSpec((None, D_MODEL, blk), lambda l, i: (l, 0, i)),
            out_shape=jax.ShapeDtypeStruct((DEPTH, D_MODEL, n_out), BF16),
            compiler_params=params, name="w_in_prep")(wt)

    main = piece(_tcast_kernel, C_END, pl.BlockSpec((None, rows, D_MODEL), lambda l, i: (l, i, 0)),
                 C_END // rows, rows)
    gates = piece(lambda w_ref, o_ref: _tcast_kernel(w_ref.at[0], o_ref), 2 * D_MODEL,
                  pl.BlockSpec((pl.Element(1), pl.Element(rows), pl.Element(D_MODEL)),
                               lambda l, i: (l, pl.multiple_of(REF_GATES + i * rows, 8), 0)),
                  2 * D_MODEL // rows, rows)
    dt = piece(_tcast_dt_kernel, LANES,
               pl.BlockSpec((None, LANES, D_MODEL), lambda l, i: (l, REF_DT // LANES, 0)), 1, LANES)
    return {"w_in": main, "w_gates": gates, "w_dt": dt}


def _dt_cols(h, wdt_ref):
    d = _bdot(h, wdt_ref[...])
    out = d
    for k in range(1, DT_COPIES):
        out = out + pltpu.roll(d, k * HEADS, 1)
    return out


def _mod_kernel(c_ref, w_ref, b_ref, os_ref, op_ref):
    s = _silu(c_ref[...])
    res = _bdot(s.astype(BF16), w_ref[...].astype(BF16)) + b_ref[...]
    os_ref[...] = res[:DEC_BATCH]
    op_ref[...] = res[DEC_BATCH:]


def _adaln_mod(c_all, w_ada, b_ada):
    rows = c_all.shape[0]
    nblk = 6
    return pl.pallas_call(
        _mod_kernel,
        grid=(DEPTH, nblk),
        in_specs=[
            pl.BlockSpec((rows, D_MODEL), lambda l, k: (0, 0)),
            pl.BlockSpec((None, D_MODEL, D_MODEL), lambda l, k: (l, 0, k)),
            pl.BlockSpec((None, 1, D_MODEL), lambda l, k: (l, 0, k)),
        ],
        out_specs=[pl.BlockSpec((None, DEC_BATCH, D_MODEL), lambda l, k: (l, 0, k)),
                   pl.BlockSpec((None, BATCH, D_MODEL), lambda l, k: (l, 0, k))],
        out_shape=[jax.ShapeDtypeStruct((DEPTH, DEC_BATCH, nblk * D_MODEL), F32),
                   jax.ShapeDtypeStruct((DEPTH, BATCH, nblk * D_MODEL), F32)],
        compiler_params=pltpu.CompilerParams(
            dimension_semantics=("parallel", "parallel"), vmem_limit_bytes=VMEM_LIMIT),
        name="adaln_mod",
    )(c_all, w_ada, b_ada.reshape(DEPTH, 1, nblk * D_MODEL))


def _cm_branch(u_v, lnvg, lnvb):
    u = _gelu(u_v[:, :D_CM])
    v = _layer_norm(_gelu(u_v[:, D_CM:]), lnvg, lnvb)
    return u, v


def _fold(v):
    s8 = jnp.sum(v.reshape(v.shape[0] // 8, 8, v.shape[1]), axis=0)
    d = s8[:, 0:LANES]
    for k in range(1, v.shape[1] // LANES):
        d = d + s8[:, k * LANES:(k + 1) * LANES]
    return d[0:1, 0:1]


def _after(never, dep, val):
    return jnp.where(never, jnp.broadcast_to(dep, val.shape).astype(val.dtype), val)


def _post_mixer(x, g1, merged_bf16, wo_ref, ln1g, ln1b):
    o = _bdot(merged_bf16, wo_ref[...])
    return _layer_norm(ALPHA * x + g1 * o, ln1g, ln1b)


def _ffn(x1, sh2, sc2, g2, wg_ref, wu_ref, wd_ref, ln2g, ln2b):
    h2 = (x1 * (1.0 + sc2) + sh2).astype(BF16)
    gate = _bdot(h2, wg_ref[...])
    up = _bdot(h2, wu_ref[...])
    act = (_silu(gate) * up).astype(BF16)
    ffn = _bdot(act, wd_ref[...])
    return _layer_norm(ALPHA * x1 + g2 * ffn, ln2g, ln2b)


def _ssd_chunk(xs_c, b_c, c_c, dtseg, dtb, alog, tri, e64, state_ref, y_ref, r0):
    row = lax.broadcasted_iota(jnp.int32, (CHUNK, LANES), 0)
    lane = lax.broadcasted_iota(jnp.int32, (CHUNK, LANES), 1)
    tril = row >= lane
    low_half = lane < HEAD_DIM

    dt = _softplus(dtseg + dtb)
    a = -jnp.exp(alog)
    da = dt * a
    hi = da.astype(BF16)
    r1 = da - hi.astype(F32)
    mid = r1.astype(BF16)
    lo = (r1 - mid.astype(F32)).astype(BF16)
    acum = _bdot(tri, hi) + _bdot(tri, mid) + _bdot(tri, lo)
    shift_t = acum.T - jnp.log(dt.T)
    alast = acum[CHUNK - 1:CHUNK, :]
    w = dt * jnp.exp(alast - acum)
    wexp = _bdot(_pack3(w), e64)
    eexp = _bdot(_pack3(jnp.exp(acum)), e64)
    cd = jnp.broadcast_to(jnp.exp(alast), (8, LANES))
    cdexp = _bdot(_pack3(cd), e64)[0:1, :]
    xs_b = xs_c.astype(BF16)
    zero_b = jnp.zeros((CHUNK, LANES), BF16)

    gw = HEADS_PER_GROUP * HEAD_DIM
    for g in range(SSD_GROUPS):
        gc = g * gw
        bg = b_c[:, g * STATE:(g + 1) * STATE]
        cgb = c_c[:, g * STATE:(g + 1) * STATE].astype(BF16)
        cb = lax.dot_general(cgb, bg.astype(BF16), (((1,), (1,)), ((), ())),
                             preferred_element_type=F32)
        y_off = eexp[:, gc:gc + gw] * _bdot(cgb, state_ref[:, gc:gc + gw].astype(BF16))
        for pr in range(HEADS_PER_GROUP // 2):
            h0 = g * HEADS_PER_GROUP + 2 * pr
            pieces = []
            for h in (h0, h0 + 1):
                seg = jnp.broadcast_to(acum[:, h:h + 1], (CHUNK, LANES)) - shift_t[h:h + 1, :]
                pieces.append((cb * jnp.exp(jnp.where(tril, seg, -jnp.inf))).astype(BF16))
            lhs = jnp.concatenate(pieces, axis=1)
            pc = h0 * HEAD_DIM
            xs_pair = xs_b[:, pc:pc + LANES]
            rhs = jnp.concatenate([jnp.where(low_half, xs_pair, zero_b),
                                   jnp.where(low_half, zero_b, xs_pair)], axis=0)
            y_ref[r0:r0 + CHUNK, pc:pc + LANES] = (
                _bdot(lhs, rhs) + y_off[:, pc - gc:pc - gc + LANES])
        xw = (xs_c[:, gc:gc + gw] * wexp[:, gc:gc + gw]).astype(BF16)
        upd = _bdot(bg.T.astype(BF16), xw)
        state_ref[:, gc:gc + gw] = state_ref[:, gc:gc + gw] * cdexp[:, gc:gc + gw] + upd


def _p1_kernel(x_ref, mod_ref, win_ref, wgate_ref, wdt_ref, bg_ref, lnvg_ref, lnvb_ref, wsp_ref, bsp_ref,
               convw_ref, convb_ref, dtb_ref, alog_ref, dskip_ref, normw_ref,
               wcm_ref, wssd_ref, wo_ref, ln1g_ref, ln1b_ref, tri_ref, e64_ref,
               x1_ref, ssd_ref, conv_ref,
               state_ref, cbuf_ref, y_ref, *, tl):
    j = pl.program_id(1)
    last = pl.num_programs(1) - 1
    nc = tl // CHUNK

    @pl.when(j == 0)
    def _():
        state_ref[...] = jnp.zeros_like(state_ref)
        cbuf_ref[0:8, :] = jnp.zeros((8, CONV_DIM), F32)

    x = x_ref[...]
    sh1 = mod_ref[0:1, :]
    sc1 = mod_ref[1:2, :]
    g1 = mod_ref[2:3, :]
    h = (x * (1.0 + sc1) + sh1).astype(BF16)

    parts = []
    for c in range(nc):
        r0 = c * CHUNK
        cbuf_ref[8 + r0:8 + r0 + CHUNK, :] = _bdot(h[r0:r0 + CHUNK], win_ref[:, C_XBC:C_END])
        acc = convb_ref[...]
        for k in range(CONV_K):
            acc = acc + cbuf_ref[5 + k + r0:5 + k + r0 + CHUNK, :] * convw_ref[k:k + 1, :]
        parts.append(_silu(acc))
    cbuf_ref[5:8, :] = cbuf_ref[tl + 5:tl + 8, :]
    xbc_c = jnp.concatenate(parts, axis=0) if nc > 1 else parts[0]
    xs = xbc_c[:, :D_INNER]
    dtseg = _dt_cols(h, wdt_ref)
    u_v = _bdot(h, win_ref[:, C_U:C_Z])
    z = _bdot(h, win_ref[:, C_Z:C_XBC])
    gcm_logit = _bdot(h, wgate_ref[:, :D_MODEL])
    gssd_logit = _bdot(h, wgate_ref[:, D_MODEL:])

    u, v = _cm_branch(u_v, lnvg_ref[...], lnvb_ref[...])
    vb = v.astype(BF16)
    row = lax.broadcasted_iota(jnp.int32, (CHUNK, CHUNK), 0)
    col = lax.broadcasted_iota(jnp.int32, (CHUNK, CHUNK), 1)
    wmix = [jnp.where(row >= col, wsp_ref[g], 0.0).astype(BF16) for g in range(CM_GROUPS)]

    rows = []
    for c in range(nc):
        cols = []
        for g in range(CM_GROUPS):
            cols.append(_bdot(wmix[g], vb[c * CHUNK:(c + 1) * CHUNK, g * LANES:(g + 1) * LANES]))
        rows.append(jnp.concatenate(cols, axis=1) + bsp_ref[...])
    mix = jnp.concatenate(rows, axis=0) if nc > 1 else rows[0]
    out_cm = (u * mix).astype(BF16)
    merged = _sigmoid(gcm_logit + bg_ref[:, :D_MODEL]) * _bdot(out_cm, wcm_ref[...])

    side_done = _fold(merged) + _fold(gssd_logit)
    for c in range(nc):
        r0 = c * CHUNK
        dtb = dtb_ref[...] if c == 0 else _after(pl.program_id(1) < 0, side_done, dtb_ref[...])
        _ssd_chunk(xs[r0:r0 + CHUNK], xbc_c[r0:r0 + CHUNK, D_INNER:D_INNER + SSD_GROUPS * STATE],
                   xbc_c[r0:r0 + CHUNK, D_INNER + SSD_GROUPS * STATE:], dtseg[r0:r0 + CHUNK],
                   dtb, alog_ref[...], tri_ref[...], e64_ref[...],
                   state_ref, y_ref, r0)

    yz = (y_ref[...] + xs * dskip_ref[...]) * _silu(z)
    yn = _rms_norm(yz, normw_ref[...]).astype(BF16)
    merged = merged + _sigmoid(gssd_logit + bg_ref[:, D_MODEL:]) * _bdot(yn, wssd_ref[...])
    x1_ref[...] = _post_mixer(x, g1, merged.astype(BF16), wo_ref, ln1g_ref[...], ln1b_ref[...])

    @pl.when(j == last)
    def _():
        conv_ref[...] = cbuf_ref[5:8, :]
        ssd_ref[...] = state_ref[...].T


def _mod_prompt_spec(layer):
    return pl.BlockSpec((None, None, 6, D_MODEL), lambda b, j: (layer, b, 0, 0))


def _prompt_mixer(layer, x, mod4, P, consts):
    tl = TILE_L
    nj = SEQ // tl
    ls = lambda *shape: _layer_spec(shape, layer)
    in_specs = [
        pl.BlockSpec((None, tl, D_MODEL), lambda b, j: (b, j, 0)),
        _mod_prompt_spec(layer),
        ls(D_MODEL, C_END), ls(D_MODEL, 2 * D_MODEL), ls(D_MODEL, LANES),
        ls(1, 2 * D_MODEL), ls(1, D_CM), ls(1, D_CM),
        ls(CM_GROUPS, CHUNK, CHUNK), ls(CHUNK, D_CM),
        ls(CONV_K, CONV_DIM), ls(1, CONV_DIM), ls(1, LANES), ls(1, LANES),
        ls(1, D_INNER), ls(1, D_INNER),
        ls(D_CM, D_MODEL), ls(D_INNER, D_MODEL), ls(D_MODEL, D_MODEL),
        ls(1, D_MODEL), ls(1, D_MODEL),
        _const_spec((CHUNK, CHUNK)), _const_spec((LANES, D_INNER)),
    ]
    out_specs = [
        pl.BlockSpec((None, tl, D_MODEL), lambda b, j: (b, j, 0)),
        pl.BlockSpec((None, D_INNER, STATE), lambda b, j: (b, 0, 0)),
        pl.BlockSpec((None, CONV_K - 1, CONV_DIM), lambda b, j: (b, 0, 0)),
    ]
    out_shape = [
        jax.ShapeDtypeStruct((BATCH, SEQ, D_MODEL), F32),
        jax.ShapeDtypeStruct((BATCH, D_INNER, STATE), F32),
        jax.ShapeDtypeStruct((BATCH, CONV_K - 1, CONV_DIM), F32),
    ]
    return pl.pallas_call(
        functools.partial(_p1_kernel, tl=tl),
        grid=(BATCH, nj),
        in_specs=in_specs, out_specs=out_specs, out_shape=out_shape,
        scratch_shapes=[pltpu.VMEM((STATE, D_INNER), F32),
                        pltpu.VMEM((tl + 8, CONV_DIM), F32),
                        pltpu.VMEM((tl, D_INNER), F32)],
        compiler_params=pltpu.CompilerParams(
            dimension_semantics=("parallel", "arbitrary"), vmem_limit_bytes=VMEM_LIMIT),
        name="prompt_mixer",
    )(x, mod4, P["w_in"], P["w_gates"], P["w_dt"], P["b_gate"], P["ln_v_g"], P["ln_v_b"], P["w_spatial"], P["bsp_full"],
      P["conv_w"], P["conv_b"], P["dt_bias3"], P["a_log3"], P["dskip_exp"], P["ssd_norm_w"],
      P["w_cm_br"], P["w_ssd_br"], P["w_o"], P["ln1_g"], P["ln1_b"],
      consts["tri"], consts["e64"])


def _p2_kernel(x_ref, mod_ref, wg_ref, wu_ref, wd_ref, ln2g_ref, ln2b_ref, o_ref):
    o_ref[...] = _ffn(x_ref[...], mod_ref[3:4, :], mod_ref[4:5, :], mod_ref[5:6, :],
                      wg_ref, wu_ref, wd_ref, ln2g_ref[...], ln2b_ref[...])


def _prompt_ffn(layer, x1, mod4, P):
    tf = TILE_F
    ls = lambda *shape: _layer_spec(shape, layer)
    return pl.pallas_call(
        _p2_kernel,
        grid=(BATCH, SEQ // tf),
        in_specs=[
            pl.BlockSpec((None, tf, D_MODEL), lambda b, j: (b, j, 0)),
            _mod_prompt_spec(layer),
            ls(D_MODEL, D_FF), ls(D_MODEL, D_FF), ls(D_FF, D_MODEL),
            ls(1, D_MODEL), ls(1, D_MODEL),
        ],
        out_specs=pl.BlockSpec((None, tf, D_MODEL), lambda b, j: (b, j, 0)),
        out_shape=jax.ShapeDtypeStruct((BATCH, SEQ, D_MODEL), F32),
        compiler_params=pltpu.CompilerParams(
            dimension_semantics=("parallel", "parallel"), vmem_limit_bytes=VMEM_LIMIT),
        name="prompt_ffn",
    )(x1, mod4, P["w_ffn_gate"], P["w_ffn_up"], P["w_ffn_down"], P["ln2_g"], P["ln2_b"])


def _s_pre_kernel(x_ref, mod_ref, win_ref, wgate_ref, wdt_ref, bg_ref, lnvg_ref, lnvb_ref, wsp0_ref, bsp0_ref,
                  convw_ref, convb_ref, cstate_ref, dtb_ref, alog_ref, dskip_ref, wcm_ref, e64_ref,
                  v_ref, conv_ref, da_ref, xdtt_ref, b_ref, c_ref, cmpart_ref, zs_ref, gs_ref, xsd_ref):
    x = x_ref[...]
    sh1 = mod_ref[:, 0:D_MODEL]
    sc1 = mod_ref[:, D_MODEL:2 * D_MODEL]
    h = (x * (1.0 + sc1) + sh1).astype(BF16)

    u, v = _cm_branch(_bdot(h, win_ref[:, C_U:C_Z]), lnvg_ref[...], lnvb_ref[...])
    v_ref[...] = v
    out_cm = (u * (wsp0_ref[...] * v + bsp0_ref[...])).astype(BF16)
    gate_cm = _sigmoid(_bdot(h, wgate_ref[:, :D_MODEL]) + bg_ref[:, :D_MODEL])
    cmpart_ref[...] = gate_cm * _bdot(out_cm, wcm_ref[...])

    xbc = _bdot(h, win_ref[:, C_XBC:C_END])
    acc = convb_ref[...]
    for k in range(CONV_K - 1):
        acc = acc + cstate_ref[:, k * CONV_DIM:(k + 1) * CONV_DIM] * convw_ref[k:k + 1, :]
    acc = acc + xbc * convw_ref[CONV_K - 1:CONV_K, :]
    conv_ref[:, 0:(CONV_K - 2) * CONV_DIM] = cstate_ref[:, CONV_DIM:(CONV_K - 1) * CONV_DIM]
    conv_ref[:, (CONV_K - 2) * CONV_DIM:] = xbc
    xbc_c = _silu(acc)
    xs = xbc_c[:, :D_INNER]
    b_ref[...] = xbc_c[:, D_INNER:D_INNER + SSD_GROUPS * STATE]
    c_ref[...] = xbc_c[:, D_INNER + SSD_GROUPS * STATE:]
    dt = _softplus(_dt_cols(h, wdt_ref) + dtb_ref[...])
    da_ref[...] = jnp.exp(dt * (-jnp.exp(alog_ref[...])))
    xdt = xs * _bdot(_pack3(dt), e64_ref[...])
    xdtt_ref[...] = xdt.T.astype(BF16)
    xsd_ref[...] = xs * dskip_ref[...]
    zs_ref[...] = _silu(_bdot(h, win_ref[:, C_Z:C_XBC]))
    gs_ref[...] = _sigmoid(_bdot(h, wgate_ref[:, D_MODEL:]) + bg_ref[:, D_MODEL:])


def _sample_pre(layer, x, mod, cstate, P, consts):
    n = DEC_BATCH
    ls = lambda *shape: _layer_spec(shape, layer)
    in_specs = [
        _const_spec((n, D_MODEL)), ls(n, 6 * D_MODEL),
        ls(D_MODEL, C_END), ls(D_MODEL, 2 * D_MODEL), ls(D_MODEL, LANES),
        ls(1, 2 * D_MODEL), ls(1, D_CM), ls(1, D_CM), ls(1, D_CM), ls(1, D_CM),
        ls(CONV_K, CONV_DIM), ls(1, CONV_DIM), ls(n, (CONV_K - 1) * CONV_DIM),
        ls(1, LANES), ls(1, LANES), ls(1, D_INNER), ls(D_CM, D_MODEL), _const_spec((LANES, D_INNER)),
    ]
    outs = [((n, D_CM), F32), ((n, (CONV_K - 1) * CONV_DIM), F32), ((n, LANES), F32),
            ((D_INNER, n), BF16), ((n, SSD_GROUPS * STATE), F32), ((n, SSD_GROUPS * STATE), F32),
            ((n, D_MODEL), F32), ((n, D_INNER), F32), ((n, D_MODEL), F32), ((n, D_INNER), F32)]
    return pl.pallas_call(
        _s_pre_kernel,
        grid=(1,),
        in_specs=in_specs,
        out_specs=[_const_spec(s, single=False) for s, _ in outs],
        out_shape=[jax.ShapeDtypeStruct(s, d) for s, d in outs],
        compiler_params=pltpu.CompilerParams(
            dimension_semantics=("arbitrary",), vmem_limit_bytes=VMEM_LIMIT),
        name="sample_pre",
    )(x, mod, P["w_in"], P["w_gates"], P["w_dt"], P["b_gate"], P["ln_v_g"], P["ln_v_b"], P["wsp0"], P["bsp0"],
      P["conv_w"], P["conv_b"], cstate, P["dt_bias3"], P["a_log3"], P["dskip_exp"], P["w_cm_br"],
      consts["e64"])


def _s_ssd_kernel(da_ref, xdtt_ref, b_ref, c_ref, st_ref, *rest):
    sto_ref, y_ref = rest[-2:]
    i = pl.program_id(0)

    @pl.when(i == 0)
    def _():
        y_ref[...] = jnp.zeros_like(y_ref)

    rows = lax.broadcasted_iota(jnp.int32, (DEC_BATCH, STATE), 0)
    rows8 = lax.broadcasted_iota(jnp.int32, (8, STATE), 0)
    gw = HEADS_PER_GROUP * HEAD_DIM
    for q in range(SSD_BB):
        b = i * SSD_BB + q
        b8 = pl.multiple_of((b // 8) * 8, 8)
        for g in range(SSD_GROUPS):
            bg = jnp.where(rows == b, b_ref[:, g * STATE:(g + 1) * STATE], 0.0).astype(BF16)
            xb = _bdot(xdtt_ref[g * gw:(g + 1) * gw, :], bg)
            blocks = []
            for hh in range(HEADS_PER_GROUP):
                hd = g * HEADS_PER_GROUP + hh
                r = hd * HEAD_DIM
                blk = st_ref[q, r:r + HEAD_DIM, :] * da_ref[b, hd] + xb[hh * HEAD_DIM:(hh + 1) * HEAD_DIM, :]
                sto_ref[q, r:r + HEAD_DIM, :] = blk
                blocks.append(blk.astype(BF16))
            newg = jnp.concatenate(blocks, axis=0)
            cg = jnp.where(rows8 == b - b8, c_ref[pl.ds(b8, 8), g * STATE:(g + 1) * STATE], 0.0).astype(BF16)
            yg = lax.dot_general(cg, newg, (((1,), (1,)), ((), ())), preferred_element_type=F32)
            y_ref[pl.ds(b8, 8), g * gw:(g + 1) * gw] += yg


def _sample_ssd(layer, da, xdtt, bm, cm, state, stacked):
    n = DEC_BATCH
    in_specs = [
        pl.BlockSpec((D_INNER, n), lambda i, da: (0, 0)),
        pl.BlockSpec((n, SSD_GROUPS * STATE), lambda i, da: (0, 0)),
        pl.BlockSpec((n, SSD_GROUPS * STATE), lambda i, da: (0, 0)),
        pl.BlockSpec((None, SSD_BB, D_INNER, STATE), lambda i, da: (layer, i, 0, 0)),
    ]
    args = [da, xdtt, bm, cm, state]
    aliases = {}
    if stacked is not None:
        in_specs.append(pl.BlockSpec(memory_space=pl.ANY))
        args.append(stacked)
        aliases = {len(args) - 1: 0}
    grid_spec = pltpu.PrefetchScalarGridSpec(
        num_scalar_prefetch=1,
        grid=(n // SSD_BB,),
        in_specs=in_specs,
        out_specs=[
            pl.BlockSpec((None, SSD_BB, D_INNER, STATE), lambda i, da: (layer, i, 0, 0)),
            pl.BlockSpec((n, D_INNER), lambda i, da: (0, 0)),
        ],
    )
    return pl.pallas_call(
        _s_ssd_kernel,
        grid_spec=grid_spec,
        out_shape=[jax.ShapeDtypeStruct((DEPTH, n, D_INNER, STATE), F32),
                   jax.ShapeDtypeStruct((n, D_INNER), F32)],
        input_output_aliases=aliases,
        compiler_params=pltpu.CompilerParams(
            dimension_semantics=("arbitrary",), vmem_limit_bytes=VMEM_LIMIT),
        name="sample_ssd",
    )(*args)


def _s_post_kernel(x_ref, mod_ref, y_ref, xsd_ref, zs_ref, gs_ref, cmpart_ref, normw_ref,
                   wssd_ref, wo_ref, ln1g_ref, ln1b_ref, wg_ref, wu_ref, wd_ref, ln2g_ref, ln2b_ref, o_ref):
    md = lambda k: mod_ref[:, k * D_MODEL:(k + 1) * D_MODEL]
    yn = _rms_norm((y_ref[...] + xsd_ref[...]) * zs_ref[...], normw_ref[...]).astype(BF16)
    merged = cmpart_ref[...] + gs_ref[...] * _bdot(yn, wssd_ref[...])
    x1 = _post_mixer(x_ref[...], md(2), merged.astype(BF16), wo_ref, ln1g_ref[...], ln1b_ref[...])
    o_ref[...] = _ffn(x1, md(3), md(4), md(5), wg_ref, wu_ref, wd_ref, ln2g_ref[...], ln2b_ref[...])


def _sample_post(layer, x, mod, y, xsd, zs, gs, cmpart, P):
    n = DEC_BATCH
    ls = lambda *shape: _layer_spec(shape, layer)
    cs = _const_spec
    in_specs = [
        cs((n, D_MODEL)), ls(n, 6 * D_MODEL), cs((n, D_INNER)), cs((n, D_INNER)), cs((n, D_INNER)),
        cs((n, D_MODEL)), cs((n, D_MODEL)), ls(1, D_INNER), ls(D_INNER, D_MODEL), ls(D_MODEL, D_MODEL),
        ls(1, D_MODEL), ls(1, D_MODEL), ls(D_MODEL, D_FF), ls(D_MODEL, D_FF), ls(D_FF, D_MODEL),
        ls(1, D_MODEL), ls(1, D_MODEL),
    ]
    return pl.pallas_call(
        _s_post_kernel,
        grid=(1,),
        in_specs=in_specs,
        out_specs=_const_spec((n, D_MODEL), single=False),
        out_shape=jax.ShapeDtypeStruct((n, D_MODEL), F32),
        compiler_params=pltpu.CompilerParams(
            dimension_semantics=("arbitrary",), vmem_limit_bytes=VMEM_LIMIT),
        name="sample_post",
    )(x, mod, y, xsd, zs, gs, cmpart, P["ssd_norm_w"], P["w_ssd_br"], P["w_o"], P["ln1_g"], P["ln1_b"],
      P["w_ffn_gate"], P["w_ffn_up"], P["w_ffn_down"], P["ln2_g"], P["ln2_b"])


def _constants():
    tri = np.tril(np.ones((CHUNK, CHUNK), np.float32))
    e64 = np.zeros((LANES, D_INNER), np.float32)
    for r in range(DT_COPIES * HEADS):
        hd = r % HEADS
        e64[r, hd * HEAD_DIM:(hd + 1) * HEAD_DIM] = 1.0
    return {"tri": jnp.asarray(tri, BF16), "e64": jnp.asarray(e64, BF16)}


def _params(p):
    rows = lambda a: a[:, None, :]
    pad = jnp.zeros((DEPTH, LANES - DT_COPIES * HEADS), F32)
    return {
        **_win_prep(p["w_in"]),
        "b_gate": rows(p["b_gate"]),
        "ln_v_g": rows(p["ln_v_g"]), "ln_v_b": rows(p["ln_v_b"]),
        "w_spatial": p["w_spatial"],
        "bsp_full": jnp.repeat(jnp.swapaxes(p["b_spatial"], 1, 2), LANES, axis=2),
        "wsp0": rows(jnp.repeat(p["w_spatial"][:, :, 0, 0], LANES, axis=1)),
        "bsp0": rows(jnp.repeat(p["b_spatial"][:, :, 0], LANES, axis=1)),
        "conv_w": p["conv_w"], "conv_b": rows(p["conv_b"]),
        "dt_bias3": rows(jnp.concatenate([p["dt_bias"]] * DT_COPIES + [pad], axis=1)),
        "a_log3": rows(jnp.concatenate([p["a_log"]] * DT_COPIES + [pad], axis=1)),
        "dskip_exp": rows(jnp.repeat(p["d_skip"], HEAD_DIM, axis=1)),
        "ssd_norm_w": rows(p["ssd_norm_w"]),
        "w_cm_br": _to_bf16(p["w_cm_br"], 512), "w_ssd_br": _to_bf16(p["w_ssd_br"], 512),
        "w_o": _to_bf16(p["w_o"], 512),
        "ln1_g": rows(p["ln1_g"]), "ln1_b": rows(p["ln1_b"]),
        "w_ffn_gate": _to_bf16(p["w_ffn_gate"], 512), "w_ffn_up": _to_bf16(p["w_ffn_up"], 512),
        "w_ffn_down": _to_bf16(p["w_ffn_down"], D_FF // 4),
        "ln2_g": rows(p["ln2_g"]), "ln2_b": rows(p["ln2_b"]),
    }


def kernel(x_prompt, x_sample, state_ssd, state_conv, c_prompt, c_sample, w_ada, b_ada, w_in, b_gate,
           ln_v_g, ln_v_b, w_spatial, b_spatial, conv_w, conv_b, dt_bias, a_log, d_skip, ssd_norm_w,
           w_cm_br, w_ssd_br, w_o, ln1_g, ln1_b, w_ffn_gate, w_ffn_up, w_ffn_down, ln2_g, ln2_b):
    P = _params(dict(w_in=w_in, b_gate=b_gate, ln_v_g=ln_v_g, ln_v_b=ln_v_b, w_spatial=w_spatial,
                     b_spatial=b_spatial, conv_w=conv_w, conv_b=conv_b, dt_bias=dt_bias, a_log=a_log,
                     d_skip=d_skip, ssd_norm_w=ssd_norm_w, w_cm_br=w_cm_br, w_ssd_br=w_ssd_br, w_o=w_o,
                     ln1_g=ln1_g, ln1_b=ln1_b, w_ffn_gate=w_ffn_gate, w_ffn_up=w_ffn_up,
                     w_ffn_down=w_ffn_down, ln2_g=ln2_g, ln2_b=ln2_b))
    consts = _constants()
    mod, mod_p = _adaln_mod(jnp.concatenate([c_sample, c_prompt], axis=0), w_ada, b_ada)
    mod4 = mod_p.reshape(DEPTH, BATCH, 6, D_MODEL)
    xp = x_prompt
    xs = x_sample.reshape(DEC_BATCH, D_MODEL)
    ssd_p, conv_p, conv_s, v_s = [], [], [], []
    state_in = state_ssd.reshape(DEPTH, DEC_BATCH, D_INNER, STATE)
    cstate = state_conv.reshape(DEPTH, DEC_BATCH, (CONV_K - 1) * CONV_DIM)
    ssd_s = None
    for i in range(DEPTH):
        x1, sp, cp = _prompt_mixer(i, xp, mod4, P, consts)
        xp = _prompt_ffn(i, x1, mod4, P)
        ssd_p.append(sp.reshape(BATCH, HEADS, HEAD_DIM, STATE))
        conv_p.append(cp)
        v, cs, da, xdtt, bm, cm, cmpart, zs, gs, xsd = _sample_pre(i, xs, mod, cstate, P, consts)
        ssd_s, y = _sample_ssd(i, da[:, :HEADS], xdtt, bm, cm, state_in, ssd_s)
        xs = _sample_post(i, xs, mod, y, xsd, zs, gs, cmpart, P)
        conv_s.append(cs.reshape(DEC_BATCH, CONV_K - 1, CONV_DIM))
        v_s.append(v.reshape(DEC_BATCH, 1, D_CM))
    return (xp, xs.reshape(DEC_BATCH, 1, D_MODEL), jnp.stack(ssd_p), jnp.stack(conv_p),
            ssd_s.reshape(DEPTH, DEC_BATCH, HEADS, HEAD_DIM, STATE), jnp.stack(conv_s), jnp.stack(v_s))
```

```python
import functools
import math

import numpy as np
import jax
import jax.numpy as jnp
from jax import lax
from jax.experimental import pallas as pl
from jax.experimental.pallas import tpu as pltpu

D_MODEL = 1024
BATCH = 8
SEQ = 2048
DEPTH = 2
DEC_BATCH = 128
CHUNK = 128
D_CM = D_MODEL
CM_GROUPS = 8
D_INNER = 2048
HEAD_DIM = 64
HEADS = 32
SSD_GROUPS = 4
HEADS_PER_GROUP = HEADS // SSD_GROUPS
STATE = 128
CONV_K = 4
CONV_DIM = D_INNER + 2 * SSD_GROUPS * STATE
D_FF = 2816
ALPHA = (2 * DEPTH) ** 0.25
LN_EPS = 1e-5
REF_DT = 2 * D_CM + D_INNER + CONV_DIM
REF_GATES = REF_DT + HEADS

LANES = 128
DT_COPIES = 3
C_U = 0
C_V = C_U + D_CM
C_Z = C_V + D_CM
C_XBC = C_Z + D_INNER
C_END = C_XBC + CONV_DIM

TILE_L = 256
TILE_F = 512
SSD_BB = 8
VMEM_LIMIT = 56 * 1024 * 1024

F32 = jnp.float32
BF16 = jnp.bfloat16


def _bdot(a, b):
    return jnp.dot(a, b, preferred_element_type=F32)


def _gelu(x):
    return 0.5 * x * (1.0 + lax.erf(x * np.float32(math.sqrt(0.5))))


def _sigmoid(x):
    return 1.0 / (1.0 + jnp.exp(-x))


def _silu(x):
    return x * _sigmoid(x)


def _softplus(x):
    return jnp.maximum(x, 0.0) + jnp.log(1.0 + jnp.exp(-jnp.abs(x)))


def _layer_norm(x, g, b):
    mu = jnp.mean(x, axis=-1, keepdims=True)
    xc = x - mu
    var = jnp.mean(xc * xc, axis=-1, keepdims=True)
    return xc * lax.rsqrt(var + LN_EPS) * g + b


def _rms_norm(x, g):
    return x * lax.rsqrt(jnp.mean(x * x, axis=-1, keepdims=True) + LN_EPS) * g


def _pack3(x):
    hi = x.astype(BF16).astype(F32)
    r1 = x - hi
    mid = r1.astype(BF16).astype(F32)
    r2 = r1 - mid
    lane = lax.broadcasted_iota(jnp.int32, x.shape, 1)
    return jnp.where(lane < HEADS, x, jnp.where(lane < 2 * HEADS, r1, r2)).astype(BF16)


def _const_spec(shape, single=True):
    nd = len(shape)
    kw = {"pipeline_mode": pl.Buffered(1)} if single else {}
    return pl.BlockSpec(shape, lambda *_: (0,) * nd, **kw)


def _layer_spec(shape, layer):
    nd = len(shape)
    return pl.BlockSpec((None,) + tuple(shape), lambda *_: (layer,) + (0,) * nd,
                        pipeline_mode=pl.Buffered(1))


def _cast_kernel(w_ref, o_ref):
    o_ref[...] = w_ref[...].astype(BF16)


def _to_bf16(w, rows):
    depth, r, c = w.shape
    return pl.pallas_call(
        _cast_kernel,
        grid=(depth, r // rows),
        in_specs=[pl.BlockSpec((None, rows, c), lambda l, i: (l, i, 0))],
        out_specs=pl.BlockSpec((None, rows, c), lambda l, i: (l, i, 0)),
        out_shape=jax.ShapeDtypeStruct((depth, r, c), BF16),
        compiler_params=pltpu.CompilerParams(
            dimension_semantics=("parallel", "parallel"), vmem_limit_bytes=VMEM_LIMIT),
        name="cast_bf16",
    )(w)


def _tcast_kernel(w_ref, o_ref):
    o_ref[...] = w_ref[...].T.astype(BF16)


def _tcast_dt_kernel(w_ref, o_ref):
    t = w_ref[...].T
    lane = lax.broadcasted_iota(jnp.int32, t.shape, 1)
    o_ref[...] = jnp.where(lane < HEADS, t, 0.0).astype(BF16)


def _win_prep(w_in, rows=512):
    wt = jnp.swapaxes(w_in, 1, 2)
    params = pltpu.CompilerParams(dimension_semantics=("parallel", "parallel"), vmem_limit_bytes=VMEM_LIMIT)

    def piece(kern, n_out, in_spec, nblk, blk):
        return pl.pallas_call(
            kern, grid=(DEPTH, nblk), in_specs=[in_spec],
            out_specs=pl.BlockSpec((None, D_MODEL, blk), lambda l, i: (l, 0, i)),
            out_shape=jax.ShapeDtypeStruct((DEPTH, D_MODEL, n_out), BF16),
            compiler_params=params, name="w_in_prep")(wt)

    main = piece(_tcast_kernel, C_END, pl.BlockSpec((None, rows, D_MODEL), lambda l, i: (l, i, 0)),
                 C_END // rows, rows)
    gates = piece(lambda w_ref, o_ref: _tcast_kernel(w_ref.at[0], o_ref), 2 * D_MODEL,
                  pl.BlockSpec((pl.Element(1), pl.Element(rows), pl.Element(D_MODEL)),
                               lambda l, i: (l, pl.multiple_of(REF_GATES + i * rows, 8), 0)),
                  2 * D_MODEL // rows, rows)
    dt = piece(_tcast_dt_kernel, LANES,
               pl.BlockSpec((None, LANES, D_MODEL), lambda l, i: (l, REF_DT // LANES, 0)), 1, LANES)
    return {"w_in": main, "w_gates": gates, "w_dt": dt}


def _dt_cols(h, wdt_ref):
    d = _bdot(h, wdt_ref[...])
    out = d
    for k in range(1, DT_COPIES):
        out = out + pltpu.roll(d, k * HEADS, 1)
    return out


def _mod_kernel(c_ref, w_ref, b_ref, os_ref, op_ref):
    s = _silu(c_ref[...])
    res = _bdot(s.astype(BF16), w_ref[...].astype(BF16)) + b_ref[...]
    os_ref[...] = res[:DEC_BATCH]
    op_ref[...] = res[DEC_BATCH:]


def _adaln_mod(c_all, w_ada, b_ada):
    rows = c_all.shape[0]
    nblk = 6
    return pl.pallas_call(
        _mod_kernel,
        grid=(DEPTH, nblk),
        in_specs=[
            pl.BlockSpec((rows, D_MODEL), lambda l, k: (0, 0)),
            pl.BlockSpec((None, D_MODEL, D_MODEL), lambda l, k: (l, 0, k)),
            pl.BlockSpec((None, 1, D_MODEL), lambda l, k: (l, 0, k)),
        ],
        out_specs=[pl.BlockSpec((None, DEC_BATCH, D_MODEL), lambda l, k: (l, 0, k)),
                   pl.BlockSpec((None, BATCH, D_MODEL), lambda l, k: (l, 0, k))],
        out_shape=[jax.ShapeDtypeStruct((DEPTH, DEC_BATCH, nblk * D_MODEL), F32),
                   jax.ShapeDtypeStruct((DEPTH, BATCH, nblk * D_MODEL), F32)],
        compiler_params=pltpu.CompilerParams(
            dimension_semantics=("parallel", "parallel"), vmem_limit_bytes=VMEM_LIMIT),
        name="adaln_mod",
    )(c_all, w_ada, b_ada.reshape(DEPTH, 1, nblk * D_MODEL))


def _cm_branch(u_v, lnvg, lnvb):
    u = _gelu(u_v[:, :D_CM])
    v = _layer_norm(_gelu(u_v[:, D_CM:]), lnvg, lnvb)
    return u, v


def _fold(v):
    s8 = jnp.sum(v.reshape(v.shape[0] // 8, 8, v.shape[1]), axis=0)
    d = s8[:, 0:LANES]
    for k in range(1, v.shape[1] // LANES):
        d = d + s8[:, k * LANES:(k + 1) * LANES]
    return d[0:1, 0:1]


def _after(never, dep, val):
    return jnp.where(never, jnp.broadcast_to(dep, val.shape).astype(val.dtype), val)


def _post_mixer(x, g1, merged_bf16, wo_ref, ln1g, ln1b):
    o = _bdot(merged_bf16, wo_ref[...])
    return _layer_norm(ALPHA * x + g1 * o, ln1g, ln1b)


def _ffn(x1, sh2, sc2, g2, wg_ref, wu_ref, wd_ref, ln2g, ln2b):
    h2 = (x1 * (1.0 + sc2) + sh2).astype(BF16)
    gate = _bdot(h2, wg_ref[...])
    up = _bdot(h2, wu_ref[...])
    act = (_silu(gate) * up).astype(BF16)
    ffn = _bdot(act, wd_ref[...])
    return _layer_norm(ALPHA * x1 + g2 * ffn, ln2g, ln2b)


def _ssd_chunk(xs_c, b_c, c_c, dtseg, dtb, alog, tri, e64, cbias, state_ref, y_ref, r0):
    row = lax.broadcasted_iota(jnp.int32, (CHUNK, LANES), 0)
    lane = lax.broadcasted_iota(jnp.int32, (CHUNK, LANES), 1)
    low_half = lane < HEAD_DIM

    dt = _softplus(dtseg + dtb)
    a = -jnp.exp(alog)
    da = dt * a
    hi = da.astype(BF16)
    r1 = da - hi.astype(F32)
    mid = r1.astype(BF16)
    lo = (r1 - mid.astype(F32)).astype(BF16)
    acum = _bdot(tri, hi) + _bdot(tri, mid) + _bdot(tri, lo)
    shift_t = acum.T - jnp.log(dt.T)
    alast = acum[CHUNK - 1:CHUNK, :]
    w = dt * jnp.exp(alast - acum)
    wexp = _bdot(_pack3(w), e64)
    eexp = _bdot(_pack3(jnp.exp(acum)), e64)
    cd = jnp.broadcast_to(jnp.exp(alast), (8, LANES))
    cdexp = _bdot(_pack3(cd), e64)[0:1, :]
    xs_b = xs_c.astype(BF16)
    zero_b = jnp.zeros((CHUNK, LANES), BF16)

    gw = HEADS_PER_GROUP * HEAD_DIM
    for g in range(SSD_GROUPS):
        gc = g * gw
        bg = b_c[:, g * STATE:(g + 1) * STATE]
        cgb = c_c[:, g * STATE:(g + 1) * STATE].astype(BF16)
        cb = lax.dot_general(cgb, bg.astype(BF16), (((1,), (1,)), ((), ())),
                             preferred_element_type=F32)
        y_off = eexp[:, gc:gc + gw] * _bdot(cgb, state_ref[:, gc:gc + gw].astype(BF16))
        for pr in range(HEADS_PER_GROUP // 2):
            h0 = g * HEADS_PER_GROUP + 2 * pr
            pieces = []
            for h in (h0, h0 + 1):
                seg = jnp.broadcast_to(acum[:, h:h + 1], (CHUNK, LANES)) - shift_t[h:h + 1, :]
                pieces.append((cb * jnp.exp(seg + cbias)).astype(BF16))
            lhs = jnp.concatenate(pieces, axis=1)
            pc = h0 * HEAD_DIM
            xs_pair = xs_b[:, pc:pc + LANES]
            rhs = jnp.concatenate([jnp.where(low_half, xs_pair, zero_b),
                                   jnp.where(low_half, zero_b, xs_pair)], axis=0)
            y_ref[r0:r0 + CHUNK, pc:pc + LANES] = (
                _bdot(lhs, rhs) + y_off[:, pc - gc:pc - gc + LANES])
        xw = (xs_c[:, gc:gc + gw] * wexp[:, gc:gc + gw]).astype(BF16)
        upd = _bdot(bg.T.astype(BF16), xw)
        state_ref[:, gc:gc + gw] = state_ref[:, gc:gc + gw] * cdexp[:, gc:gc + gw] + upd


def _p1_kernel(x_ref, mod_ref, win_ref, wgate_ref, wdt_ref, bg_ref, lnvg_ref, lnvb_ref, wsp_ref, bsp_ref,
               convw_ref, convb_ref, dtb_ref, alog_ref, dskip_ref, normw_ref,
               wcm_ref, wssd_ref, wo_ref, ln1g_ref, ln1b_ref, tri_ref, e64_ref, cbias_ref,
               x1_ref, ssd_ref, conv_ref,
               state_ref, cbuf_ref, y_ref, *, tl):
    j = pl.program_id(1)
    last = pl.num_programs(1) - 1
    nc = tl // CHUNK

    @pl.when(j == 0)
    def _():
        state_ref[...] = jnp.zeros_like(state_ref)
        cbuf_ref[0:8, :] = jnp.zeros((8, CONV_DIM), F32)

    x = x_ref[...]
    sh1 = mod_ref[0:1, :]
    sc1 = mod_ref[1:2, :]
    g1 = mod_ref[2:3, :]
    h = (x * (1.0 + sc1) + sh1).astype(BF16)

    parts = []
    for c in range(nc):
        r0 = c * CHUNK
        cbuf_ref[8 + r0:8 + r0 + CHUNK, :] = _bdot(h[r0:r0 + CHUNK], win_ref[:, C_XBC:C_END])
        acc = convb_ref[...]
        for k in range(CONV_K):
            acc = acc + cbuf_ref[5 + k + r0:5 + k + r0 + CHUNK, :] * convw_ref[k:k + 1, :]
        parts.append(_silu(acc))
    cbuf_ref[5:8, :] = cbuf_ref[tl + 5:tl + 8, :]
    xbc_c = jnp.concatenate(parts, axis=0) if nc > 1 else parts[0]
    xs = xbc_c[:, :D_INNER]
    dtseg = _dt_cols(h, wdt_ref)
    u_v = _bdot(h, win_ref[:, C_U:C_Z])
    z = _bdot(h, win_ref[:, C_Z:C_XBC])
    gcm_logit = _bdot(h, wgate_ref[:, :D_MODEL])
    gssd_logit = _bdot(h, wgate_ref[:, D_MODEL:])

    u, v = _cm_branch(u_v, lnvg_ref[...], lnvb_ref[...])
    vb = v.astype(BF16)
    row = lax.broadcasted_iota(jnp.int32, (CHUNK, CHUNK), 0)
    col = lax.broadcasted_iota(jnp.int32, (CHUNK, CHUNK), 1)
    wmix = [jnp.where(row >= col, wsp_ref[g], 0.0).astype(BF16) for g in range(CM_GROUPS)]

    rows = []
    for c in range(nc):
        cols = []
        for g in range(CM_GROUPS):
            cols.append(_bdot(wmix[g], vb[c * CHUNK:(c + 1) * CHUNK, g * LANES:(g + 1) * LANES]))
        rows.append(jnp.concatenate(cols, axis=1) + bsp_ref[...])
    mix = jnp.concatenate(rows, axis=0) if nc > 1 else rows[0]
    out_cm = (u * mix).astype(BF16)
    merged = _sigmoid(gcm_logit + bg_ref[:, :D_MODEL]) * _bdot(out_cm, wcm_ref[...])

    side_done = _fold(merged) + _fold(gssd_logit)
    for c in range(nc):
        r0 = c * CHUNK
        dtb = dtb_ref[...] if c == 0 else _after(pl.program_id(1) < 0, side_done, dtb_ref[...])
        _ssd_chunk(xs[r0:r0 + CHUNK], xbc_c[r0:r0 + CHUNK, D_INNER:D_INNER + SSD_GROUPS * STATE],
                   xbc_c[r0:r0 + CHUNK, D_INNER + SSD_GROUPS * STATE:], dtseg[r0:r0 + CHUNK],
                   dtb, alog_ref[...], tri_ref[...], e64_ref[...], cbias_ref[...],
                   state_ref, y_ref, r0)

    yz = (y_ref[...] + xs * dskip_ref[...]) * _silu(z)
    yn = _rms_norm(yz, normw_ref[...]).astype(BF16)
    merged = merged + _sigmoid(gssd_logit + bg_ref[:, D_MODEL:]) * _bdot(yn, wssd_ref[...])
    x1_ref[...] = _post_mixer(x, g1, merged.astype(BF16), wo_ref, ln1g_ref[...], ln1b_ref[...])

    @pl.when(j == last)
    def _():
        conv_ref[...] = cbuf_ref[5:8, :]
        ssd_ref[...] = state_ref[...].T


def _mod_prompt_spec(layer):
    return pl.BlockSpec((None, None, 6, D_MODEL), lambda b, j: (layer, b, 0, 0))


def _prompt_mixer(layer, x, mod4, P, consts):
    tl = TILE_L
    nj = SEQ // tl
    ls = lambda *shape: _layer_spec(shape, layer)
    in_specs = [
        pl.BlockSpec((None, tl, D_MODEL), lambda b, j: (b, j, 0)),
        _mod_prompt_spec(layer),
        ls(D_MODEL, C_END), ls(D_MODEL, 2 * D_MODEL), ls(D_MODEL, LANES),
        ls(1, 2 * D_MODEL), ls(1, D_CM), ls(1, D_CM),
        ls(CM_GROUPS, CHUNK, CHUNK), ls(CHUNK, D_CM),
        ls(CONV_K, CONV_DIM), ls(1, CONV_DIM), ls(1, LANES), ls(1, LANES),
        ls(1, D_INNER), ls(1, D_INNER),
        ls(D_CM, D_MODEL), ls(D_INNER, D_MODEL), ls(D_MODEL, D_MODEL),
        ls(1, D_MODEL), ls(1, D_MODEL),
        _const_spec((CHUNK, CHUNK)), _const_spec((LANES, D_INNER)), _const_spec((CHUNK, CHUNK)),
    ]
    out_specs = [
        pl.BlockSpec((None, tl, D_MODEL), lambda b, j: (b, j, 0)),
        pl.BlockSpec((None, D_INNER, STATE), lambda b, j: (b, 0, 0)),
        pl.BlockSpec((None, CONV_K - 1, CONV_DIM), lambda b, j: (b, 0, 0)),
    ]
    out_shape = [
        jax.ShapeDtypeStruct((BATCH, SEQ, D_MODEL), F32),
        jax.ShapeDtypeStruct((BATCH, D_INNER, STATE), F32),
        jax.ShapeDtypeStruct((BATCH, CONV_K - 1, CONV_DIM), F32),
    ]
    return pl.pallas_call(
        functools.partial(_p1_kernel, tl=tl),
        grid=(BATCH, nj),
        in_specs=in_specs, out_specs=out_specs, out_shape=out_shape,
        scratch_shapes=[pltpu.VMEM((STATE, D_INNER), F32),
                        pltpu.VMEM((tl + 8, CONV_DIM), F32),
                        pltpu.VMEM((tl, D_INNER), F32)],
        compiler_params=pltpu.CompilerParams(
            dimension_semantics=("parallel", "arbitrary"), vmem_limit_bytes=VMEM_LIMIT),
        name="prompt_mixer",
    )(x, mod4, P["w_in"], P["w_gates"], P["w_dt"], P["b_gate"], P["ln_v_g"], P["ln_v_b"], P["w_spatial"], P["bsp_full"],
      P["conv_w"], P["conv_b"], P["dt_bias3"], P["a_log3"], P["dskip_exp"], P["ssd_norm_w"],
      P["w_cm_br"], P["w_ssd_br"], P["w_o"], P["ln1_g"], P["ln1_b"],
      consts["tri"], consts["e64"], consts["cbias"])


def _p2_kernel(x_ref, mod_ref, wg_ref, wu_ref, wd_ref, ln2g_ref, ln2b_ref, o_ref):
    o_ref[...] = _ffn(x_ref[...], mod_ref[3:4, :], mod_ref[4:5, :], mod_ref[5:6, :],
                      wg_ref, wu_ref, wd_ref, ln2g_ref[...], ln2b_ref[...])


def _prompt_ffn(layer, x1, mod4, P):
    tf = TILE_F
    ls = lambda *shape: _layer_spec(shape, layer)
    return pl.pallas_call(
        _p2_kernel,
        grid=(BATCH, SEQ // tf),
        in_specs=[
            pl.BlockSpec((None, tf, D_MODEL), lambda b, j: (b, j, 0)),
            _mod_prompt_spec(layer),
            ls(D_MODEL, D_FF), ls(D_MODEL, D_FF), ls(D_FF, D_MODEL),
            ls(1, D_MODEL), ls(1, D_MODEL),
        ],
        out_specs=pl.BlockSpec((None, tf, D_MODEL), lambda b, j: (b, j, 0)),
        out_shape=jax.ShapeDtypeStruct((BATCH, SEQ, D_MODEL), F32),
        compiler_params=pltpu.CompilerParams(
            dimension_semantics=("parallel", "parallel"), vmem_limit_bytes=VMEM_LIMIT),
        name="prompt_ffn",
    )(x1, mod4, P["w_ffn_gate"], P["w_ffn_up"], P["w_ffn_down"], P["ln2_g"], P["ln2_b"])


def _s_pre_kernel(x_ref, mod_ref, win_ref, wgate_ref, wdt_ref, bg_ref, lnvg_ref, lnvb_ref, wsp0_ref, bsp0_ref,
                  convw_ref, convb_ref, cstate_ref, dtb_ref, alog_ref, dskip_ref, wcm_ref, e64_ref,
                  v_ref, conv_ref, da_ref, xdtt_ref, b_ref, c_ref, cmpart_ref, zs_ref, gs_ref, xsd_ref):
    x = x_ref[...]
    sh1 = mod_ref[:, 0:D_MODEL]
    sc1 = mod_ref[:, D_MODEL:2 * D_MODEL]
    h = (x * (1.0 + sc1) + sh1).astype(BF16)

    u, v = _cm_branch(_bdot(h, win_ref[:, C_U:C_Z]), lnvg_ref[...], lnvb_ref[...])
    v_ref[...] = v
    out_cm = (u * (wsp0_ref[...] * v + bsp0_ref[...])).astype(BF16)
    gate_cm = _sigmoid(_bdot(h, wgate_ref[:, :D_MODEL]) + bg_ref[:, :D_MODEL])
    cmpart_ref[...] = gate_cm * _bdot(out_cm, wcm_ref[...])

    xbc = _bdot(h, win_ref[:, C_XBC:C_END])
    acc = convb_ref[...]
    for k in range(CONV_K - 1):
        acc = acc + cstate_ref[:, k * CONV_DIM:(k + 1) * CONV_DIM] * convw_ref[k:k + 1, :]
    acc = acc + xbc * convw_ref[CONV_K - 1:CONV_K, :]
    conv_ref[:, 0:(CONV_K - 2) * CONV_DIM] = cstate_ref[:, CONV_DIM:(CONV_K - 1) * CONV_DIM]
    conv_ref[:, (CONV_K - 2) * CONV_DIM:] = xbc
    xbc_c = _silu(acc)
    xs = xbc_c[:, :D_INNER]
    b_ref[...] = xbc_c[:, D_INNER:D_INNER + SSD_GROUPS * STATE]
    c_ref[...] = xbc_c[:, D_INNER + SSD_GROUPS * STATE:]
    dt = _softplus(_dt_cols(h, wdt_ref) + dtb_ref[...])
    da_ref[...] = jnp.exp(dt * (-jnp.exp(alog_ref[...])))
    xdt = xs * _bdot(_pack3(dt), e64_ref[...])
    xdtt_ref[...] = xdt.T.astype(BF16)
    xsd_ref[...] = xs * dskip_ref[...]
    zs_ref[...] = _silu(_bdot(h, win_ref[:, C_Z:C_XBC]))
    gs_ref[...] = _sigmoid(_bdot(h, wgate_ref[:, D_MODEL:]) + bg_ref[:, D_MODEL:])


def _sample_pre(layer, x, mod, cstate, P, consts):
    n = DEC_BATCH
    ls = lambda *shape: _layer_spec(shape, layer)
    in_specs = [
        _const_spec((n, D_MODEL)), ls(n, 6 * D_MODEL),
        ls(D_MODEL, C_END), ls(D_MODEL, 2 * D_MODEL), ls(D_MODEL, LANES),
        ls(1, 2 * D_MODEL), ls(1, D_CM), ls(1, D_CM), ls(1, D_CM), ls(1, D_CM),
        ls(CONV_K, CONV_DIM), ls(1, CONV_DIM), ls(n, (CONV_K - 1) * CONV_DIM),
        ls(1, LANES), ls(1, LANES), ls(1, D_INNER), ls(D_CM, D_MODEL), _const_spec((LANES, D_INNER)),
    ]
    outs = [((n, D_CM), F32), ((n, (CONV_K - 1) * CONV_DIM), F32), ((n, LANES), F32),
            ((D_INNER, n), BF16), ((n, SSD_GROUPS * STATE), F32), ((n, SSD_GROUPS * STATE), F32),
            ((n, D_MODEL), F32), ((n, D_INNER), F32), ((n, D_MODEL), F32), ((n, D_INNER), F32)]
    return pl.pallas_call(
        _s_pre_kernel,
        grid=(1,),
        in_specs=in_specs,
        out_specs=[_const_spec(s, single=False) for s, _ in outs],
        out_shape=[jax.ShapeDtypeStruct(s, d) for s, d in outs],
        compiler_params=pltpu.CompilerParams(
            dimension_semantics=("arbitrary",), vmem_limit_bytes=VMEM_LIMIT),
        name="sample_pre",
    )(x, mod, P["w_in"], P["w_gates"], P["w_dt"], P["b_gate"], P["ln_v_g"], P["ln_v_b"], P["wsp0"], P["bsp0"],
      P["conv_w"], P["conv_b"], cstate, P["dt_bias3"], P["a_log3"], P["dskip_exp"], P["w_cm_br"],
      consts["e64"])


def _s_ssd_kernel(da_ref, xdtt_ref, b_ref, c_ref, st_ref, *rest):
    sto_ref, y_ref = rest[-2:]
    i = pl.program_id(0)

    @pl.when(i == 0)
    def _():
        y_ref[...] = jnp.zeros_like(y_ref)

    rows = lax.broadcasted_iota(jnp.int32, (DEC_BATCH, STATE), 0)
    rows8 = lax.broadcasted_iota(jnp.int32, (8, STATE), 0)
    gw = HEADS_PER_GROUP * HEAD_DIM
    for q in range(SSD_BB):
        b = i * SSD_BB + q
        b8 = pl.multiple_of((b // 8) * 8, 8)
        for g in range(SSD_GROUPS):
            bg = jnp.where(rows == b, b_ref[:, g * STATE:(g + 1) * STATE], 0.0).astype(BF16)
            xb = _bdot(xdtt_ref[g * gw:(g + 1) * gw, :], bg)
            blocks = []
            for hh in range(HEADS_PER_GROUP):
                hd = g * HEADS_PER_GROUP + hh
                r = hd * HEAD_DIM
                blk = st_ref[q, r:r + HEAD_DIM, :] * da_ref[b, hd] + xb[hh * HEAD_DIM:(hh + 1) * HEAD_DIM, :]
                sto_ref[q, r:r + HEAD_DIM, :] = blk
                blocks.append(blk.astype(BF16))
            newg = jnp.concatenate(blocks, axis=0)
            cg = jnp.where(rows8 == b - b8, c_ref[pl.ds(b8, 8), g * STATE:(g + 1) * STATE], 0.0).astype(BF16)
            yg = lax.dot_general(cg, newg, (((1,), (1,)), ((), ())), preferred_element_type=F32)
            y_ref[pl.ds(b8, 8), g * gw:(g + 1) * gw] += yg


def _sample_ssd(layer, da, xdtt, bm, cm, state, stacked):
    n = DEC_BATCH
    in_specs = [
        pl.BlockSpec((D_INNER, n), lambda i, da: (0, 0)),
        pl.BlockSpec((n, SSD_GROUPS * STATE), lambda i, da: (0, 0)),
        pl.BlockSpec((n, SSD_GROUPS * STATE), lambda i, da: (0, 0)),
        pl.BlockSpec((None, SSD_BB, D_INNER, STATE), lambda i, da: (layer, i, 0, 0)),
    ]
    args = [da, xdtt, bm, cm, state]
    aliases = {}
    if stacked is not None:
        in_specs.append(pl.BlockSpec(memory_space=pl.ANY))
        args.append(stacked)
        aliases = {len(args) - 1: 0}
    grid_spec = pltpu.PrefetchScalarGridSpec(
        num_scalar_prefetch=1,
        grid=(n // SSD_BB,),
        in_specs=in_specs,
        out_specs=[
            pl.BlockSpec((None, SSD_BB, D_INNER, STATE), lambda i, da: (layer, i, 0, 0)),
            pl.BlockSpec((n, D_INNER), lambda i, da: (0, 0)),
        ],
    )
    return pl.pallas_call(
        _s_ssd_kernel,
        grid_spec=grid_spec,
        out_shape=[jax.ShapeDtypeStruct((DEPTH, n, D_INNER, STATE), F32),
                   jax.ShapeDtypeStruct((n, D_INNER), F32)],
        input_output_aliases=aliases,
        compiler_params=pltpu.CompilerParams(
            dimension_semantics=("arbitrary",), vmem_limit_bytes=VMEM_LIMIT),
        name="sample_ssd",
    )(*args)


def _s_post_kernel(x_ref, mod_ref, y_ref, xsd_ref, zs_ref, gs_ref, cmpart_ref, normw_ref,
                   wssd_ref, wo_ref, ln1g_ref, ln1b_ref, wg_ref, wu_ref, wd_ref, ln2g_ref, ln2b_ref, o_ref):
    md = lambda k: mod_ref[:, k * D_MODEL:(k + 1) * D_MODEL]
    yn = _rms_norm((y_ref[...] + xsd_ref[...]) * zs_ref[...], normw_ref[...]).astype(BF16)
    merged = cmpart_ref[...] + gs_ref[...] * _bdot(yn, wssd_ref[...])
    x1 = _post_mixer(x_ref[...], md(2), merged.astype(BF16), wo_ref, ln1g_ref[...], ln1b_ref[...])
    o_ref[...] = _ffn(x1, md(3), md(4), md(5), wg_ref, wu_ref, wd_ref, ln2g_ref[...], ln2b_ref[...])


def _sample_post(layer, x, mod, y, xsd, zs, gs, cmpart, P):
    n = DEC_BATCH
    ls = lambda *shape: _layer_spec(shape, layer)
    cs = _const_spec
    in_specs = [
        cs((n, D_MODEL)), ls(n, 6 * D_MODEL), cs((n, D_INNER)), cs((n, D_INNER)), cs((n, D_INNER)),
        cs((n, D_MODEL)), cs((n, D_MODEL)), ls(1, D_INNER), ls(D_INNER, D_MODEL), ls(D_MODEL, D_MODEL),
        ls(1, D_MODEL), ls(1, D_MODEL), ls(D_MODEL, D_FF), ls(D_MODEL, D_FF), ls(D_FF, D_MODEL),
        ls(1, D_MODEL), ls(1, D_MODEL),
    ]
    return pl.pallas_call(
        _s_post_kernel,
        grid=(1,),
        in_specs=in_specs,
        out_specs=_const_spec((n, D_MODEL), single=False),
        out_shape=jax.ShapeDtypeStruct((n, D_MODEL), F32),
        compiler_params=pltpu.CompilerParams(
            dimension_semantics=("arbitrary",), vmem_limit_bytes=VMEM_LIMIT),
        name="sample_post",
    )(x, mod, y, xsd, zs, gs, cmpart, P["ssd_norm_w"], P["w_ssd_br"], P["w_o"], P["ln1_g"], P["ln1_b"],
      P["w_ffn_gate"], P["w_ffn_up"], P["w_ffn_down"], P["ln2_g"], P["ln2_b"])


def _constants():
    tri = np.tril(np.ones((CHUNK, CHUNK), np.float32))
    e64 = np.zeros((LANES, D_INNER), np.float32)
    for r in range(DT_COPIES * HEADS):
        hd = r % HEADS
        e64[r, hd * HEAD_DIM:(hd + 1) * HEAD_DIM] = 1.0
    cbias = np.where(tri > 0, 0.0, -np.inf).astype(np.float32)
    return {"tri": jnp.asarray(tri, BF16), "e64": jnp.asarray(e64, BF16), "cbias": jnp.asarray(cbias)}


def _params(p):
    rows = lambda a: a[:, None, :]
    pad = jnp.zeros((DEPTH, LANES - DT_COPIES * HEADS), F32)
    return {
        **_win_prep(p["w_in"]),
        "b_gate": rows(p["b_gate"]),
        "ln_v_g": rows(p["ln_v_g"]), "ln_v_b": rows(p["ln_v_b"]),
        "w_spatial": p["w_spatial"],
        "bsp_full": jnp.repeat(jnp.swapaxes(p["b_spatial"], 1, 2), LANES, axis=2),
        "wsp0": rows(jnp.repeat(p["w_spatial"][:, :, 0, 0], LANES, axis=1)),
        "bsp0": rows(jnp.repeat(p["b_spatial"][:, :, 0], LANES, axis=1)),
        "conv_w": p["conv_w"], "conv_b": rows(p["conv_b"]),
        "dt_bias3": rows(jnp.concatenate([p["dt_bias"]] * DT_COPIES + [pad], axis=1)),
        "a_log3": rows(jnp.concatenate([p["a_log"]] * DT_COPIES + [pad], axis=1)),
        "dskip_exp": rows(jnp.repeat(p["d_skip"], HEAD_DIM, axis=1)),
        "ssd_norm_w": rows(p["ssd_norm_w"]),
        "w_cm_br": _to_bf16(p["w_cm_br"], 512), "w_ssd_br": _to_bf16(p["w_ssd_br"], 512),
        "w_o": _to_bf16(p["w_o"], 512),
        "ln1_g": rows(p["ln1_g"]), "ln1_b": rows(p["ln1_b"]),
        "w_ffn_gate": _to_bf16(p["w_ffn_gate"], 512), "w_ffn_up": _to_bf16(p["w_ffn_up"], 512),
        "w_ffn_down": _to_bf16(p["w_ffn_down"], D_FF // 4),
        "ln2_g": rows(p["ln2_g"]), "ln2_b": rows(p["ln2_b"]),
    }


def kernel(x_prompt, x_sample, state_ssd, state_conv, c_prompt, c_sample, w_ada, b_ada, w_in, b_gate,
           ln_v_g, ln_v_b, w_spatial, b_spatial, conv_w, conv_b, dt_bias, a_log, d_skip, ssd_norm_w,
           w_cm_br, w_ssd_br, w_o, ln1_g, ln1_b, w_ffn_gate, w_ffn_up, w_ffn_down, ln2_g, ln2_b):
    P = _params(dict(w_in=w_in, b_gate=b_gate, ln_v_g=ln_v_g, ln_v_b=ln_v_b, w_spatial=w_spatial,
                     b_spatial=b_spatial, conv_w=conv_w, conv_b=conv_b, dt_bias=dt_bias, a_log=a_log,
                     d_skip=d_skip, ssd_norm_w=ssd_norm_w, w_cm_br=w_cm_br, w_ssd_br=w_ssd_br, w_o=w_o,
                     ln1_g=ln1_g, ln1_b=ln1_b, w_ffn_gate=w_ffn_gate, w_ffn_up=w_ffn_up,
                     w_ffn_down=w_ffn_down, ln2_g=ln2_g, ln2_b=ln2_b))
    consts = _constants()
    mod, mod_p = _adaln_mod(jnp.concatenate([c_sample, c_prompt], axis=0), w_ada, b_ada)
    mod4 = mod_p.reshape(DEPTH, BATCH, 6, D_MODEL)
    xp = x_prompt
    xs = x_sample.reshape(DEC_BATCH, D_MODEL)
    ssd_p, conv_p, conv_s, v_s = [], [], [], []
    state_in = state_ssd.reshape(DEPTH, DEC_BATCH, D_INNER, STATE)
    cstate = state_conv.reshape(DEPTH, DEC_BATCH, (CONV_K - 1) * CONV_DIM)
    ssd_s = None
    for i in range(DEPTH):
        x1, sp, cp = _prompt_mixer(i, xp, mod4, P, consts)
        xp = _prompt_ffn(i, x1, mod4, P)
        ssd_p.append(sp.reshape(BATCH, HEADS, HEAD_DIM, STATE))
        conv_p.append(cp)
        v, cs, da, xdtt, bm, cm, cmpart, zs, gs, xsd = _sample_pre(i, xs, mod, cstate, P, consts)
        ssd_s, y = _sample_ssd(i, da[:, :HEADS], xdtt, bm, cm, state_in, ssd_s)
        xs = _sample_post(i, xs, mod, y, xsd, zs, gs, cmpart, P)
        conv_s.append(cs.reshape(DEC_BATCH, CONV_K - 1, CONV_DIM))
        v_s.append(v.reshape(DEC_BATCH, 1, D_CM))
    return (xp, xs.reshape(DEC_BATCH, 1, D_MODEL), jnp.stack(ssd_p), jnp.stack(conv_p),
            ssd_s.reshape(DEPTH, DEC_BATCH, HEADS, HEAD_DIM, STATE), jnp.stack(conv_s), jnp.stack(v_s))
```

```python
import functools
import math

import numpy as np
import jax
import jax.numpy as jnp
from jax import lax
from jax.experimental import pallas as pl
from jax.experimental.pallas import tpu as pltpu

D_MODEL = 1024
BATCH = 8
SEQ = 2048
DEPTH = 2
DEC_BATCH = 128
CHUNK = 128
D_CM = D_MODEL
CM_GROUPS = 8
D_INNER = 2048
HEAD_DIM = 64
HEADS = 32
SSD_GROUPS = 4
HEADS_PER_GROUP = HEADS // SSD_GROUPS
STATE = 128
CONV_K = 4
CONV_DIM = D_INNER + 2 * SSD_GROUPS * STATE
D_FF = 2816
ALPHA = (2 * DEPTH) ** 0.25
LN_EPS = 1e-5
REF_DT = 2 * D_CM + D_INNER + CONV_DIM
REF_GATES = REF_DT + HEADS

LANES = 128
DT_COPIES = 3
C_U = 0
C_V = C_U + D_CM
C_Z = C_V + D_CM
C_XBC = C_Z + D_INNER
C_END = C_XBC + CONV_DIM

TILE_L = 256
TILE_F = 512
SSD_BB = 8
VMEM_LIMIT = 56 * 1024 * 1024

F32 = jnp.float32
BF16 = jnp.bfloat16


def _bdot(a, b):
    return jnp.dot(a, b, preferred_element_type=F32)


def _gelu(x):
    return 0.5 * x * (1.0 + lax.erf(x * np.float32(math.sqrt(0.5))))


def _sigmoid(x):
    return 1.0 / (1.0 + jnp.exp(-x))


def _silu(x):
    hx = 0.5 * x
    return hx + hx * jnp.tanh(hx)


def _softplus(x):
    return jnp.maximum(x, 0.0) + jnp.log(1.0 + jnp.exp(-jnp.abs(x)))


def _layer_norm(x, g, b):
    mu = jnp.mean(x, axis=-1, keepdims=True)
    xc = x - mu
    var = jnp.mean(xc * xc, axis=-1, keepdims=True)
    return xc * lax.rsqrt(var + LN_EPS) * g + b


def _rms_norm(x, g):
    return x * lax.rsqrt(jnp.mean(x * x, axis=-1, keepdims=True) + LN_EPS) * g


def _pack3(x):
    hi = x.astype(BF16).astype(F32)
    r1 = x - hi
    mid = r1.astype(BF16).astype(F32)
    r2 = r1 - mid
    lane = lax.broadcasted_iota(jnp.int32, x.shape, 1)
    return jnp.where(lane < HEADS, x, jnp.where(lane < 2 * HEADS, r1, r2)).astype(BF16)


def _const_spec(shape, single=True):
    nd = len(shape)
    kw = {"pipeline_mode": pl.Buffered(1)} if single else {}
    return pl.BlockSpec(shape, lambda *_: (0,) * nd, **kw)


def _layer_spec(shape, layer):
    nd = len(shape)
    return pl.BlockSpec((None,) + tuple(shape), lambda *_: (layer,) + (0,) * nd,
                        pipeline_mode=pl.Buffered(1))


def _cast_kernel(w_ref, o_ref):
    o_ref[...] = w_ref[...].astype(BF16)


def _to_bf16(w, rows):
    depth, r, c = w.shape
    return pl.pallas_call(
        _cast_kernel,
        grid=(depth, r // rows),
        in_specs=[pl.BlockSpec((None, rows, c), lambda l, i: (l, i, 0))],
        out_specs=pl.BlockSpec((None, rows, c), lambda l, i: (l, i, 0)),
        out_shape=jax.ShapeDtypeStruct((depth, r, c), BF16),
        compiler_params=pltpu.CompilerParams(
            dimension_semantics=("parallel", "parallel"), vmem_limit_bytes=VMEM_LIMIT),
        name="cast_bf16",
    )(w)


def _tcast_kernel(w_ref, o_ref):
    o_ref[...] = w_ref[...].T.astype(BF16)


def _tcast_dt_kernel(w_ref, o_ref):
    t = w_ref[...].T
    lane = lax.broadcasted_iota(jnp.int32, t.shape, 1)
    o_ref[...] = jnp.where(lane < HEADS, t, 0.0).astype(BF16)


def _win_prep(w_in, rows=512):
    wt = jnp.swapaxes(w_in, 1, 2)
    params = pltpu.CompilerParams(dimension_semantics=("parallel", "parallel"), vmem_limit_bytes=VMEM_LIMIT)

    def piece(kern, n_out, in_spec, nblk, blk):
        return pl.pallas_call(
            kern, grid=(DEPTH, nblk), in_specs=[in_spec],
            out_specs=pl.BlockSpec((None, D_MODEL, blk), lambda l, i: (l, 0, i)),
            out_shape=jax.ShapeDtypeStruct((DEPTH, D_MODEL, n_out), BF16),
            compiler_params=params, name="w_in_prep")(wt)

    main = piece(_tcast_kernel, C_END, pl.BlockSpec((None, rows, D_MODEL), lambda l, i: (l, i, 0)),
                 C_END // rows, rows)
    gates = piece(lambda w_ref, o_ref: _tcast_kernel(w_ref.at[0], o_ref), 2 * D_MODEL,
                  pl.BlockSpec((pl.Element(1), pl.Element(rows), pl.Element(D_MODEL)),
                               lambda l, i: (l, pl.multiple_of(REF_GATES + i * rows, 8), 0)),
                  2 * D_MODEL // rows, rows)
    dt = piece(_tcast_dt_kernel, LANES,
               pl.BlockSpec((None, LANES, D_MODEL), lambda l, i: (l, REF_DT // LANES, 0)), 1, LANES)
    return {"w_in": main, "w_gates": gates, "w_dt": dt}


def _dt_cols(h, wdt_ref):
    d = _bdot(h, wdt_ref[...])
    out = d
    for k in range(1, DT_COPIES):
        out = out + pltpu.roll(d, k * HEADS, 1)
    return out


def _mod_kernel(c_ref, w_ref, b_ref, os_ref, op_ref):
    s = _silu(c_ref[...])
    res = _bdot(s.astype(BF16), w_ref[...].astype(BF16)) + b_ref[...]
    os_ref[...] = res[:DEC_BATCH]
    op_ref[...] = res[DEC_BATCH:]


def _adaln_mod(c_all, w_ada, b_ada):
    rows = c_all.shape[0]
    nblk = 6
    return pl.pallas_call(
        _mod_kernel,
        grid=(DEPTH, nblk),
        in_specs=[
            pl.BlockSpec((rows, D_MODEL), lambda l, k: (0, 0)),
            pl.BlockSpec((None, D_MODEL, D_MODEL), lambda l, k: (l, 0, k)),
            pl.BlockSpec((None, 1, D_MODEL), lambda l, k: (l, 0, k)),
        ],
        out_specs=[pl.BlockSpec((None, DEC_BATCH, D_MODEL), lambda l, k: (l, 0, k)),
                   pl.BlockSpec((None, BATCH, D_MODEL), lambda l, k: (l, 0, k))],
        out_shape=[jax.ShapeDtypeStruct((DEPTH, DEC_BATCH, nblk * D_MODEL), F32),
                   jax.ShapeDtypeStruct((DEPTH, BATCH, nblk * D_MODEL), F32)],
        compiler_params=pltpu.CompilerParams(
            dimension_semantics=("parallel", "parallel"), vmem_limit_bytes=VMEM_LIMIT),
        name="adaln_mod",
    )(c_all, w_ada, b_ada.reshape(DEPTH, 1, nblk * D_MODEL))


def _cm_branch(u_v, lnvg, lnvb):
    u = _gelu(u_v[:, :D_CM])
    v = _layer_norm(_gelu(u_v[:, D_CM:]), lnvg, lnvb)
    return u, v


def _fold(v):
    s8 = jnp.sum(v.reshape(v.shape[0] // 8, 8, v.shape[1]), axis=0)
    d = s8[:, 0:LANES]
    for k in range(1, v.shape[1] // LANES):
        d = d + s8[:, k * LANES:(k + 1) * LANES]
    return d[0:1, 0:1]


def _after(never, dep, val):
    return jnp.where(never, jnp.broadcast_to(dep, val.shape).astype(val.dtype), val)


def _post_mixer(x, g1, merged_bf16, wo_ref, ln1g, ln1b):
    o = _bdot(merged_bf16, wo_ref[...])
    return _layer_norm(ALPHA * x + g1 * o, ln1g, ln1b)


def _ffn(x1, sh2, sc2, g2, wg_ref, wu_ref, wd_ref, ln2g, ln2b):
    h2 = (x1 * (1.0 + sc2) + sh2).astype(BF16)
    gate = _bdot(h2, wg_ref[...])
    up = _bdot(h2, wu_ref[...])
    act = (_silu(gate) * up).astype(BF16)
    ffn = _bdot(act, wd_ref[...])
    return _layer_norm(ALPHA * x1 + g2 * ffn, ln2g, ln2b)


def _ssd_chunk(xs_c, b_c, c_c, dtseg, dtb, alog, tri, e64, cbias, state_ref, y_ref, r0):
    row = lax.broadcasted_iota(jnp.int32, (CHUNK, LANES), 0)
    lane = lax.broadcasted_iota(jnp.int32, (CHUNK, LANES), 1)
    low_half = lane < HEAD_DIM

    dt = _softplus(dtseg + dtb)
    a = -jnp.exp(alog)
    da = dt * a
    hi = da.astype(BF16)
    r1 = da - hi.astype(F32)
    mid = r1.astype(BF16)
    lo = (r1 - mid.astype(F32)).astype(BF16)
    acum = _bdot(tri, hi) + _bdot(tri, mid) + _bdot(tri, lo)
    shift_t = acum.T - jnp.log(dt.T)
    alast = acum[CHUNK - 1:CHUNK, :]
    w = dt * jnp.exp(alast - acum)
    wexp = _bdot(_pack3(w), e64)
    eexp = _bdot(_pack3(jnp.exp(acum)), e64)
    cd = jnp.broadcast_to(jnp.exp(alast), (8, LANES))
    cdexp = _bdot(_pack3(cd), e64)[0:1, :]
    xs_b = xs_c.astype(BF16)
    zero_b = jnp.zeros((CHUNK, LANES), BF16)

    gw = HEADS_PER_GROUP * HEAD_DIM
    for g in range(SSD_GROUPS):
        gc = g * gw
        bg = b_c[:, g * STATE:(g + 1) * STATE]
        cgb = c_c[:, g * STATE:(g + 1) * STATE].astype(BF16)
        cb = lax.dot_general(cgb, bg.astype(BF16), (((1,), (1,)), ((), ())),
                             preferred_element_type=F32)
        y_off = eexp[:, gc:gc + gw] * _bdot(cgb, state_ref[:, gc:gc + gw].astype(BF16))
        for pr in range(HEADS_PER_GROUP // 2):
            h0 = g * HEADS_PER_GROUP + 2 * pr
            pieces = []
            for h in (h0, h0 + 1):
                seg = jnp.broadcast_to(acum[:, h:h + 1], (CHUNK, LANES)) - shift_t[h:h + 1, :]
                pieces.append((cb * jnp.exp(seg + cbias)).astype(BF16))
            lhs = jnp.concatenate(pieces, axis=1)
            pc = h0 * HEAD_DIM
            xs_pair = xs_b[:, pc:pc + LANES]
            rhs = jnp.concatenate([jnp.where(low_half, xs_pair, zero_b),
                                   jnp.where(low_half, zero_b, xs_pair)], axis=0)
            y_ref[r0:r0 + CHUNK, pc:pc + LANES] = (
                _bdot(lhs, rhs) + y_off[:, pc - gc:pc - gc + LANES])
        xw = (xs_c[:, gc:gc + gw] * wexp[:, gc:gc + gw]).astype(BF16)
        upd = _bdot(bg.T.astype(BF16), xw)
        state_ref[:, gc:gc + gw] = state_ref[:, gc:gc + gw] * cdexp[:, gc:gc + gw] + upd


def _p1_kernel(x_ref, mod_ref, win_ref, wgate_ref, wdt_ref, bg_ref, lnvg_ref, lnvb_ref, wsp_ref, bsp_ref,
               convw_ref, convb_ref, dtb_ref, alog_ref, dskip_ref, normw_ref,
               wcm_ref, wssd_ref, wo_ref, ln1g_ref, ln1b_ref, tri_ref, e64_ref, cbias_ref,
               x1_ref, ssd_ref, conv_ref,
               state_ref, cbuf_ref, y_ref, *, tl):
    j = pl.program_id(1)
    last = pl.num_programs(1) - 1
    nc = tl // CHUNK

    @pl.when(j == 0)
    def _():
        state_ref[...] = jnp.zeros_like(state_ref)
        cbuf_ref[0:8, :] = jnp.zeros((8, CONV_DIM), F32)

    x = x_ref[...]
    sh1 = mod_ref[0:1, :]
    sc1 = mod_ref[1:2, :]
    g1 = mod_ref[2:3, :]
    h = (x * (1.0 + sc1) + sh1).astype(BF16)

    parts = []
    for c in range(nc):
        r0 = c * CHUNK
        cbuf_ref[8 + r0:8 + r0 + CHUNK, :] = _bdot(h[r0:r0 + CHUNK], win_ref[:, C_XBC:C_END])
        acc = convb_ref[...]
        for k in range(CONV_K):
            acc = acc + cbuf_ref[5 + k + r0:5 + k + r0 + CHUNK, :] * convw_ref[k:k + 1, :]
        parts.append(_silu(acc))
    cbuf_ref[5:8, :] = cbuf_ref[tl + 5:tl + 8, :]
    xbc_c = jnp.concatenate(parts, axis=0) if nc > 1 else parts[0]
    xs = xbc_c[:, :D_INNER]
    dtseg = _dt_cols(h, wdt_ref)
    u_v = _bdot(h, win_ref[:, C_U:C_Z])
    z = _bdot(h, win_ref[:, C_Z:C_XBC])
    gcm_logit = _bdot(h, wgate_ref[:, :D_MODEL])
    gssd_logit = _bdot(h, wgate_ref[:, D_MODEL:])

    u, v = _cm_branch(u_v, lnvg_ref[...], lnvb_ref[...])
    vb = v.astype(BF16)
    row = lax.broadcasted_iota(jnp.int32, (CHUNK, CHUNK), 0)
    col = lax.broadcasted_iota(jnp.int32, (CHUNK, CHUNK), 1)
    wmix = [jnp.where(row >= col, wsp_ref[g], 0.0).astype(BF16) for g in range(CM_GROUPS)]

    rows = []
    for c in range(nc):
        cols = []
        for g in range(CM_GROUPS):
            cols.append(_bdot(wmix[g], vb[c * CHUNK:(c + 1) * CHUNK, g * LANES:(g + 1) * LANES]))
        rows.append(jnp.concatenate(cols, axis=1) + bsp_ref[...])
    mix = jnp.concatenate(rows, axis=0) if nc > 1 else rows[0]
    out_cm = (u * mix).astype(BF16)
    merged = _sigmoid(gcm_logit + bg_ref[:, :D_MODEL]) * _bdot(out_cm, wcm_ref[...])

    side_done = _fold(merged) + _fold(gssd_logit)
    for c in range(nc):
        r0 = c * CHUNK
        dtb = dtb_ref[...] if c == 0 else _after(pl.program_id(1) < 0, side_done, dtb_ref[...])
        _ssd_chunk(xs[r0:r0 + CHUNK], xbc_c[r0:r0 + CHUNK, D_INNER:D_INNER + SSD_GROUPS * STATE],
                   xbc_c[r0:r0 + CHUNK, D_INNER + SSD_GROUPS * STATE:], dtseg[r0:r0 + CHUNK],
                   dtb, alog_ref[...], tri_ref[...], e64_ref[...], cbias_ref[...],
                   state_ref, y_ref, r0)

    yz = (y_ref[...] + xs * dskip_ref[...]) * _silu(z)
    yn = _rms_norm(yz, normw_ref[...]).astype(BF16)
    merged = merged + _sigmoid(gssd_logit + bg_ref[:, D_MODEL:]) * _bdot(yn, wssd_ref[...])
    x1_ref[...] = _post_mixer(x, g1, merged.astype(BF16), wo_ref, ln1g_ref[...], ln1b_ref[...])

    @pl.when(j == last)
    def _():
        conv_ref[...] = cbuf_ref[5:8, :]
        ssd_ref[...] = state_ref[...].T


def _mod_prompt_spec(layer):
    return pl.BlockSpec((None, None, 6, D_MODEL), lambda b, j: (layer, b, 0, 0))


def _prompt_mixer(layer, x, mod4, P, consts):
    tl = TILE_L
    nj = SEQ // tl
    ls = lambda *shape: _layer_spec(shape, layer)
    in_specs = [
        pl.BlockSpec((None, tl, D_MODEL), lambda b, j: (b, j, 0)),
        _mod_prompt_spec(layer),
        ls(D_MODEL, C_END), ls(D_MODEL, 2 * D_MODEL), ls(D_MODEL, LANES),
        ls(1, 2 * D_MODEL), ls(1, D_CM), ls(1, D_CM),
        ls(CM_GROUPS, CHUNK, CHUNK), ls(CHUNK, D_CM),
        ls(CONV_K, CONV_DIM), ls(1, CONV_DIM), ls(1, LANES), ls(1, LANES),
        ls(1, D_INNER), ls(1, D_INNER),
        ls(D_CM, D_MODEL), ls(D_INNER, D_MODEL), ls(D_MODEL, D_MODEL),
        ls(1, D_MODEL), ls(1, D_MODEL),
        _const_spec((CHUNK, CHUNK)), _const_spec((LANES, D_INNER)), _const_spec((CHUNK, CHUNK)),
    ]
    out_specs = [
        pl.BlockSpec((None, tl, D_MODEL), lambda b, j: (b, j, 0)),
        pl.BlockSpec((None, D_INNER, STATE), lambda b, j: (b, 0, 0)),
        pl.BlockSpec((None, CONV_K - 1, CONV_DIM), lambda b, j: (b, 0, 0)),
    ]
    out_shape = [
        jax.ShapeDtypeStruct((BATCH, SEQ, D_MODEL), F32),
        jax.ShapeDtypeStruct((BATCH, D_INNER, STATE), F32),
        jax.ShapeDtypeStruct((BATCH, CONV_K - 1, CONV_DIM), F32),
    ]
    return pl.pallas_call(
        functools.partial(_p1_kernel, tl=tl),
        grid=(BATCH, nj),
        in_specs=in_specs, out_specs=out_specs, out_shape=out_shape,
        scratch_shapes=[pltpu.VMEM((STATE, D_INNER), F32),
                        pltpu.VMEM((tl + 8, CONV_DIM), F32),
                        pltpu.VMEM((tl, D_INNER), F32)],
        compiler_params=pltpu.CompilerParams(
            dimension_semantics=("parallel", "arbitrary"), vmem_limit_bytes=VMEM_LIMIT),
        name="prompt_mixer",
    )(x, mod4, P["w_in"], P["w_gates"], P["w_dt"], P["b_gate"], P["ln_v_g"], P["ln_v_b"], P["w_spatial"], P["bsp_full"],
      P["conv_w"], P["conv_b"], P["dt_bias3"], P["a_log3"], P["dskip_exp"], P["ssd_norm_w"],
      P["w_cm_br"], P["w_ssd_br"], P["w_o"], P["ln1_g"], P["ln1_b"],
      consts["tri"], consts["e64"], consts["cbias"])


def _p2_kernel(x_ref, mod_ref, wg_ref, wu_ref, wd_ref, ln2g_ref, ln2b_ref, o_ref):
    o_ref[...] = _ffn(x_ref[...], mod_ref[3:4, :], mod_ref[4:5, :], mod_ref[5:6, :],
                      wg_ref, wu_ref, wd_ref, ln2g_ref[...], ln2b_ref[...])


def _prompt_ffn(layer, x1, mod4, P):
    tf = TILE_F
    ls = lambda *shape: _layer_spec(shape, layer)
    return pl.pallas_call(
        _p2_kernel,
        grid=(BATCH, SEQ // tf),
        in_specs=[
            pl.BlockSpec((None, tf, D_MODEL), lambda b, j: (b, j, 0)),
            _mod_prompt_spec(layer),
            ls(D_MODEL, D_FF), ls(D_MODEL, D_FF), ls(D_FF, D_MODEL),
            ls(1, D_MODEL), ls(1, D_MODEL),
        ],
        out_specs=pl.BlockSpec((None, tf, D_MODEL), lambda b, j: (b, j, 0)),
        out_shape=jax.ShapeDtypeStruct((BATCH, SEQ, D_MODEL), F32),
        compiler_params=pltpu.CompilerParams(
            dimension_semantics=("parallel", "parallel"), vmem_limit_bytes=VMEM_LIMIT),
        name="prompt_ffn",
    )(x1, mod4, P["w_ffn_gate"], P["w_ffn_up"], P["w_ffn_down"], P["ln2_g"], P["ln2_b"])


def _s_pre_kernel(x_ref, mod_ref, win_ref, wgate_ref, wdt_ref, bg_ref, lnvg_ref, lnvb_ref, wsp0_ref, bsp0_ref,
                  convw_ref, convb_ref, cstate_ref, dtb_ref, alog_ref, dskip_ref, wcm_ref, e64_ref,
                  v_ref, conv_ref, da_ref, xdtt_ref, b_ref, c_ref, cmpart_ref, zs_ref, gs_ref, xsd_ref):
    x = x_ref[...]
    sh1 = mod_ref[:, 0:D_MODEL]
    sc1 = mod_ref[:, D_MODEL:2 * D_MODEL]
    h = (x * (1.0 + sc1) + sh1).astype(BF16)

    u, v = _cm_branch(_bdot(h, win_ref[:, C_U:C_Z]), lnvg_ref[...], lnvb_ref[...])
    v_ref[...] = v
    out_cm = (u * (wsp0_ref[...] * v + bsp0_ref[...])).astype(BF16)
    gate_cm = _sigmoid(_bdot(h, wgate_ref[:, :D_MODEL]) + bg_ref[:, :D_MODEL])
    cmpart_ref[...] = gate_cm * _bdot(out_cm, wcm_ref[...])

    xbc = _bdot(h, win_ref[:, C_XBC:C_END])
    acc = convb_ref[...]
    for k in range(CONV_K - 1):
        acc = acc + cstate_ref[:, k * CONV_DIM:(k + 1) * CONV_DIM] * convw_ref[k:k + 1, :]
    acc = acc + xbc * convw_ref[CONV_K - 1:CONV_K, :]
    conv_ref[:, 0:(CONV_K - 2) * CONV_DIM] = cstate_ref[:, CONV_DIM:(CONV_K - 1) * CONV_DIM]
    conv_ref[:, (CONV_K - 2) * CONV_DIM:] = xbc
    xbc_c = _silu(acc)
    xs = xbc_c[:, :D_INNER]
    b_ref[...] = xbc_c[:, D_INNER:D_INNER + SSD_GROUPS * STATE]
    c_ref[...] = xbc_c[:, D_INNER + SSD_GROUPS * STATE:]
    dt = _softplus(_dt_cols(h, wdt_ref) + dtb_ref[...])
    da_ref[...] = jnp.exp(dt * (-jnp.exp(alog_ref[...])))
    xdt = xs * _bdot(_pack3(dt), e64_ref[...])
    xdtt_ref[...] = xdt.T.astype(BF16)
    xsd_ref[...] = xs * dskip_ref[...]
    zs_ref[...] = _silu(_bdot(h, win_ref[:, C_Z:C_XBC]))
    gs_ref[...] = _sigmoid(_bdot(h, wgate_ref[:, D_MODEL:]) + bg_ref[:, D_MODEL:])


def _sample_pre(layer, x, mod, cstate, P, consts):
    n = DEC_BATCH
    ls = lambda *shape: _layer_spec(shape, layer)
    in_specs = [
        _const_spec((n, D_MODEL)), ls(n, 6 * D_MODEL),
        ls(D_MODEL, C_END), ls(D_MODEL, 2 * D_MODEL), ls(D_MODEL, LANES),
        ls(1, 2 * D_MODEL), ls(1, D_CM), ls(1, D_CM), ls(1, D_CM), ls(1, D_CM),
        ls(CONV_K, CONV_DIM), ls(1, CONV_DIM), ls(n, (CONV_K - 1) * CONV_DIM),
        ls(1, LANES), ls(1, LANES), ls(1, D_INNER), ls(D_CM, D_MODEL), _const_spec((LANES, D_INNER)),
    ]
    outs = [((n, D_CM), F32), ((n, (CONV_K - 1) * CONV_DIM), F32), ((n, LANES), F32),
            ((D_INNER, n), BF16), ((n, SSD_GROUPS * STATE), F32), ((n, SSD_GROUPS * STATE), F32),
            ((n, D_MODEL), F32), ((n, D_INNER), F32), ((n, D_MODEL), F32), ((n, D_INNER), F32)]
    return pl.pallas_call(
        _s_pre_kernel,
        grid=(1,),
        in_specs=in_specs,
        out_specs=[_const_spec(s, single=False) for s, _ in outs],
        out_shape=[jax.ShapeDtypeStruct(s, d) for s, d in outs],
        compiler_params=pltpu.CompilerParams(
            dimension_semantics=("arbitrary",), vmem_limit_bytes=VMEM_LIMIT),
        name="sample_pre",
    )(x, mod, P["w_in"], P["w_gates"], P["w_dt"], P["b_gate"], P["ln_v_g"], P["ln_v_b"], P["wsp0"], P["bsp0"],
      P["conv_w"], P["conv_b"], cstate, P["dt_bias3"], P["a_log3"], P["dskip_exp"], P["w_cm_br"],
      consts["e64"])


def _s_ssd_kernel(da_ref, xdtt_ref, b_ref, c_ref, st_ref, *rest):
    sto_ref, y_ref = rest[-2:]
    i = pl.program_id(0)

    @pl.when(i == 0)
    def _():
        y_ref[...] = jnp.zeros_like(y_ref)

    rows = lax.broadcasted_iota(jnp.int32, (DEC_BATCH, STATE), 0)
    rows8 = lax.broadcasted_iota(jnp.int32, (8, STATE), 0)
    gw = HEADS_PER_GROUP * HEAD_DIM
    for q in range(SSD_BB):
        b = i * SSD_BB + q
        b8 = pl.multiple_of((b // 8) * 8, 8)
        for g in range(SSD_GROUPS):
            bg = jnp.where(rows == b, b_ref[:, g * STATE:(g + 1) * STATE], 0.0).astype(BF16)
            xb = _bdot(xdtt_ref[g * gw:(g + 1) * gw, :], bg)
            blocks = []
            for hh in range(HEADS_PER_GROUP):
                hd = g * HEADS_PER_GROUP + hh
                r = hd * HEAD_DIM
                blk = st_ref[q, r:r + HEAD_DIM, :] * da_ref[b, hd] + xb[hh * HEAD_DIM:(hh + 1) * HEAD_DIM, :]
                sto_ref[q, r:r + HEAD_DIM, :] = blk
                blocks.append(blk.astype(BF16))
            newg = jnp.concatenate(blocks, axis=0)
            cg = jnp.where(rows8 == b - b8, c_ref[pl.ds(b8, 8), g * STATE:(g + 1) * STATE], 0.0).astype(BF16)
            yg = lax.dot_general(cg, newg, (((1,), (1,)), ((), ())), preferred_element_type=F32)
            y_ref[pl.ds(b8, 8), g * gw:(g + 1) * gw] += yg


def _sample_ssd(layer, da, xdtt, bm, cm, state, stacked):
    n = DEC_BATCH
    in_specs = [
        pl.BlockSpec((D_INNER, n), lambda i, da: (0, 0)),
        pl.BlockSpec((n, SSD_GROUPS * STATE), lambda i, da: (0, 0)),
        pl.BlockSpec((n, SSD_GROUPS * STATE), lambda i, da: (0, 0)),
        pl.BlockSpec((None, SSD_BB, D_INNER, STATE), lambda i, da: (layer, i, 0, 0)),
    ]
    args = [da, xdtt, bm, cm, state]
    aliases = {}
    if stacked is not None:
        in_specs.append(pl.BlockSpec(memory_space=pl.ANY))
        args.append(stacked)
        aliases = {len(args) - 1: 0}
    grid_spec = pltpu.PrefetchScalarGridSpec(
        num_scalar_prefetch=1,
        grid=(n // SSD_BB,),
        in_specs=in_specs,
        out_specs=[
            pl.BlockSpec((None, SSD_BB, D_INNER, STATE), lambda i, da: (layer, i, 0, 0)),
            pl.BlockSpec((n, D_INNER), lambda i, da: (0, 0)),
        ],
    )
    return pl.pallas_call(
        _s_ssd_kernel,
        grid_spec=grid_spec,
        out_shape=[jax.ShapeDtypeStruct((DEPTH, n, D_INNER, STATE), F32),
                   jax.ShapeDtypeStruct((n, D_INNER), F32)],
        input_output_aliases=aliases,
        compiler_params=pltpu.CompilerParams(
            dimension_semantics=("arbitrary",), vmem_limit_bytes=VMEM_LIMIT),
        name="sample_ssd",
    )(*args)


def _s_post_kernel(x_ref, mod_ref, y_ref, xsd_ref, zs_ref, gs_ref, cmpart_ref, normw_ref,
                   wssd_ref, wo_ref, ln1g_ref, ln1b_ref, wg_ref, wu_ref, wd_ref, ln2g_ref, ln2b_ref, o_ref):
    md = lambda k: mod_ref[:, k * D_MODEL:(k + 1) * D_MODEL]
    yn = _rms_norm((y_ref[...] + xsd_ref[...]) * zs_ref[...], normw_ref[...]).astype(BF16)
    merged = cmpart_ref[...] + gs_ref[...] * _bdot(yn, wssd_ref[...])
    x1 = _post_mixer(x_ref[...], md(2), merged.astype(BF16), wo_ref, ln1g_ref[...], ln1b_ref[...])
    o_ref[...] = _ffn(x1, md(3), md(4), md(5), wg_ref, wu_ref, wd_ref, ln2g_ref[...], ln2b_ref[...])


def _sample_post(layer, x, mod, y, xsd, zs, gs, cmpart, P):
    n = DEC_BATCH
    ls = lambda *shape: _layer_spec(shape, layer)
    cs = _const_spec
    in_specs = [
        cs((n, D_MODEL)), ls(n, 6 * D_MODEL), cs((n, D_INNER)), cs((n, D_INNER)), cs((n, D_INNER)),
        cs((n, D_MODEL)), cs((n, D_MODEL)), ls(1, D_INNER), ls(D_INNER, D_MODEL), ls(D_MODEL, D_MODEL),
        ls(1, D_MODEL), ls(1, D_MODEL), ls(D_MODEL, D_FF), ls(D_MODEL, D_FF), ls(D_FF, D_MODEL),
        ls(1, D_MODEL), ls(1, D_MODEL),
    ]
    return pl.pallas_call(
        _s_post_kernel,
        grid=(1,),
        in_specs=in_specs,
        out_specs=_const_spec((n, D_MODEL), single=False),
        out_shape=jax.ShapeDtypeStruct((n, D_MODEL), F32),
        compiler_params=pltpu.CompilerParams(
            dimension_semantics=("arbitrary",), vmem_limit_bytes=VMEM_LIMIT),
        name="sample_post",
    )(x, mod, y, xsd, zs, gs, cmpart, P["ssd_norm_w"], P["w_ssd_br"], P["w_o"], P["ln1_g"], P["ln1_b"],
      P["w_ffn_gate"], P["w_ffn_up"], P["w_ffn_down"], P["ln2_g"], P["ln2_b"])


def _constants():
    tri = np.tril(np.ones((CHUNK, CHUNK), np.float32))
    e64 = np.zeros((LANES, D_INNER), np.float32)
    for r in range(DT_COPIES * HEADS):
        hd = r % HEADS
        e64[r, hd * HEAD_DIM:(hd + 1) * HEAD_DIM] = 1.0
    cbias = np.where(tri > 0, 0.0, -np.inf).astype(np.float32)
    return {"tri": jnp.asarray(tri, BF16), "e64": jnp.asarray(e64, BF16), "cbias": jnp.asarray(cbias)}


def _params(p):
    rows = lambda a: a[:, None, :]
    pad = jnp.zeros((DEPTH, LANES - DT_COPIES * HEADS), F32)
    return {
        **_win_prep(p["w_in"]),
        "b_gate": rows(p["b_gate"]),
        "ln_v_g": rows(p["ln_v_g"]), "ln_v_b": rows(p["ln_v_b"]),
        "w_spatial": p["w_spatial"],
        "bsp_full": jnp.repeat(jnp.swapaxes(p["b_spatial"], 1, 2), LANES, axis=2),
        "wsp0": rows(jnp.repeat(p["w_spatial"][:, :, 0, 0], LANES, axis=1)),
        "bsp0": rows(jnp.repeat(p["b_spatial"][:, :, 0], LANES, axis=1)),
        "conv_w": p["conv_w"], "conv_b": rows(p["conv_b"]),
        "dt_bias3": rows(jnp.concatenate([p["dt_bias"]] * DT_COPIES + [pad], axis=1)),
        "a_log3": rows(jnp.concatenate([p["a_log"]] * DT_COPIES + [pad], axis=1)),
        "dskip_exp": rows(jnp.repeat(p["d_skip"], HEAD_DIM, axis=1)),
        "ssd_norm_w": rows(p["ssd_norm_w"]),
        "w_cm_br": _to_bf16(p["w_cm_br"], 512), "w_ssd_br": _to_bf16(p["w_ssd_br"], 512),
        "w_o": _to_bf16(p["w_o"], 512),
        "ln1_g": rows(p["ln1_g"]), "ln1_b": rows(p["ln1_b"]),
        "w_ffn_gate": _to_bf16(p["w_ffn_gate"], 512), "w_ffn_up": _to_bf16(p["w_ffn_up"], 512),
        "w_ffn_down": _to_bf16(p["w_ffn_down"], D_FF // 4),
        "ln2_g": rows(p["ln2_g"]), "ln2_b": rows(p["ln2_b"]),
    }


def kernel(x_prompt, x_sample, state_ssd, state_conv, c_prompt, c_sample, w_ada, b_ada, w_in, b_gate,
           ln_v_g, ln_v_b, w_spatial, b_spatial, conv_w, conv_b, dt_bias, a_log, d_skip, ssd_norm_w,
           w_cm_br, w_ssd_br, w_o, ln1_g, ln1_b, w_ffn_gate, w_ffn_up, w_ffn_down, ln2_g, ln2_b):
    P = _params(dict(w_in=w_in, b_gate=b_gate, ln_v_g=ln_v_g, ln_v_b=ln_v_b, w_spatial=w_spatial,
                     b_spatial=b_spatial, conv_w=conv_w, conv_b=conv_b, dt_bias=dt_bias, a_log=a_log,
                     d_skip=d_skip, ssd_norm_w=ssd_norm_w, w_cm_br=w_cm_br, w_ssd_br=w_ssd_br, w_o=w_o,
                     ln1_g=ln1_g, ln1_b=ln1_b, w_ffn_gate=w_ffn_gate, w_ffn_up=w_ffn_up,
                     w_ffn_down=w_ffn_down, ln2_g=ln2_g, ln2_b=ln2_b))
    consts = _constants()
    mod, mod_p = _adaln_mod(jnp.concatenate([c_sample, c_prompt], axis=0), w_ada, b_ada)
    mod4 = mod_p.reshape(DEPTH, BATCH, 6, D_MODEL)
    xp = x_prompt
    xs = x_sample.reshape(DEC_BATCH, D_MODEL)
    ssd_p, conv_p, conv_s, v_s = [], [], [], []
    state_in = state_ssd.reshape(DEPTH, DEC_BATCH, D_INNER, STATE)
    cstate = state_conv.reshape(DEPTH, DEC_BATCH, (CONV_K - 1) * CONV_DIM)
    ssd_s = None
    for i in range(DEPTH):
        x1, sp, cp = _prompt_mixer(i, xp, mod4, P, consts)
        xp = _prompt_ffn(i, x1, mod4, P)
        ssd_p.append(sp.reshape(BATCH, HEADS, HEAD_DIM, STATE))
        conv_p.append(cp)
        v, cs, da, xdtt, bm, cm, cmpart, zs, gs, xsd = _sample_pre(i, xs, mod, cstate, P, consts)
        ssd_s, y = _sample_ssd(i, da[:, :HEADS], xdtt, bm, cm, state_in, ssd_s)
        xs = _sample_post(i, xs, mod, y, xsd, zs, gs, cmpart, P)
        conv_s.append(cs.reshape(DEC_BATCH, CONV_K - 1, CONV_DIM))
        v_s.append(v.reshape(DEC_BATCH, 1, D_CM))
    return (xp, xs.reshape(DEC_BATCH, 1, D_MODEL), jnp.stack(ssd_p), jnp.stack(conv_p),
            ssd_s.reshape(DEPTH, DEC_BATCH, HEADS, HEAD_DIM, STATE), jnp.stack(conv_s), jnp.stack(v_s))
```

```python
import functools
import math

import numpy as np
import jax
import jax.numpy as jnp
from jax import lax
from jax.experimental import pallas as pl
from jax.experimental.pallas import tpu as pltpu

D_MODEL = 1024
BATCH = 8
SEQ = 2048
DEPTH = 2
DEC_BATCH = 128
CHUNK = 128
D_CM = D_MODEL
CM_GROUPS = 8
D_INNER = 2048
HEAD_DIM = 64
HEADS = 32
SSD_GROUPS = 4
HEADS_PER_GROUP = HEADS // SSD_GROUPS
STATE = 128
CONV_K = 4
CONV_DIM = D_INNER + 2 * SSD_GROUPS * STATE
D_FF = 2816
ALPHA = (2 * DEPTH) ** 0.25
LN_EPS = 1e-5
REF_DT = 2 * D_CM + D_INNER + CONV_DIM
REF_GATES = REF_DT + HEADS

LANES = 128
DT_COPIES = 3
C_U = 0
C_V = C_U + D_CM
C_Z = C_V + D_CM
C_XBC = C_Z + D_INNER
C_END = C_XBC + CONV_DIM

TILE_L = 256
TILE_F = 512
SSD_BB = 8
VMEM_LIMIT = 56 * 1024 * 1024

F32 = jnp.float32
BF16 = jnp.bfloat16


def _bdot(a, b):
    return jnp.dot(a, b, preferred_element_type=F32)


def _gelu(x):
    return 0.5 * x * (1.0 + lax.erf(x * np.float32(math.sqrt(0.5))))


def _sigmoid(x):
    return 1.0 / (1.0 + jnp.exp(-x))


def _silu(x):
    hx = 0.5 * x
    return hx + hx * jnp.tanh(hx)


def _softplus(x):
    return jnp.maximum(x, 0.0) + jnp.log(1.0 + jnp.exp(-jnp.abs(x)))


def _layer_norm(x, g, b):
    mu = jnp.mean(x, axis=-1, keepdims=True)
    xc = x - mu
    var = jnp.mean(xc * xc, axis=-1, keepdims=True)
    return xc * lax.rsqrt(var + LN_EPS) * g + b


def _rms_norm(x, g):
    return x * lax.rsqrt(jnp.mean(x * x, axis=-1, keepdims=True) + LN_EPS) * g


def _pack3(x):
    hi = x.astype(BF16).astype(F32)
    r1 = x - hi
    mid = r1.astype(BF16).astype(F32)
    r2 = r1 - mid
    lane = lax.broadcasted_iota(jnp.int32, x.shape, 1)
    return jnp.where(lane < HEADS, x, jnp.where(lane < 2 * HEADS, r1, r2)).astype(BF16)


def _const_spec(shape, single=True):
    nd = len(shape)
    kw = {"pipeline_mode": pl.Buffered(1)} if single else {}
    return pl.BlockSpec(shape, lambda *_: (0,) * nd, **kw)


def _layer_spec(shape, layer):
    nd = len(shape)
    return pl.BlockSpec((None,) + tuple(shape), lambda *_: (layer,) + (0,) * nd,
                        pipeline_mode=pl.Buffered(1))


def _cast_kernel(w_ref, o_ref):
    o_ref[...] = w_ref[...].astype(BF16)


def _to_bf16(w, rows):
    depth, r, c = w.shape
    return pl.pallas_call(
        _cast_kernel,
        grid=(depth, r // rows),
        in_specs=[pl.BlockSpec((None, rows, c), lambda l, i: (l, i, 0))],
        out_specs=pl.BlockSpec((None, rows, c), lambda l, i: (l, i, 0)),
        out_shape=jax.ShapeDtypeStruct((depth, r, c), BF16),
        compiler_params=pltpu.CompilerParams(
            dimension_semantics=("parallel", "parallel"), vmem_limit_bytes=VMEM_LIMIT),
        name="cast_bf16",
    )(w)


def _tcast_kernel(w_ref, o_ref):
    o_ref[...] = w_ref[...].T.astype(BF16)


def _tcast_dt_kernel(w_ref, o_ref):
    t = w_ref[...].T
    lane = lax.broadcasted_iota(jnp.int32, t.shape, 1)
    o_ref[...] = jnp.where(lane < HEADS, t, 0.0).astype(BF16)


def _win_prep(w_in, rows=512):
    wt = jnp.swapaxes(w_in, 1, 2)
    params = pltpu.CompilerParams(dimension_semantics=("parallel", "parallel"), vmem_limit_bytes=VMEM_LIMIT)

    def piece(kern, n_out, in_spec, nblk, blk):
        return pl.pallas_call(
            kern, grid=(DEPTH, nblk), in_specs=[in_spec],
            out_specs=pl.BlockSpec((None, D_MODEL, blk), lambda l, i: (l, 0, i)),
            out_shape=jax.ShapeDtypeStruct((DEPTH, D_MODEL, n_out), BF16),
            compiler_params=params, name="w_in_prep")(wt)

    main = piece(_tcast_kernel, C_END, pl.BlockSpec((None, rows, D_MODEL), lambda l, i: (l, i, 0)),
                 C_END // rows, rows)
    gates = piece(lambda w_ref, o_ref: _tcast_kernel(w_ref.at[0], o_ref), 2 * D_MODEL,
                  pl.BlockSpec((pl.Element(1), pl.Element(rows), pl.Element(D_MODEL)),
                               lambda l, i: (l, pl.multiple_of(REF_GATES + i * rows, 8), 0)),
                  2 * D_MODEL // rows, rows)
    dt = piece(_tcast_dt_kernel, LANES,
               pl.BlockSpec((None, LANES, D_MODEL), lambda l, i: (l, REF_DT // LANES, 0)), 1, LANES)
    return {"w_in": main, "w_gates": gates, "w_dt": dt}


def _dt_cols(h, wdt_ref):
    d = _bdot(h, wdt_ref[...])
    out = d
    for k in range(1, DT_COPIES):
        out = out + pltpu.roll(d, k * HEADS, 1)
    return out


def _mod_kernel(c_ref, w_ref, b_ref, os_ref, op_ref):
    s = _silu(c_ref[...])
    res = _bdot(s.astype(BF16), w_ref[...].astype(BF16)) + b_ref[...]
    os_ref[...] = res[:DEC_BATCH]
    op_ref[...] = res[DEC_BATCH:]


def _adaln_mod(c_all, w_ada, b_ada):
    rows = c_all.shape[0]
    nblk = 6
    return pl.pallas_call(
        _mod_kernel,
        grid=(DEPTH, nblk),
        in_specs=[
            pl.BlockSpec((rows, D_MODEL), lambda l, k: (0, 0)),
            pl.BlockSpec((None, D_MODEL, D_MODEL), lambda l, k: (l, 0, k)),
            pl.BlockSpec((None, 1, D_MODEL), lambda l, k: (l, 0, k)),
        ],
        out_specs=[pl.BlockSpec((None, DEC_BATCH, D_MODEL), lambda l, k: (l, 0, k)),
                   pl.BlockSpec((None, BATCH, D_MODEL), lambda l, k: (l, 0, k))],
        out_shape=[jax.ShapeDtypeStruct((DEPTH, DEC_BATCH, nblk * D_MODEL), F32),
                   jax.ShapeDtypeStruct((DEPTH, BATCH, nblk * D_MODEL), F32)],
        compiler_params=pltpu.CompilerParams(
            dimension_semantics=("parallel", "parallel"), vmem_limit_bytes=VMEM_LIMIT),
        name="adaln_mod",
    )(c_all, w_ada, b_ada.reshape(DEPTH, 1, nblk * D_MODEL))


def _cm_branch(u_v, lnvg, lnvb):
    u = _gelu(u_v[:, :D_CM])
    v = _layer_norm(_gelu(u_v[:, D_CM:]), lnvg, lnvb)
    return u, v


def _fold(v):
    s8 = jnp.sum(v.reshape(v.shape[0] // 8, 8, v.shape[1]), axis=0)
    d = s8[:, 0:LANES]
    for k in range(1, v.shape[1] // LANES):
        d = d + s8[:, k * LANES:(k + 1) * LANES]
    return d[0:1, 0:1]


def _after(never, dep, val):
    return jnp.where(never, jnp.broadcast_to(dep, val.shape).astype(val.dtype), val)


def _post_mixer(x, g1, merged_bf16, wo_ref, ln1g, ln1b):
    o = _bdot(merged_bf16, wo_ref[...])
    return _layer_norm(ALPHA * x + g1 * o, ln1g, ln1b)


def _ffn(x1, sh2, sc2, g2, wg_ref, wu_ref, wd_ref, ln2g, ln2b):
    h2 = (x1 * (1.0 + sc2) + sh2).astype(BF16)
    gate = _bdot(h2, wg_ref[...])
    up = _bdot(h2, wu_ref[...])
    act = (_silu(gate) * up).astype(BF16)
    ffn = _bdot(act, wd_ref[...])
    return _layer_norm(ALPHA * x1 + g2 * ffn, ln2g, ln2b)


def _ssd_chunk(xs_c, b_c, c_c, dtseg, dtb, alog, tri, e64, cbias, state_ref, y_ref, r0):
    row = lax.broadcasted_iota(jnp.int32, (CHUNK, LANES), 0)
    lane = lax.broadcasted_iota(jnp.int32, (CHUNK, LANES), 1)
    low_half = lane < HEAD_DIM

    dt = _softplus(dtseg + dtb)
    a = -jnp.exp(alog)
    da = dt * a
    hi = da.astype(BF16)
    r1 = da - hi.astype(F32)
    mid = r1.astype(BF16)
    lo = (r1 - mid.astype(F32)).astype(BF16)
    acum = _bdot(tri, hi) + _bdot(tri, mid) + _bdot(tri, lo)
    shift_t = acum.T - jnp.log(dt.T)
    alast = acum[CHUNK - 1:CHUNK, :]
    w = dt * jnp.exp(alast - acum)
    wexp = _bdot(_pack3(w), e64)
    eexp = _bdot(_pack3(jnp.exp(acum)), e64)
    cd = jnp.broadcast_to(jnp.exp(alast), (8, LANES))
    cdexp = _bdot(_pack3(cd), e64)[0:1, :]
    xs_b = xs_c.astype(BF16)
    zero_b = jnp.zeros((CHUNK, LANES), BF16)

    gw = HEADS_PER_GROUP * HEAD_DIM
    for g in range(SSD_GROUPS):
        gc = g * gw
        bg = b_c[:, g * STATE:(g + 1) * STATE]
        cgb = c_c[:, g * STATE:(g + 1) * STATE].astype(BF16)
        cb = lax.dot_general(cgb, bg.astype(BF16), (((1,), (1,)), ((), ())),
                             preferred_element_type=F32)
        y_off = eexp[:, gc:gc + gw] * _bdot(cgb, state_ref[:, gc:gc + gw].astype(BF16))
        for pr in range(HEADS_PER_GROUP // 2):
            h0 = g * HEADS_PER_GROUP + 2 * pr
            pieces = []
            for h in (h0, h0 + 1):
                seg = jnp.broadcast_to(acum[:, h:h + 1], (CHUNK, LANES)) - shift_t[h:h + 1, :]
                pieces.append((cb * jnp.exp(seg + cbias)).astype(BF16))
            lhs = jnp.concatenate(pieces, axis=1)
            pc = h0 * HEAD_DIM
            xs_pair = xs_b[:, pc:pc + LANES]
            rhs = jnp.concatenate([jnp.where(low_half, xs_pair, zero_b),
                                   jnp.where(low_half, zero_b, xs_pair)], axis=0)
            y_ref[r0:r0 + CHUNK, pc:pc + LANES] = (
                _bdot(lhs, rhs) + y_off[:, pc - gc:pc - gc + LANES])
        xw = (xs_c[:, gc:gc + gw] * wexp[:, gc:gc + gw]).astype(BF16)
        upd = _bdot(bg.T.astype(BF16), xw)
        state_ref[:, gc:gc + gw] = state_ref[:, gc:gc + gw] * cdexp[:, gc:gc + gw] + upd


def _p1_kernel(x_ref, mod_ref, win_ref, wgate_ref, wdt_ref, bg_ref, lnvg_ref, lnvb_ref, wsp_ref, bsp_ref,
               convw_ref, convb_ref, dtb_ref, alog_ref, dskip_ref, normw_ref,
               wcm_ref, wssd_ref, wo_ref, ln1g_ref, ln1b_ref, tri_ref, e64_ref, cbias_ref,
               x1_ref, ssd_ref, conv_ref,
               state_ref, cbuf_ref, y_ref, *, tl):
    j = pl.program_id(1)
    last = pl.num_programs(1) - 1
    nc = tl // CHUNK

    @pl.when(j == 0)
    def _():
        state_ref[...] = jnp.zeros_like(state_ref)
        cbuf_ref[0:8, :] = jnp.zeros((8, CONV_DIM), F32)

    x = x_ref[...]
    sh1 = mod_ref[0:1, :]
    sc1 = mod_ref[1:2, :]
    g1 = mod_ref[2:3, :]
    h = (x * (1.0 + sc1) + sh1).astype(BF16)

    parts = []
    for c in range(nc):
        r0 = c * CHUNK
        cbuf_ref[8 + r0:8 + r0 + CHUNK, :] = _bdot(h[r0:r0 + CHUNK], win_ref[:, C_XBC:C_END])
        acc = convb_ref[...]
        for k in range(CONV_K):
            acc = acc + cbuf_ref[5 + k + r0:5 + k + r0 + CHUNK, :] * convw_ref[k:k + 1, :]
        parts.append(_silu(acc))
    cbuf_ref[5:8, :] = cbuf_ref[tl + 5:tl + 8, :]
    xbc_c = jnp.concatenate(parts, axis=0) if nc > 1 else parts[0]
    xs = xbc_c[:, :D_INNER]
    dtseg = _dt_cols(h, wdt_ref)
    u_v = _bdot(h, win_ref[:, C_U:C_Z])
    z = _bdot(h, win_ref[:, C_Z:C_XBC])
    gcm_logit = _bdot(h, wgate_ref[:, :D_MODEL])
    gssd_logit = _bdot(h, wgate_ref[:, D_MODEL:])

    u, v = _cm_branch(u_v, lnvg_ref[...], lnvb_ref[...])
    vb = v.astype(BF16)
    row = lax.broadcasted_iota(jnp.int32, (CHUNK, CHUNK), 0)
    col = lax.broadcasted_iota(jnp.int32, (CHUNK, CHUNK), 1)
    wmix = [jnp.where(row >= col, wsp_ref[g], 0.0).astype(BF16) for g in range(CM_GROUPS)]

    rows = []
    for c in range(nc):
        cols = []
        for g in range(CM_GROUPS):
            cols.append(_bdot(wmix[g], vb[c * CHUNK:(c + 1) * CHUNK, g * LANES:(g + 1) * LANES]))
        rows.append(jnp.concatenate(cols, axis=1) + bsp_ref[...])
    mix = jnp.concatenate(rows, axis=0) if nc > 1 else rows[0]
    out_cm = (u * mix).astype(BF16)
    merged = _sigmoid(gcm_logit + bg_ref[:, :D_MODEL]) * _bdot(out_cm, wcm_ref[...])

    side_done = _fold(merged) + _fold(gssd_logit)
    for c in range(nc):
        r0 = c * CHUNK
        dtb = dtb_ref[...] if c == 0 else _after(pl.program_id(1) < 0, side_done, dtb_ref[...])
        _ssd_chunk(xs[r0:r0 + CHUNK], xbc_c[r0:r0 + CHUNK, D_INNER:D_INNER + SSD_GROUPS * STATE],
                   xbc_c[r0:r0 + CHUNK, D_INNER + SSD_GROUPS * STATE:], dtseg[r0:r0 + CHUNK],
                   dtb, alog_ref[...], tri_ref[...], e64_ref[...], cbias_ref[...],
                   state_ref, y_ref, r0)

    yz = (y_ref[...] + xs * dskip_ref[...]) * _silu(z)
    yn = _rms_norm(yz, normw_ref[...]).astype(BF16)
    merged = merged + _sigmoid(gssd_logit + bg_ref[:, D_MODEL:]) * _bdot(yn, wssd_ref[...])
    x1_ref[...] = _post_mixer(x, g1, merged.astype(BF16), wo_ref, ln1g_ref[...], ln1b_ref[...])

    @pl.when(j == last)
    def _():
        conv_ref[...] = cbuf_ref[5:8, :]
        ssd_ref[...] = state_ref[...].T


def _mod_prompt_spec(layer):
    return pl.BlockSpec((None, None, 6, D_MODEL), lambda b, j: (layer, b, 0, 0))


def _prompt_mixer(layer, x, mod4, P, consts, stacked):
    tl = TILE_L
    nj = SEQ // tl
    ls = lambda *shape: _layer_spec(shape, layer)
    in_specs = [
        pl.BlockSpec((None, tl, D_MODEL), lambda b, j: (b, j, 0)),
        _mod_prompt_spec(layer),
        ls(D_MODEL, C_END), ls(D_MODEL, 2 * D_MODEL), ls(D_MODEL, LANES),
        ls(1, 2 * D_MODEL), ls(1, D_CM), ls(1, D_CM),
        ls(CM_GROUPS, CHUNK, CHUNK), ls(CHUNK, D_CM),
        ls(CONV_K, CONV_DIM), ls(1, CONV_DIM), ls(1, LANES), ls(1, LANES),
        ls(1, D_INNER), ls(1, D_INNER),
        ls(D_CM, D_MODEL), ls(D_INNER, D_MODEL), ls(D_MODEL, D_MODEL),
        ls(1, D_MODEL), ls(1, D_MODEL),
        _const_spec((CHUNK, CHUNK)), _const_spec((LANES, D_INNER)), _const_spec((CHUNK, CHUNK)),
    ]
    out_specs = [
        pl.BlockSpec((None, tl, D_MODEL), lambda b, j: (b, j, 0)),
        pl.BlockSpec((None, None, D_INNER, STATE), lambda b, j: (layer, b, 0, 0)),
        pl.BlockSpec((None, None, CONV_K - 1, CONV_DIM), lambda b, j: (layer, b, 0, 0)),
    ]
    out_shape = [
        jax.ShapeDtypeStruct((BATCH, SEQ, D_MODEL), F32),
        jax.ShapeDtypeStruct((DEPTH, BATCH, D_INNER, STATE), F32),
        jax.ShapeDtypeStruct((DEPTH, BATCH, CONV_K - 1, CONV_DIM), F32),
    ]
    n_in = len(in_specs)
    extra = () if stacked is None else tuple(stacked)
    in_specs = in_specs + [pl.BlockSpec(memory_space=pl.ANY)] * len(extra)
    body = functools.partial(_p1_kernel, tl=tl)
    return pl.pallas_call(
        lambda *refs: body(*refs[:n_in], *refs[n_in + len(extra):]),
        grid=(BATCH, nj),
        in_specs=in_specs, out_specs=out_specs, out_shape=out_shape,
        input_output_aliases={n_in + k: 1 + k for k in range(len(extra))},
        scratch_shapes=[pltpu.VMEM((STATE, D_INNER), F32),
                        pltpu.VMEM((tl + 8, CONV_DIM), F32),
                        pltpu.VMEM((tl, D_INNER), F32)],
        compiler_params=pltpu.CompilerParams(
            dimension_semantics=("parallel", "arbitrary"), vmem_limit_bytes=VMEM_LIMIT),
        name="prompt_mixer",
    )(x, mod4, P["w_in"], P["w_gates"], P["w_dt"], P["b_gate"], P["ln_v_g"], P["ln_v_b"], P["w_spatial"], P["bsp_full"],
      P["conv_w"], P["conv_b"], P["dt_bias3"], P["a_log3"], P["dskip_exp"], P["ssd_norm_w"],
      P["w_cm_br"], P["w_ssd_br"], P["w_o"], P["ln1_g"], P["ln1_b"],
      consts["tri"], consts["e64"], consts["cbias"], *extra)


def _p2_kernel(x_ref, mod_ref, wg_ref, wu_ref, wd_ref, ln2g_ref, ln2b_ref, o_ref):
    o_ref[...] = _ffn(x_ref[...], mod_ref[3:4, :], mod_ref[4:5, :], mod_ref[5:6, :],
                      wg_ref, wu_ref, wd_ref, ln2g_ref[...], ln2b_ref[...])


def _prompt_ffn(layer, x1, mod4, P):
    tf = TILE_F
    ls = lambda *shape: _layer_spec(shape, layer)
    return pl.pallas_call(
        _p2_kernel,
        grid=(BATCH, SEQ // tf),
        in_specs=[
            pl.BlockSpec((None, tf, D_MODEL), lambda b, j: (b, j, 0)),
            _mod_prompt_spec(layer),
            ls(D_MODEL, D_FF), ls(D_MODEL, D_FF), ls(D_FF, D_MODEL),
            ls(1, D_MODEL), ls(1, D_MODEL),
        ],
        out_specs=pl.BlockSpec((None, tf, D_MODEL), lambda b, j: (b, j, 0)),
        out_shape=jax.ShapeDtypeStruct((BATCH, SEQ, D_MODEL), F32),
        compiler_params=pltpu.CompilerParams(
            dimension_semantics=("parallel", "parallel"), vmem_limit_bytes=VMEM_LIMIT),
        name="prompt_ffn",
    )(x1, mod4, P["w_ffn_gate"], P["w_ffn_up"], P["w_ffn_down"], P["ln2_g"], P["ln2_b"])


def _s_pre_kernel(x_ref, mod_ref, win_ref, wgate_ref, wdt_ref, bg_ref, lnvg_ref, lnvb_ref, wsp0_ref, bsp0_ref,
                  convw_ref, convb_ref, cstate_ref, dtb_ref, alog_ref, dskip_ref, wcm_ref, e64_ref,
                  v_ref, conv_ref, da_ref, xdtt_ref, b_ref, c_ref, cmpart_ref, zs_ref, gs_ref, xsd_ref):
    x = x_ref[...]
    sh1 = mod_ref[:, 0:D_MODEL]
    sc1 = mod_ref[:, D_MODEL:2 * D_MODEL]
    h = (x * (1.0 + sc1) + sh1).astype(BF16)

    u, v = _cm_branch(_bdot(h, win_ref[:, C_U:C_Z]), lnvg_ref[...], lnvb_ref[...])
    v_ref[...] = v
    out_cm = (u * (wsp0_ref[...] * v + bsp0_ref[...])).astype(BF16)
    gate_cm = _sigmoid(_bdot(h, wgate_ref[:, :D_MODEL]) + bg_ref[:, :D_MODEL])
    cmpart_ref[...] = gate_cm * _bdot(out_cm, wcm_ref[...])

    xbc = _bdot(h, win_ref[:, C_XBC:C_END])
    acc = convb_ref[...]
    for k in range(CONV_K - 1):
        acc = acc + cstate_ref[:, k * CONV_DIM:(k + 1) * CONV_DIM] * convw_ref[k:k + 1, :]
    acc = acc + xbc * convw_ref[CONV_K - 1:CONV_K, :]
    conv_ref[:, 0:(CONV_K - 2) * CONV_DIM] = cstate_ref[:, CONV_DIM:(CONV_K - 1) * CONV_DIM]
    conv_ref[:, (CONV_K - 2) * CONV_DIM:] = xbc
    xbc_c = _silu(acc)
    xs = xbc_c[:, :D_INNER]
    b_ref[...] = xbc_c[:, D_INNER:D_INNER + SSD_GROUPS * STATE]
    c_ref[...] = xbc_c[:, D_INNER + SSD_GROUPS * STATE:]
    dt = _softplus(_dt_cols(h, wdt_ref) + dtb_ref[...])
    da_ref[...] = jnp.exp(dt * (-jnp.exp(alog_ref[...])))
    xdt = xs * _bdot(_pack3(dt), e64_ref[...])
    xdtt_ref[...] = xdt.T.astype(BF16)
    xsd_ref[...] = xs * dskip_ref[...]
    zs_ref[...] = _silu(_bdot(h, win_ref[:, C_Z:C_XBC]))
    gs_ref[...] = _sigmoid(_bdot(h, wgate_ref[:, D_MODEL:]) + bg_ref[:, D_MODEL:])


def _sample_pre(layer, x, mod, cstate, P, consts, stacked):
    n = DEC_BATCH
    ls = lambda *shape: _layer_spec(shape, layer)
    in_specs = [
        _const_spec((n, D_MODEL)), ls(n, 6 * D_MODEL),
        ls(D_MODEL, C_END), ls(D_MODEL, 2 * D_MODEL), ls(D_MODEL, LANES),
        ls(1, 2 * D_MODEL), ls(1, D_CM), ls(1, D_CM), ls(1, D_CM), ls(1, D_CM),
        ls(CONV_K, CONV_DIM), ls(1, CONV_DIM), ls(n, (CONV_K - 1) * CONV_DIM),
        ls(1, LANES), ls(1, LANES), ls(1, D_INNER), ls(D_CM, D_MODEL), _const_spec((LANES, D_INNER)),
    ]
    outs = [((n, D_CM), F32), ((n, (CONV_K - 1) * CONV_DIM), F32), ((n, LANES), F32),
            ((D_INNER, n), BF16), ((n, SSD_GROUPS * STATE), F32), ((n, SSD_GROUPS * STATE), F32),
            ((n, D_MODEL), F32), ((n, D_INNER), F32), ((n, D_MODEL), F32), ((n, D_INNER), F32)]
    n_in = len(in_specs)
    extra = () if stacked is None else tuple(stacked)
    in_specs = in_specs + [pl.BlockSpec(memory_space=pl.ANY)] * len(extra)
    n_stacked = 2
    lspec = lambda shape: pl.BlockSpec((None,) + shape, lambda *_: (layer,) + (0,) * len(shape))
    return pl.pallas_call(
        lambda *refs: _s_pre_kernel(*refs[:n_in], *refs[n_in + len(extra):]),
        grid=(1,),
        in_specs=in_specs,
        out_specs=[lspec(s) if k < n_stacked else _const_spec(s, single=False) for k, (s, _) in enumerate(outs)],
        out_shape=[jax.ShapeDtypeStruct(((DEPTH,) + s) if k < n_stacked else s, d)
                   for k, (s, d) in enumerate(outs)],
        input_output_aliases={n_in + k: k for k in range(len(extra))},
        compiler_params=pltpu.CompilerParams(
            dimension_semantics=("arbitrary",), vmem_limit_bytes=VMEM_LIMIT),
        name="sample_pre",
    )(x, mod, P["w_in"], P["w_gates"], P["w_dt"], P["b_gate"], P["ln_v_g"], P["ln_v_b"], P["wsp0"], P["bsp0"],
      P["conv_w"], P["conv_b"], cstate, P["dt_bias3"], P["a_log3"], P["dskip_exp"], P["w_cm_br"],
      consts["e64"], *extra)


def _s_ssd_kernel(da_ref, xdtt_ref, b_ref, c_ref, st_ref, *rest):
    sto_ref, y_ref = rest[-2:]
    i = pl.program_id(0)

    @pl.when(i == 0)
    def _():
        y_ref[...] = jnp.zeros_like(y_ref)

    rows = lax.broadcasted_iota(jnp.int32, (DEC_BATCH, STATE), 0)
    rows8 = lax.broadcasted_iota(jnp.int32, (8, STATE), 0)
    gw = HEADS_PER_GROUP * HEAD_DIM
    for q in range(SSD_BB):
        b = i * SSD_BB + q
        b8 = pl.multiple_of((b // 8) * 8, 8)
        for g in range(SSD_GROUPS):
            bg = jnp.where(rows == b, b_ref[:, g * STATE:(g + 1) * STATE], 0.0).astype(BF16)
            xb = _bdot(xdtt_ref[g * gw:(g + 1) * gw, :], bg)
            blocks = []
            for hh in range(HEADS_PER_GROUP):
                hd = g * HEADS_PER_GROUP + hh
                r = hd * HEAD_DIM
                blk = st_ref[q, r:r + HEAD_DIM, :] * da_ref[b, hd] + xb[hh * HEAD_DIM:(hh + 1) * HEAD_DIM, :]
                sto_ref[q, r:r + HEAD_DIM, :] = blk
                blocks.append(blk.astype(BF16))
            newg = jnp.concatenate(blocks, axis=0)
            cg = jnp.where(rows8 == b - b8, c_ref[pl.ds(b8, 8), g * STATE:(g + 1) * STATE], 0.0).astype(BF16)
            yg = lax.dot_general(cg, newg, (((1,), (1,)), ((), ())), preferred_element_type=F32)
            y_ref[pl.ds(b8, 8), g * gw:(g + 1) * gw] += yg


def _sample_ssd(layer, da, xdtt, bm, cm, state, stacked):
    n = DEC_BATCH
    in_specs = [
        pl.BlockSpec((D_INNER, n), lambda i, da: (0, 0)),
        pl.BlockSpec((n, SSD_GROUPS * STATE), lambda i, da: (0, 0)),
        pl.BlockSpec((n, SSD_GROUPS * STATE), lambda i, da: (0, 0)),
        pl.BlockSpec((None, SSD_BB, D_INNER, STATE), lambda i, da: (layer, i, 0, 0)),
    ]
    args = [da, xdtt, bm, cm, state]
    aliases = {}
    if stacked is not None:
        in_specs.append(pl.BlockSpec(memory_space=pl.ANY))
        args.append(stacked)
        aliases = {len(args) - 1: 0}
    grid_spec = pltpu.PrefetchScalarGridSpec(
        num_scalar_prefetch=1,
        grid=(n // SSD_BB,),
        in_specs=in_specs,
        out_specs=[
            pl.BlockSpec((None, SSD_BB, D_INNER, STATE), lambda i, da: (layer, i, 0, 0)),
            pl.BlockSpec((n, D_INNER), lambda i, da: (0, 0)),
        ],
    )
    return pl.pallas_call(
        _s_ssd_kernel,
        grid_spec=grid_spec,
        out_shape=[jax.ShapeDtypeStruct((DEPTH, n, D_INNER, STATE), F32),
                   jax.ShapeDtypeStruct((n, D_INNER), F32)],
        input_output_aliases=aliases,
        compiler_params=pltpu.CompilerParams(
            dimension_semantics=("arbitrary",), vmem_limit_bytes=VMEM_LIMIT),
        name="sample_ssd",
    )(*args)


def _s_post_kernel(x_ref, mod_ref, y_ref, xsd_ref, zs_ref, gs_ref, cmpart_ref, normw_ref,
                   wssd_ref, wo_ref, ln1g_ref, ln1b_ref, wg_ref, wu_ref, wd_ref, ln2g_ref, ln2b_ref, o_ref):
    md = lambda k: mod_ref[:, k * D_MODEL:(k + 1) * D_MODEL]
    yn = _rms_norm((y_ref[...] + xsd_ref[...]) * zs_ref[...], normw_ref[...]).astype(BF16)
    merged = cmpart_ref[...] + gs_ref[...] * _bdot(yn, wssd_ref[...])
    x1 = _post_mixer(x_ref[...], md(2), merged.astype(BF16), wo_ref, ln1g_ref[...], ln1b_ref[...])
    o_ref[...] = _ffn(x1, md(3), md(4), md(5), wg_ref, wu_ref, wd_ref, ln2g_ref[...], ln2b_ref[...])


def _sample_post(layer, x, mod, y, xsd, zs, gs, cmpart, P):
    n = DEC_BATCH
    ls = lambda *shape: _layer_spec(shape, layer)
    cs = _const_spec
    in_specs = [
        cs((n, D_MODEL)), ls(n, 6 * D_MODEL), cs((n, D_INNER)), cs((n, D_INNER)), cs((n, D_INNER)),
        cs((n, D_MODEL)), cs((n, D_MODEL)), ls(1, D_INNER), ls(D_INNER, D_MODEL), ls(D_MODEL, D_MODEL),
        ls(1, D_MODEL), ls(1, D_MODEL), ls(D_MODEL, D_FF), ls(D_MODEL, D_FF), ls(D_FF, D_MODEL),
        ls(1, D_MODEL), ls(1, D_MODEL),
    ]
    return pl.pallas_call(
        _s_post_kernel,
        grid=(1,),
        in_specs=in_specs,
        out_specs=_const_spec((n, D_MODEL), single=False),
        out_shape=jax.ShapeDtypeStruct((n, D_MODEL), F32),
        compiler_params=pltpu.CompilerParams(
            dimension_semantics=("arbitrary",), vmem_limit_bytes=VMEM_LIMIT),
        name="sample_post",
    )(x, mod, y, xsd, zs, gs, cmpart, P["ssd_norm_w"], P["w_ssd_br"], P["w_o"], P["ln1_g"], P["ln1_b"],
      P["w_ffn_gate"], P["w_ffn_up"], P["w_ffn_down"], P["ln2_g"], P["ln2_b"])


def _constants():
    tri = np.tril(np.ones((CHUNK, CHUNK), np.float32))
    e64 = np.zeros((LANES, D_INNER), np.float32)
    for r in range(DT_COPIES * HEADS):
        hd = r % HEADS
        e64[r, hd * HEAD_DIM:(hd + 1) * HEAD_DIM] = 1.0
    cbias = np.where(tri > 0, 0.0, -np.inf).astype(np.float32)
    return {"tri": jnp.asarray(tri, BF16), "e64": jnp.asarray(e64, BF16), "cbias": jnp.asarray(cbias)}


def _params(p):
    rows = lambda a: a[:, None, :]
    pad = jnp.zeros((DEPTH, LANES - DT_COPIES * HEADS), F32)
    return {
        **_win_prep(p["w_in"]),
        "b_gate": rows(p["b_gate"]),
        "ln_v_g": rows(p["ln_v_g"]), "ln_v_b": rows(p["ln_v_b"]),
        "w_spatial": p["w_spatial"],
        "bsp_full": jnp.repeat(jnp.swapaxes(p["b_spatial"], 1, 2), LANES, axis=2),
        "wsp0": rows(jnp.repeat(p["w_spatial"][:, :, 0, 0], LANES, axis=1)),
        "bsp0": rows(jnp.repeat(p["b_spatial"][:, :, 0], LANES, axis=1)),
        "conv_w": p["conv_w"], "conv_b": rows(p["conv_b"]),
        "dt_bias3": rows(jnp.concatenate([p["dt_bias"]] * DT_COPIES + [pad], axis=1)),
        "a_log3": rows(jnp.concatenate([p["a_log"]] * DT_COPIES + [pad], axis=1)),
        "dskip_exp": rows(jnp.repeat(p["d_skip"], HEAD_DIM, axis=1)),
        "ssd_norm_w": rows(p["ssd_norm_w"]),
        "w_cm_br": _to_bf16(p["w_cm_br"], 512), "w_ssd_br": _to_bf16(p["w_ssd_br"], 512),
        "w_o": _to_bf16(p["w_o"], 512),
        "ln1_g": rows(p["ln1_g"]), "ln1_b": rows(p["ln1_b"]),
        "w_ffn_gate": _to_bf16(p["w_ffn_gate"], 512), "w_ffn_up": _to_bf16(p["w_ffn_up"], 512),
        "w_ffn_down": _to_bf16(p["w_ffn_down"], D_FF // 4),
        "ln2_g": rows(p["ln2_g"]), "ln2_b": rows(p["ln2_b"]),
    }


def kernel(x_prompt, x_sample, state_ssd, state_conv, c_prompt, c_sample, w_ada, b_ada, w_in, b_gate,
           ln_v_g, ln_v_b, w_spatial, b_spatial, conv_w, conv_b, dt_bias, a_log, d_skip, ssd_norm_w,
           w_cm_br, w_ssd_br, w_o, ln1_g, ln1_b, w_ffn_gate, w_ffn_up, w_ffn_down, ln2_g, ln2_b):
    P = _params(dict(w_in=w_in, b_gate=b_gate, ln_v_g=ln_v_g, ln_v_b=ln_v_b, w_spatial=w_spatial,
                     b_spatial=b_spatial, conv_w=conv_w, conv_b=conv_b, dt_bias=dt_bias, a_log=a_log,
                     d_skip=d_skip, ssd_norm_w=ssd_norm_w, w_cm_br=w_cm_br, w_ssd_br=w_ssd_br, w_o=w_o,
                     ln1_g=ln1_g, ln1_b=ln1_b, w_ffn_gate=w_ffn_gate, w_ffn_up=w_ffn_up,
                     w_ffn_down=w_ffn_down, ln2_g=ln2_g, ln2_b=ln2_b))
    consts = _constants()
    mod, mod_p = _adaln_mod(jnp.concatenate([c_sample, c_prompt], axis=0), w_ada, b_ada)
    mod4 = mod_p.reshape(DEPTH, BATCH, 6, D_MODEL)
    xp = x_prompt
    xs = x_sample.reshape(DEC_BATCH, D_MODEL)
    ssd_p = conv_p = conv_s = v_s = None
    state_in = state_ssd.reshape(DEPTH, DEC_BATCH, D_INNER, STATE)
    cstate = state_conv.reshape(DEPTH, DEC_BATCH, (CONV_K - 1) * CONV_DIM)
    ssd_s = None
    for i in range(DEPTH):
        x1, ssd_p, conv_p = _prompt_mixer(i, xp, mod4, P, consts, None if i == 0 else (ssd_p, conv_p))
        xp = _prompt_ffn(i, x1, mod4, P)
        v_s, conv_s, da, xdtt, bm, cm, cmpart, zs, gs, xsd = _sample_pre(
            i, xs, mod, cstate, P, consts, None if i == 0 else (v_s, conv_s))
        ssd_s, y = _sample_ssd(i, da[:, :HEADS], xdtt, bm, cm, state_in, ssd_s)
        xs = _sample_post(i, xs, mod, y, xsd, zs, gs, cmpart, P)
    return (xp, xs.reshape(DEC_BATCH, 1, D_MODEL), ssd_p.reshape(DEPTH, BATCH, HEADS, HEAD_DIM, STATE), conv_p,
            ssd_s.reshape(DEPTH, DEC_BATCH, HEADS, HEAD_DIM, STATE),
            conv_s.reshape(DEPTH, DEC_BATCH, CONV_K - 1, CONV_DIM), v_s.reshape(DEPTH, DEC_BATCH, 1, D_CM))
```
